```python
import jax, jax.numpy as jnp
from jax import lax
import numpy as np

D_MODEL = 1024
BATCH = 8
SEQ = 4096
DEPTH = 4

CHUNK = 64
Q_BLOCK = 128
D_PLE = 256
EPS = 1e-6
SB_HEADS = 8
SB_HEAD_DIM = 64
SB_WIDTH = SB_HEADS * SB_HEAD_DIM
MLA_HEADS = 8
MLA_NOPE = 64
MLA_ROPE = 32
MLA_QK = MLA_NOPE + MLA_ROPE
MLA_V = 64
MLA_Q_RANK = 384
MLA_KV_RANK = 256
MLA_WIDTH = MLA_HEADS * MLA_V
MIX_WIDTH = SB_WIDTH + MLA_WIDTH
ROPE_THETA = 10000.0
SPLITS = [SB_WIDTH, 2 * SB_WIDTH, 3 * SB_WIDTH,
          3 * SB_WIDTH + MLA_Q_RANK,
          3 * SB_WIDTH + MLA_Q_RANK + MLA_KV_RANK]
IN_COLS = 3 * SB_WIDTH + MLA_Q_RANK + MLA_KV_RANK + MLA_ROPE
N_GROUPS = 4
EXPERTS_PER_GROUP = 8
N_EXPERTS = N_GROUPS * EXPERTS_PER_GROUP
TOP_K = 2
D_EXPERT = 256

kernel_name = "hybrid_sb_mla_hmoe_ple_trunk"


def rmsnorm(x, g):
    xf = x.astype(jnp.float32)
    y = xf * lax.rsqrt(jnp.mean(xf * xf, axis=-1, keepdims=True) + EPS)
    return (y * g.astype(jnp.float32)).astype(x.dtype)


def apply_rope(x, cos, sin):
    half = x.shape[-1] // 2
    x1, x2 = x[..., :half], x[..., half:]
    return jnp.concatenate([x1 * cos - x2 * sin, x2 * cos + x1 * sin], axis=-1)


def split_heads(t, n):
    b, s, _ = t.shape
    return t.reshape(b, s, n, -1).transpose(0, 2, 1, 3)


def merge_heads(t):
    b, h, s, d = t.shape
    return t.transpose(0, 2, 1, 3).reshape(b, s, h * d)


def to_blocks(t):
    b, h, s, d = t.shape
    return t.reshape(b, h, s // Q_BLOCK, Q_BLOCK, d).transpose(2, 0, 1, 3, 4)


def from_blocks(t):
    nb, b, h, qb, d = t.shape
    return t.transpose(1, 2, 0, 3, 4).reshape(b, h, nb * qb, d)


def stick_breaking_attention(q, k, v):
    s_len, dh = q.shape[2], q.shape[3]
    scale = dh ** -0.5
    key_idx = jnp.arange(s_len)

    def block(args):
        q_blk, start = args
        z = jnp.einsum("bhqd,bhkd->bhqk", q_blk, k).astype(jnp.float32) * scale
        q_idx = start + jnp.arange(Q_BLOCK)
        mask = key_idx[None, :] < q_idx[:, None]
        log_not = jnp.where(mask, jax.nn.log_sigmoid(-z), 0.0)
        later = lax.cumsum(log_not, axis=3, reverse=True) - log_not
        w = jnp.where(mask, jnp.exp(jax.nn.log_sigmoid(z) + later), 0.0)
        return jnp.einsum("bhqk,bhkd->bhqd", w.astype(v.dtype), v)

    starts = jnp.arange(s_len // Q_BLOCK) * Q_BLOCK
    return from_blocks(lax.map(block, (to_blocks(q), starts)))


def chunk_causal_softmax_attention(q, k, v):
    s_len, dh = q.shape[2], q.shape[3]
    scale = dh ** -0.5
    key_chunk = jnp.arange(s_len) // CHUNK

    def block(args):
        q_blk, start = args
        sc = jnp.einsum("bhqd,bhkd->bhqk", q_blk, k).astype(jnp.float32) * scale
        q_chunk = (start + jnp.arange(Q_BLOCK)) // CHUNK
        mask = key_chunk[None, :] <= q_chunk[:, None]
        pr = jax.nn.softmax(jnp.where(mask, sc, -jnp.inf), axis=-1)
        return jnp.einsum("bhqk,bhkd->bhqd", pr.astype(v.dtype), v)

    starts = jnp.arange(s_len // Q_BLOCK) * Q_BLOCK
    return from_blocks(lax.map(block, (to_blocks(q), starts)))


def hier_moe(x, w_rg, b_rg, w_re, b_re, w_gate, w_up, w_down):
    b, s, d = x.shape
    t = x.reshape(-1, d)
    tf = t.astype(jnp.float32)
    group_p = jax.nn.softmax(tf @ w_rg.astype(jnp.float32) + b_rg.astype(jnp.float32), axis=-1)
    gp, g = lax.top_k(group_p, 1)
    e_logits = (tf @ w_re.astype(jnp.float32) + b_re.astype(jnp.float32)).reshape(-1, N_GROUPS, EXPERTS_PER_GROUP)
    e_logits = jnp.take_along_axis(e_logits, g[:, :, None], axis=1)[:, 0]
    top_l, top_i = lax.top_k(e_logits, TOP_K)
    top_w = jax.nn.softmax(top_l, axis=-1) * gp
    expert_id = g * EXPERTS_PER_GROUP + top_i
    combine = jnp.sum(jax.nn.one_hot(expert_id, N_EXPERTS, dtype=jnp.float32) * top_w[..., None], axis=1)
    y = jnp.zeros_like(tf)
    for e in range(N_EXPERTS):
        hid = jax.nn.silu(t @ w_gate[e]) * (t @ w_up[e])
        y = y + combine[:, e:e + 1] * (hid @ w_down[e]).astype(jnp.float32)
    return y.astype(x.dtype).reshape(b, s, d)


def setup_inputs(seed: int = 0) -> dict:
    key = jax.random.key(seed)
    ks = iter(jax.random.split(key, 32))
    f32 = jnp.float32

    def w(shape, fan_in):
        return jax.random.normal(next(ks), shape, f32) * (fan_in ** -0.5)

    def gain(shape):
        return 1.0 + 0.02 * jax.random.normal(next(ks), shape, f32)

    x = jax.random.normal(next(ks), (BATCH, SEQ, D_MODEL), f32)
    p = jax.random.normal(next(ks), (DEPTH, BATCH, SEQ, D_PLE), f32)
    offsets = jax.random.randint(next(ks), (BATCH,), 0, 1000, dtype=jnp.int32) * CHUNK
    positions = (offsets[:, None] + jnp.arange(SEQ, dtype=jnp.int32)[None, :]).astype(jnp.int32)
    return {
        "x": x,
        "p": p,
        "positions": positions,
        "g_mix": gain((DEPTH, D_MODEL)),
        "w_in": w((DEPTH, D_MODEL, IN_COLS), D_MODEL),
        "g_cq": gain((DEPTH, MLA_Q_RANK)),
        "w_uq": w((DEPTH, MLA_Q_RANK, MLA_HEADS * MLA_QK), MLA_Q_RANK),
        "g_ckv": gain((DEPTH, MLA_KV_RANK)),
        "w_ukv": w((DEPTH, MLA_KV_RANK, MLA_HEADS * (MLA_NOPE + MLA_V)), MLA_KV_RANK),
        "g_osb": gain((DEPTH, SB_WIDTH)),
        "g_omla": gain((DEPTH, MLA_WIDTH)),
        "w_out": w((DEPTH, MIX_WIDTH, D_MODEL), MIX_WIDTH),
        "g_moe": gain((DEPTH, D_MODEL)),
        "w_rg": w((DEPTH, D_MODEL, N_GROUPS), D_MODEL),
        "b_rg": 0.01 * jax.random.normal(next(ks), (DEPTH, N_GROUPS), f32),
        "w_re": w((DEPTH, D_MODEL, N_EXPERTS), D_MODEL),
        "b_re": 0.01 * jax.random.normal(next(ks), (DEPTH, N_EXPERTS), f32),
        "w_gate": w((DEPTH, N_EXPERTS, D_MODEL, D_EXPERT), D_MODEL),
        "w_up": w((DEPTH, N_EXPERTS, D_MODEL, D_EXPERT), D_MODEL),
        "w_down": w((DEPTH, N_EXPERTS, D_EXPERT, D_MODEL), D_EXPERT),
        "g_ple": gain((DEPTH, D_MODEL)),
        "w_pg": w((DEPTH, D_MODEL, D_MODEL), D_MODEL),
        "w_pe": w((DEPTH, D_PLE, D_MODEL), D_PLE),
        "g_final": gain((D_MODEL,)),
    }


def reference(x, p, positions, g_mix, w_in, g_cq, w_uq, g_ckv, w_ukv, g_osb, g_omla,
              w_out, g_moe, w_rg, b_rg, w_re, b_re, w_gate, w_up, w_down,
              g_ple, w_pg, w_pe, g_final):
    b, s, _ = x.shape
    inv_freq = ROPE_THETA ** (-jnp.arange(0, MLA_ROPE, 2, dtype=jnp.float32) / MLA_ROPE)
    ang = positions.astype(jnp.float32)[..., None] * inv_freq
    cos = jnp.cos(ang).astype(x.dtype)
    sin = jnp.sin(ang).astype(x.dtype)

    h = x
    for i in range(DEPTH):
        hn = rmsnorm(h, g_mix[i])
        proj = hn @ w_in[i]
        q_sb, k_sb, v_sb, c_q, c_kv, k_r = jnp.split(proj, SPLITS, axis=-1)

        o_sb = merge_heads(stick_breaking_attention(
            split_heads(q_sb, SB_HEADS), split_heads(k_sb, SB_HEADS), split_heads(v_sb, SB_HEADS)))

        q = (rmsnorm(c_q, g_cq[i]) @ w_uq[i]).reshape(b, s, MLA_HEADS, MLA_QK)
        q_nope, q_rope = q[..., :MLA_NOPE], q[..., MLA_NOPE:]
        q_rope = apply_rope(q_rope, cos[:, :, None, :], sin[:, :, None, :])
        kv = (rmsnorm(c_kv, g_ckv[i]) @ w_ukv[i]).reshape(b, s, MLA_HEADS, MLA_NOPE + MLA_V)
        k_nope, v_mla = kv[..., :MLA_NOPE], kv[..., MLA_NOPE:]
        k_rope = jnp.broadcast_to(apply_rope(k_r, cos, sin)[:, :, None, :], (b, s, MLA_HEADS, MLA_ROPE))
        q_full = jnp.concatenate([q_nope, q_rope], axis=-1).transpose(0, 2, 1, 3)
        k_full = jnp.concatenate([k_nope, k_rope], axis=-1).transpose(0, 2, 1, 3)
        o_mla = merge_heads(chunk_causal_softmax_attention(q_full, k_full, v_mla.transpose(0, 2, 1, 3)))

        mixed = jnp.concatenate([rmsnorm(o_sb, g_osb[i]), rmsnorm(o_mla, g_omla[i])], axis=-1)
        h = h + mixed @ w_out[i]

        h = h + hier_moe(rmsnorm(h, g_moe[i]), w_rg[i], b_rg[i], w_re[i], b_re[i],
                         w_gate[i], w_up[i], w_down[i])

        gate = jax.nn.sigmoid((rmsnorm(h, g_ple[i]) @ w_pg[i]).astype(jnp.float32))
        h = h + (gate * (p[i] @ w_pe[i]).astype(jnp.float32)).astype(h.dtype)

    return rmsnorm(h, g_final)
```

```python
import functools

import jax
import jax.numpy as jnp
from jax import lax
from jax.experimental import pallas as pl
from jax.experimental.pallas import tpu as pltpu

F32 = jnp.float32
BF16 = jnp.bfloat16

D_MODEL = 1024
CHUNK = 64
D_PLE = 256
EPS = 1e-6
SB_HEADS = 8
SB_HEAD_DIM = 64
SB_WIDTH = SB_HEADS * SB_HEAD_DIM
MLA_HEADS = 8
MLA_NOPE = 64
MLA_ROPE = 32
MLA_QK = MLA_NOPE + MLA_ROPE
MLA_V = 64
MLA_Q_RANK = 384
MLA_KV_RANK = 256
MLA_WIDTH = MLA_HEADS * MLA_V
ROPE_THETA = 10000.0
N_GROUPS = 4
EXPERTS_PER_GROUP = 8
N_EXPERTS = N_GROUPS * EXPERTS_PER_GROUP
D_EXPERT = 256

LANES = 128
HEAD_PAIRS = SB_HEADS // 2
ROUTER_LANES = LANES
TOKEN_TILE = 512
ATTN_BLOCK = 256
VMEM_LIMIT = 48 * 1024 * 1024


def _params(*sem):
    return pltpu.CompilerParams(dimension_semantics=sem, vmem_limit_bytes=VMEM_LIMIT)


def _rms(x, g):
    return x * lax.rsqrt(jnp.mean(x * x, axis=-1, keepdims=True) + EPS) * g


def _dot(a, b):
    return jnp.dot(a, b, preferred_element_type=F32)


def _dot_t(a, b):
    return lax.dot_general(a, b, (((1,), (1,)), ((), ())), preferred_element_type=F32)


def _mixer_in_kernel(x_ref, g_ref, cos_ref, sin_ref, gcq_ref, gckv_ref, w1_ref, wqn_ref,
                     wqra_ref, wqrb_ref, wkvk_ref, wkvv_ref,
                     qsb_ref, ksb_ref, vsb_ref, qn_ref, qr_ref, kn_ref, kr_ref, vm_ref):
    xn = _rms(x_ref[...], g_ref[...]).astype(BF16)
    c0 = 0
    qsb_ref[...] = _dot(xn, w1_ref[:, c0:c0 + SB_WIDTH]).astype(BF16)
    c0 += SB_WIDTH
    ksb_ref[...] = _dot(xn, w1_ref[:, c0:c0 + SB_WIDTH]).astype(BF16)
    c0 += SB_WIDTH
    vsb_ref[...] = _dot(xn, w1_ref[:, c0:c0 + SB_WIDTH]).astype(BF16)
    c0 += SB_WIDTH
    cq = _dot(xn, w1_ref[:, c0:c0 + MLA_Q_RANK])
    c0 += MLA_Q_RANK
    ckv = _dot(xn, w1_ref[:, c0:c0 + MLA_KV_RANK])
    c0 += MLA_KV_RANK
    kra = _dot(xn, w1_ref[:, c0:c0 + LANES])
    c0 += LANES
    krb = _dot(xn, w1_ref[:, c0:c0 + LANES])
    cos = cos_ref[...]
    sin = sin_ref[...]
    kr_ref[...] = (kra * cos + krb * sin).astype(BF16)

    cqn = _rms(cq, gcq_ref[...]).astype(BF16)
    qn_ref[...] = _dot(cqn, wqn_ref[...]).astype(BF16)
    ra = _dot(cqn, wqra_ref[...])
    rb = _dot(cqn, wqrb_ref[...])
    for p in range(HEAD_PAIRS):
        sl = slice(p * LANES, (p + 1) * LANES)
        qr_ref[:, sl] = (ra[:, sl] * cos + rb[:, sl] * sin).astype(BF16)

    ckvn = _rms(ckv, gckv_ref[...]).astype(BF16)
    kn_ref[...] = _dot(ckvn, wkvk_ref[...]).astype(BF16)
    vm_ref[...] = _dot(ckvn, wkvv_ref[...]).astype(BF16)


def _mixer_in(h, g, cos, sin, gcq, gckv, w1, wqn, wqra, wqrb, wkvk, wkvv):
    t = h.shape[0]
    tm = TOKEN_TILE
    row = lambda w: pl.BlockSpec((tm, w), lambda i: (i, 0))
    full = lambda a: pl.BlockSpec(a.shape, lambda i: (0,) * a.ndim)
    widths = [SB_WIDTH, SB_WIDTH, SB_WIDTH, MLA_WIDTH, MLA_WIDTH, MLA_WIDTH, LANES, MLA_WIDTH]
    return pl.pallas_call(
        _mixer_in_kernel,
        grid=(t // tm,),
        in_specs=[row(D_MODEL), full(g), row(LANES), row(LANES), full(gcq), full(gckv),
                  full(w1), full(wqn), full(wqra), full(wqrb), full(wkvk), full(wkvv)],
        out_specs=[row(w) for w in widths],
        out_shape=[jax.ShapeDtypeStruct((t, w), BF16) for w in widths],
        compiler_params=_params("parallel"),
        name="mixer_in",
    )(h, g, cos, sin, gcq, gckv, w1, wqn, wqra, wqrb, wkvk, wkvv)


def _sb_kernel(q_ref, k_ref, v_ref, o_ref, acc_ref, c_ref, *, blk):
    qi = pl.program_id(2)
    q2 = q_ref[0]
    lane = lax.broadcasted_iota(jnp.int32, (1, LANES), 1)
    first = lane < SB_HEAD_DIM
    zero = jnp.zeros_like(q2)
    q_h = (jnp.where(first, q2, zero), jnp.where(first, zero, q2))
    r = lax.broadcasted_iota(jnp.int32, (blk, blk), 0)
    c = lax.broadcasted_iota(jnp.int32, (blk, blk), 1)
    tri = jnp.where(r >= c, 1.0, 0.0).astype(BF16)
    causal = c < r

    acc_ref[...] = jnp.zeros_like(acc_ref)
    c_ref[...] = jnp.zeros_like(c_ref)

    def block(kb, diag):
        start = pl.multiple_of(kb * blk, blk)
        k_blk = k_ref[0, pl.ds(start, blk), :]
        v_blk = v_ref[0, pl.ds(start, blk), :]
        for h in range(2):
            z = _dot_t(q_h[h], k_blk)
            sp = jnp.maximum(z, 0.0) + jnp.log1p(jnp.exp(-jnp.abs(z)))
            if diag:
                sp = jnp.where(causal, sp, 0.0)
            hi = sp.astype(BF16)
            lo = (sp - hi.astype(F32)).astype(BF16)
            cl = _dot(hi, tri) + _dot(lo, tri)
            carry = c_ref[h]
            e = z - (jnp.concatenate([carry] * (blk // LANES), axis=1) + cl)
            w = jnp.exp(e)
            if diag:
                w = jnp.where(causal, w, 0.0)
            acc_ref[h] += _dot(w.astype(BF16), v_blk)
            c_ref[h] = carry + jnp.broadcast_to(cl[:, 0:1], carry.shape)

    block(qi, True)

    def body(j, _):
        block(qi - 1 - j, False)
        return 0

    lax.fori_loop(0, qi, body, 0)
    o_ref[0] = jnp.where(first, acc_ref[0], acc_ref[1]).astype(o_ref.dtype)


def _sb_attention(q, k, v):
    b, s, _ = q.shape
    blk = ATTN_BLOCK
    qspec = pl.BlockSpec((1, blk, LANES), lambda bi, hp, qi: (bi, qi, hp))
    kspec = pl.BlockSpec((1, s, LANES), lambda bi, hp, qi: (bi, 0, hp))
    return pl.pallas_call(
        functools.partial(_sb_kernel, blk=blk),
        grid=(b, HEAD_PAIRS, s // blk),
        in_specs=[qspec, kspec, kspec],
        out_specs=qspec,
        out_shape=jax.ShapeDtypeStruct(q.shape, BF16),
        scratch_shapes=[pltpu.VMEM((2, blk, LANES), F32), pltpu.VMEM((2, blk, LANES), F32)],
        compiler_params=_params("parallel", "parallel", "arbitrary"),
        name="sb_attention",
    )(q, k, v)


def _mla_kernel(qn_ref, qr_ref, kn_ref, kr_ref, v_ref, o_ref, acc_ref, m_ref, l_ref, *, blk):
    qi = pl.program_id(2)
    qcat = jnp.concatenate([qn_ref[0], qr_ref[0]], axis=1)
    lane2 = lax.broadcasted_iota(jnp.int32, (1, 2 * LANES), 1)
    sel0 = (lane2 < MLA_NOPE) | ((lane2 >= LANES) & (lane2 < LANES + MLA_ROPE))
    sel1 = ((lane2 >= MLA_NOPE) & (lane2 < LANES)) | (
        (lane2 >= LANES + MLA_ROPE) & (lane2 < LANES + 2 * MLA_ROPE))
    zero = jnp.zeros_like(qcat)
    q_h = (jnp.where(sel0, qcat, zero), jnp.where(sel1, qcat, zero))
    r = lax.broadcasted_iota(jnp.int32, (blk, blk), 0)
    c = lax.broadcasted_iota(jnp.int32, (blk, blk), 1)
    visible = (c // CHUNK) <= (r // CHUNK)
    lane = lax.broadcasted_iota(jnp.int32, (1, LANES), 1)
    first = lane < MLA_V

    acc_ref[...] = jnp.zeros_like(acc_ref)
    l_ref[...] = jnp.zeros_like(l_ref)
    m_ref[...] = jnp.full(m_ref.shape, -jnp.inf, F32)

    def block(kb, diag):
        start = pl.multiple_of(kb * blk, blk)
        kcat = jnp.concatenate([kn_ref[0, pl.ds(start, blk), :], kr_ref[0, pl.ds(start, blk), :]], axis=1)
        v_blk = v_ref[0, pl.ds(start, blk), :]
        for h in range(2):
            sc = _dot_t(q_h[h], kcat)
            if diag:
                sc = jnp.where(visible, sc, -jnp.inf)
            m_prev = m_ref[h]
            m_new = jnp.maximum(m_prev, jnp.max(sc, axis=1, keepdims=True))
            alpha = jnp.exp(m_prev - m_new)
            p = jnp.exp(sc - jnp.concatenate([m_new] * (blk // LANES), axis=1))
            l_ref[h] = alpha * l_ref[h] + jnp.sum(p, axis=1, keepdims=True)
            acc_ref[h] = alpha * acc_ref[h] + _dot(p.astype(BF16), v_blk)
            m_ref[h] = m_new

    block(qi, True)

    def body(j, _):
        block(j, False)
        return 0

    lax.fori_loop(0, qi, body, 0)
    o_ref[0] = jnp.where(first, acc_ref[0] / l_ref[0], acc_ref[1] / l_ref[1]).astype(o_ref.dtype)


def _mla_attention(qn, qr, kn, kr, v):
    b, s, _ = qn.shape
    blk = ATTN_BLOCK
    qspec = pl.BlockSpec((1, blk, LANES), lambda bi, hp, qi: (bi, qi, hp))
    kspec = pl.BlockSpec((1, s, LANES), lambda bi, hp, qi: (bi, 0, hp))
    krspec = pl.BlockSpec((1, s, LANES), lambda bi, hp, qi: (bi, 0, 0))
    stat = pltpu.VMEM((2, blk, LANES), F32)
    return pl.pallas_call(
        functools.partial(_mla_kernel, blk=blk),
        grid=(b, HEAD_PAIRS, s // blk),
        in_specs=[qspec, qspec, kspec, krspec, kspec],
        out_specs=qspec,
        out_shape=jax.ShapeDtypeStruct(qn.shape, BF16),
        scratch_shapes=[stat, stat, stat],
        compiler_params=_params("parallel", "parallel", "arbitrary"),
        name="mla_attention",
    )(qn, qr, kn, kr, v)


def _route(logits):
    lane = lax.broadcasted_iota(jnp.int32, logits.shape, 1).astype(F32)
    ninf = -jnp.inf
    big = float(ROUTER_LANES)
    is_g = lane < N_GROUPS
    lg = jnp.where(is_g, logits, ninf)
    gmax = jnp.max(lg, axis=1, keepdims=True)
    gsum = jnp.sum(jnp.where(is_g, jnp.exp(lg - gmax), 0.0), axis=1, keepdims=True)
    gp = 1.0 / gsum
    g = jnp.min(jnp.where(lg == gmax, lane, big), axis=1, keepdims=True)
    lo = N_GROUPS + EXPERTS_PER_GROUP * g
    in_grp = (lane >= lo) & (lane < lo + EXPERTS_PER_GROUP)
    le = jnp.where(in_grp, logits, ninf)
    l1 = jnp.max(le, axis=1, keepdims=True)
    i1 = jnp.min(jnp.where(le == l1, lane, big), axis=1, keepdims=True)
    le2 = jnp.where(lane == i1, ninf, le)
    l2 = jnp.max(le2, axis=1, keepdims=True)
    i2 = jnp.min(jnp.where(le2 == l2, lane, big), axis=1, keepdims=True)
    t = jnp.exp(l2 - l1)
    w1 = gp / (1.0 + t)
    w2 = gp * t / (1.0 + t)
    return jnp.where(lane == i1, w1, jnp.where(lane == i2, w2, 0.0))


def _mixer_out_kernel(h_ref, osb_ref, omla_ref, gosb_ref, gomla_ref, wout_ref, gmoe_ref,
                      wr_ref, br_ref, h1_ref, xn_ref, comb_ref):
    nsb = _rms(osb_ref[...].astype(F32), gosb_ref[...]).astype(BF16)
    nmla = _rms(omla_ref[...].astype(F32), gomla_ref[...]).astype(BF16)
    h1 = h_ref[...] + _dot(nsb, wout_ref[0:SB_WIDTH, :]) + _dot(nmla, wout_ref[SB_WIDTH:, :])
    h1_ref[...] = h1
    xn = _rms(h1, gmoe_ref[...])
    xn_ref[...] = xn.astype(BF16)
    logits = jnp.dot(xn, wr_ref[...], preferred_element_type=F32,
                     precision=lax.Precision.HIGHEST) + br_ref[...]
    comb_ref[...] = _route(logits)


def _mixer_out(h, osb, omla, gosb, gomla, wout, gmoe, wr, br):
    t = h.shape[0]
    tm = TOKEN_TILE
    row = lambda w: pl.BlockSpec((tm, w), lambda i: (i, 0))
    full = lambda a: pl.BlockSpec(a.shape, lambda i: (0,) * a.ndim)
    return pl.pallas_call(
        _mixer_out_kernel,
        grid=(t // tm,),
        in_specs=[row(D_MODEL), row(SB_WIDTH), row(MLA_WIDTH), full(gosb), full(gomla), full(wout),
                  full(gmoe), full(wr), full(br)],
        out_specs=[row(D_MODEL), row(D_MODEL), row(ROUTER_LANES)],
        out_shape=[jax.ShapeDtypeStruct((t, D_MODEL), F32), jax.ShapeDtypeStruct((t, D_MODEL), BF16),
                   jax.ShapeDtypeStruct((t, ROUTER_LANES), F32)],
        compiler_params=_params("parallel"),
        name="mixer_out",
    )(h, osb, omla, gosb, gomla, wout, gmoe, wr, br)


def _moe_kernel(x_ref, comb_ref, h_ref, wg_ref, wu_ref, wd_ref, o_ref, acc_ref):
    e = pl.program_id(1)

    @pl.when(e == 0)
    def _():
        acc_ref[...] = jnp.zeros_like(acc_ref)

    x = x_ref[...]
    a = _dot(x, wg_ref[0])
    hid = (a / (1.0 + jnp.exp(-a))) * _dot(x, wu_ref[0])
    y = _dot(hid.astype(BF16), wd_ref[0])
    comb = comb_ref[...]
    lane = lax.broadcasted_iota(jnp.int32, comb.shape, 1)
    ce = jnp.sum(jnp.where(lane == e + N_GROUPS, comb, 0.0), axis=1, keepdims=True)
    acc_ref[...] += ce * y

    @pl.when(e == N_EXPERTS - 1)
    def _():
        o_ref[...] = h_ref[...] + acc_ref[...]


def _moe(xn, comb, h, wg, wu, wd):
    t = h.shape[0]
    tm = TOKEN_TILE
    row = lambda w: pl.BlockSpec((tm, w), lambda i, e: (i, 0))
    return pl.pallas_call(
        _moe_kernel,
        grid=(t // tm, N_EXPERTS),
        in_specs=[row(D_MODEL), row(ROUTER_LANES), row(D_MODEL),
                  pl.BlockSpec((1, D_MODEL, D_EXPERT), lambda i, e: (e, 0, 0)),
                  pl.BlockSpec((1, D_MODEL, D_EXPERT), lambda i, e: (e, 0, 0)),
                  pl.BlockSpec((1, D_EXPERT, D_MODEL), lambda i, e: (e, 0, 0))],
        out_specs=row(D_MODEL),
        out_shape=jax.ShapeDtypeStruct((t, D_MODEL), F32),
        scratch_shapes=[pltpu.VMEM((tm, D_MODEL), F32)],
        compiler_params=_params("parallel", "arbitrary"),
        name="moe",
    )(xn, comb, h, wg, wu, wd)


def _ple_kernel(h_ref, p_ref, g_ref, wpg_ref, wpe_ref, gf_ref, o_ref, *, final):
    h = h_ref[...]
    xn = _rms(h, g_ref[...]).astype(BF16)
    gate = 1.0 / (1.0 + jnp.exp(-_dot(xn, wpg_ref[...])))
    out = h + gate * _dot(p_ref[...].astype(BF16), wpe_ref[...])
    if final:
        out = _rms(out, gf_ref[...])
    o_ref[...] = out


def _ple(h, p, g, wpg, wpe, gf, final):
    t = h.shape[0]
    tm = TOKEN_TILE
    row = lambda w: pl.BlockSpec((tm, w), lambda i: (i, 0))
    full = lambda a: pl.BlockSpec(a.shape, lambda i: (0,) * a.ndim)
    return pl.pallas_call(
        functools.partial(_ple_kernel, final=final),
        grid=(t // tm,),
        in_specs=[row(D_MODEL), row(D_PLE), full(g), full(wpg), full(wpe), full(gf)],
        out_specs=row(D_MODEL),
        out_shape=jax.ShapeDtypeStruct((t, D_MODEL), F32),
        compiler_params=_params("parallel"),
        name="ple",
    )(h, p, g, wpg, wpe, gf)


def _rot_cols(w):
    half = w.shape[-1] // 2
    return jnp.concatenate([-w[:, half:], w[:, :half]], axis=1)


def _prep_in(w_in):
    sb_scale = SB_HEAD_DIM ** -0.5
    kr = w_in[:, 3 * SB_WIDTH + MLA_Q_RANK + MLA_KV_RANK:]
    pad = jnp.zeros((w_in.shape[0], LANES - 2 * MLA_ROPE), w_in.dtype)
    kra = jnp.concatenate([kr, kr, pad], axis=1)
    krr = _rot_cols(kr)
    krb = jnp.concatenate([krr, krr, pad], axis=1)
    w1 = jnp.concatenate([w_in[:, :SB_WIDTH] * sb_scale,
                          w_in[:, SB_WIDTH:3 * SB_WIDTH + MLA_Q_RANK + MLA_KV_RANK], kra, krb], axis=1)
    return w1.astype(BF16)


def _prep_uq(w_uq):
    scale = MLA_QK ** -0.5
    w = w_uq.reshape(MLA_Q_RANK, MLA_HEADS, MLA_QK) * scale
    wqn = w[:, :, :MLA_NOPE].reshape(MLA_Q_RANK, MLA_HEADS * MLA_NOPE)
    rope = w[:, :, MLA_NOPE:]
    half = MLA_ROPE // 2
    rot = jnp.concatenate([-rope[:, :, half:], rope[:, :, :half]], axis=2)
    pad = jnp.zeros((MLA_Q_RANK, HEAD_PAIRS, LANES - 2 * MLA_ROPE), w.dtype)

    def pairs(r):
        return jnp.concatenate([r.reshape(MLA_Q_RANK, HEAD_PAIRS, 2 * MLA_ROPE), pad], axis=2).reshape(
            MLA_Q_RANK, HEAD_PAIRS * LANES)

    return wqn.astype(BF16), pairs(rope).astype(BF16), pairs(rot).astype(BF16)


def _prep_ukv(w_ukv):
    w = w_ukv.reshape(MLA_KV_RANK, MLA_HEADS, MLA_NOPE + MLA_V)
    wk = w[:, :, :MLA_NOPE].reshape(MLA_KV_RANK, MLA_HEADS * MLA_NOPE)
    wv = w[:, :, MLA_NOPE:].reshape(MLA_KV_RANK, MLA_HEADS * MLA_V)
    return wk.astype(BF16), wv.astype(BF16)


def _prep_router(w_rg, b_rg, w_re, b_re):
    pad = ROUTER_LANES - N_GROUPS - N_EXPERTS
    wr = jnp.concatenate([w_rg, w_re, jnp.zeros((D_MODEL, pad), F32)], axis=1)
    br = jnp.concatenate([b_rg, b_re, jnp.zeros((pad,), F32)])[None, :]
    return wr, br


def kernel(x, p, positions, g_mix, w_in, g_cq, w_uq, g_ckv, w_ukv, g_osb, g_omla, w_out, g_moe,
           w_rg, b_rg, w_re, b_re, w_gate, w_up, w_down, g_ple, w_pg, w_pe, g_final):
    b, s, d = x.shape
    t = b * s
    depth = w_in.shape[0]

    inv_freq = ROPE_THETA ** (-jnp.arange(0, MLA_ROPE, 2, dtype=F32) / MLA_ROPE)
    ang = positions.astype(F32)[..., None] * inv_freq
    reps = LANES // (MLA_ROPE // 2)
    cos = jnp.tile(jnp.cos(ang), (1, 1, reps)).reshape(t, LANES)
    sin = jnp.tile(jnp.sin(ang), (1, 1, reps)).reshape(t, LANES)

    h = x.reshape(t, d)
    r3 = lambda a: a.reshape(b, s, a.shape[-1])
    r2 = lambda a: a.reshape(t, a.shape[-1])
    for i in range(depth):
        w1 = _prep_in(w_in[i])
        wqn, wqra, wqrb = _prep_uq(w_uq[i])
        wkvk, wkvv = _prep_ukv(w_ukv[i])
        qsb, ksb, vsb, qn, qr, kn, kr, vm = _mixer_in(
            h, g_mix[i][None], cos, sin, g_cq[i][None], g_ckv[i][None], w1, wqn, wqra, wqrb, wkvk, wkvv)
        osb = _sb_attention(r3(qsb), r3(ksb), r3(vsb))
        omla = _mla_attention(r3(qn), r3(qr), r3(kn), r3(kr), r3(vm))
        wr, br = _prep_router(w_rg[i], b_rg[i], w_re[i], b_re[i])
        h1, xn, comb = _mixer_out(h, r2(osb), r2(omla), g_osb[i][None], g_omla[i][None],
                                  w_out[i].astype(BF16), g_moe[i][None], wr, br)
        h2 = _moe(xn, comb, h1, w_gate[i].astype(BF16), w_up[i].astype(BF16), w_down[i].astype(BF16))
        h = _ple(h2, p[i].reshape(t, D_PLE), g_ple[i][None], w_pg[i].astype(BF16), w_pe[i].astype(BF16),
                 g_final[None], final=(i == depth - 1))
    return h.reshape(b, s, d)
```

```python
import functools

import jax
import jax.numpy as jnp
from jax import lax
from jax.experimental import pallas as pl
from jax.experimental.pallas import tpu as pltpu

F32 = jnp.float32
BF16 = jnp.bfloat16

D_MODEL = 1024
CHUNK = 64
D_PLE = 256
EPS = 1e-6
SB_HEADS = 8
SB_HEAD_DIM = 64
SB_WIDTH = SB_HEADS * SB_HEAD_DIM
MLA_HEADS = 8
MLA_NOPE = 64
MLA_ROPE = 32
MLA_QK = MLA_NOPE + MLA_ROPE
MLA_V = 64
MLA_Q_RANK = 384
MLA_KV_RANK = 256
MLA_WIDTH = MLA_HEADS * MLA_V
ROPE_THETA = 10000.0
N_GROUPS = 4
EXPERTS_PER_GROUP = 8
N_EXPERTS = N_GROUPS * EXPERTS_PER_GROUP
D_EXPERT = 256

LANES = 128
HEAD_PAIRS = SB_HEADS // 2
ROUTER_LANES = LANES
TOKEN_TILE = 512
ATTN_BLOCK = 256
LOG2_E = 1.4426950408889634
SB_UNDERFLOW = 151.0
VMEM_LIMIT = 48 * 1024 * 1024


def _params(*sem):
    return pltpu.CompilerParams(dimension_semantics=sem, vmem_limit_bytes=VMEM_LIMIT)


def _rms(x, g):
    return x * lax.rsqrt(jnp.mean(x * x, axis=-1, keepdims=True) + EPS) * g


def _dot(a, b):
    return jnp.dot(a, b, preferred_element_type=F32)


def _dot_t(a, b):
    return lax.dot_general(a, b, (((1,), (1,)), ((), ())), preferred_element_type=F32)


def _mixer_in_kernel(x_ref, g_ref, cos_ref, sin_ref, gcq_ref, gckv_ref, w1_ref, wqn_ref,
                     wqra_ref, wqrb_ref, wkvk_ref, wkvv_ref,
                     qsb_ref, ksb_ref, vsb_ref, qn_ref, qr_ref, kn_ref, kr_ref, vm_ref):
    xn = _rms(x_ref[...], g_ref[...]).astype(BF16)
    c0 = 0
    qsb_ref[...] = _dot(xn, w1_ref[:, c0:c0 + SB_WIDTH]).astype(BF16)
    c0 += SB_WIDTH
    ksb_ref[...] = _dot(xn, w1_ref[:, c0:c0 + SB_WIDTH]).astype(BF16)
    c0 += SB_WIDTH
    vsb_ref[...] = _dot(xn, w1_ref[:, c0:c0 + SB_WIDTH]).astype(BF16)
    c0 += SB_WIDTH
    cq = _dot(xn, w1_ref[:, c0:c0 + MLA_Q_RANK])
    c0 += MLA_Q_RANK
    ckv = _dot(xn, w1_ref[:, c0:c0 + MLA_KV_RANK])
    c0 += MLA_KV_RANK
    kra = _dot(xn, w1_ref[:, c0:c0 + LANES])
    c0 += LANES
    krb = _dot(xn, w1_ref[:, c0:c0 + LANES])
    cos = cos_ref[...]
    sin = sin_ref[...]
    kr_ref[...] = (kra * cos + krb * sin).astype(BF16)

    cqn = _rms(cq, gcq_ref[...]).astype(BF16)
    qn_ref[...] = _dot(cqn, wqn_ref[...]).astype(BF16)
    ra = _dot(cqn, wqra_ref[...])
    rb = _dot(cqn, wqrb_ref[...])
    for p in range(HEAD_PAIRS):
        sl = slice(p * LANES, (p + 1) * LANES)
        qr_ref[:, sl] = (ra[:, sl] * cos + rb[:, sl] * sin).astype(BF16)

    ckvn = _rms(ckv, gckv_ref[...]).astype(BF16)
    kn_ref[...] = _dot(ckvn, wkvk_ref[...]).astype(BF16)
    lane = lax.broadcasted_iota(jnp.int32, (1, LANES), 1)
    for hd in range(MLA_HEADS):
        sl = slice(hd * LANES, (hd + 1) * LANES)
        vm_ref[:, sl] = jnp.where(lane < MLA_V, _dot(ckvn, wkvv_ref[:, sl]), 1.0).astype(BF16)


def _mixer_in(h, g, cos, sin, gcq, gckv, w1, wqn, wqra, wqrb, wkvk, wkvv):
    t = h.shape[0]
    tm = TOKEN_TILE
    row = lambda w: pl.BlockSpec((tm, w), lambda i: (i, 0))
    full = lambda a: pl.BlockSpec(a.shape, lambda i: (0,) * a.ndim)
    widths = [SB_WIDTH, SB_WIDTH, SB_WIDTH, MLA_WIDTH, MLA_WIDTH, MLA_WIDTH, LANES, MLA_HEADS * LANES]
    return pl.pallas_call(
        _mixer_in_kernel,
        grid=(t // tm,),
        in_specs=[row(D_MODEL), full(g), row(LANES), row(LANES), full(gcq), full(gckv),
                  full(w1), full(wqn), full(wqra), full(wqrb), full(wkvk), full(wkvv)],
        out_specs=[row(w) for w in widths],
        out_shape=[jax.ShapeDtypeStruct((t, w), BF16) for w in widths],
        compiler_params=_params("parallel"),
        name="mixer_in",
    )(h, g, cos, sin, gcq, gckv, w1, wqn, wqra, wqrb, wkvk, wkvv)


def _sb_kernel(q_ref, k_ref, v_ref, o_ref, acc_ref, c_ref, *, blk):
    qi = pl.program_id(1)
    lane = lax.broadcasted_iota(jnp.int32, (1, LANES), 1)
    first = lane < SB_HEAD_DIM
    q_st = []
    for p in range(HEAD_PAIRS):
        q2 = q_ref[0, :, p * LANES:(p + 1) * LANES]
        zero = jnp.zeros_like(q2)
        q_st.append(jnp.concatenate([jnp.where(first, q2, zero), jnp.where(first, zero, q2)], axis=0))
    r = lax.broadcasted_iota(jnp.int32, (blk, blk), 0)
    c = lax.broadcasted_iota(jnp.int32, (blk, blk), 1)
    tri = jnp.where(r >= c, 1.0, 0.0).astype(BF16)
    r2 = lax.broadcasted_iota(jnp.int32, (2 * blk, blk), 0)
    c2 = lax.broadcasted_iota(jnp.int32, (2 * blk, blk), 1)
    causal = c2 < jnp.where(r2 >= blk, r2 - blk, r2)
    hi_mask = jnp.uint32(0xFFFF0000)
    sign_bit = jnp.uint32(0x80000000)

    acc_ref[...] = jnp.zeros_like(acc_ref)
    c_ref[...] = jnp.zeros_like(c_ref)

    def block(kb, diag):
        start = pl.multiple_of(kb * blk, blk)
        cols = lambda ref, p: ref[0, pl.ds(start, blk), p * LANES:(p + 1) * LANES]
        z = [_dot_t(q_st[p], cols(k_ref, p)) for p in range(HEAD_PAIRS)]
        hi, lo = [], []
        for p in range(HEAD_PAIRS):
            neg_abs = pltpu.bitcast(pltpu.bitcast(z[p], jnp.uint32) | sign_bit, F32)
            sp = jnp.maximum(z[p], 0.0) + jnp.log(1.0 + jnp.exp2(neg_abs)) * LOG2_E
            if diag:
                sp = jnp.where(causal, sp, 0.0)
            top = pltpu.bitcast(pltpu.bitcast(sp, jnp.uint32) & hi_mask, F32)
            hi.append(top.astype(BF16))
            lo.append((sp - top).astype(BF16))
        cl = [_dot(hi[p], tri) + _dot(lo[p], tri) for p in range(HEAD_PAIRS)]
        w = []
        for p in range(HEAD_PAIRS):
            wp = jnp.exp2(z[p] - cl[p])
            if diag:
                wp = jnp.where(causal, wp, 0.0)
            w.append(wp.astype(BF16))
        pv = [_dot(w[p], cols(v_ref, p)) for p in range(HEAD_PAIRS)]
        for p in range(HEAD_PAIRS):
            carry = c_ref[p]
            acc_ref[p] += jnp.exp2(-carry) * pv[p]
            c_ref[p] = carry + jnp.broadcast_to(cl[p][:, 0:1], carry.shape)

    def min_carry():
        return jnp.min(jnp.min(c_ref[...], axis=0))

    block(qi, True)

    def cond(st):
        return (st[0] < qi) & (st[1] < SB_UNDERFLOW)

    def body(st):
        block(qi - 1 - st[0], False)
        return st[0] + 1, min_carry()

    lax.while_loop(cond, body, (jnp.int32(0), min_carry()))
    for p in range(HEAD_PAIRS):
        a = acc_ref[p]
        o_ref[0, :, p * LANES:(p + 1) * LANES] = jnp.where(first, a[:blk], a[blk:]).astype(o_ref.dtype)


def _sb_attention(q, k, v):
    b, s, wdt = q.shape
    blk = ATTN_BLOCK
    qspec = pl.BlockSpec((1, blk, wdt), lambda bi, qi: (bi, qi, 0))
    kspec = pl.BlockSpec((1, s, wdt), lambda bi, qi: (bi, 0, 0))
    state = pltpu.VMEM((HEAD_PAIRS, 2 * blk, LANES), F32)
    return pl.pallas_call(
        functools.partial(_sb_kernel, blk=blk),
        grid=(b, s // blk),
        in_specs=[qspec, kspec, kspec],
        out_specs=qspec,
        out_shape=jax.ShapeDtypeStruct(q.shape, BF16),
        scratch_shapes=[state, state],
        compiler_params=_params("parallel", "arbitrary"),
        name="sb_attention",
    )(q, k, v)


def _mla_kernel(qn_ref, qr_ref, kn_ref, kr_ref, v_ref, o_ref, acc_ref, m_ref, *, blk):
    qi = pl.program_id(1)
    lane2 = lax.broadcasted_iota(jnp.int32, (1, 2 * LANES), 1)
    sel0 = (lane2 < MLA_NOPE) | ((lane2 >= LANES) & (lane2 < LANES + MLA_ROPE))
    sel1 = ((lane2 >= MLA_NOPE) & (lane2 < LANES)) | (
        (lane2 >= LANES + MLA_ROPE) & (lane2 < LANES + 2 * MLA_ROPE))
    q_st = []
    for p in range(HEAD_PAIRS):
        sl = slice(p * LANES, (p + 1) * LANES)
        qcat = jnp.concatenate([qn_ref[0, :, sl], qr_ref[0, :, sl]], axis=1)
        zero = jnp.zeros_like(qcat)
        q_st.append(jnp.concatenate([jnp.where(sel0, qcat, zero), jnp.where(sel1, qcat, zero)], axis=0))
    r2 = lax.broadcasted_iota(jnp.int32, (2 * blk, blk), 0)
    c2 = lax.broadcasted_iota(jnp.int32, (2 * blk, blk), 1)
    visible = (c2 // CHUNK) <= (jnp.where(r2 >= blk, r2 - blk, r2) // CHUNK)
    lane = lax.broadcasted_iota(jnp.int32, (1, LANES), 1)
    first = lane < MLA_V

    acc_ref[...] = jnp.zeros_like(acc_ref)
    m_ref[...] = jnp.full(m_ref.shape, -jnp.inf, F32)

    def block(kb, diag):
        start = pl.multiple_of(kb * blk, blk)
        kr_blk = kr_ref[0, pl.ds(start, blk), :]
        sc = []
        for p in range(HEAD_PAIRS):
            kcat = jnp.concatenate([kn_ref[0, pl.ds(start, blk), p * LANES:(p + 1) * LANES], kr_blk], axis=1)
            sc.append(_dot_t(q_st[p], kcat))
        pr, alpha = [], []
        for p in range(HEAD_PAIRS):
            s_p = jnp.where(visible, sc[p], -jnp.inf) if diag else sc[p]
            m_prev = m_ref[p]
            m_new = jnp.maximum(m_prev, jnp.max(s_p, axis=1, keepdims=True))
            alpha.append(jnp.exp2(m_prev - m_new))
            pr.append(jnp.exp2(s_p - jnp.concatenate([m_new] * (blk // LANES), axis=1)).astype(BF16))
            m_ref[p] = m_new
        for h in range(MLA_HEADS):
            p, half = divmod(h, 2)
            rows = slice(half * blk, (half + 1) * blk)
            pv = _dot(pr[p][rows], v_ref[0, pl.ds(start, blk), h * LANES:(h + 1) * LANES])
            acc_ref[h] = alpha[p][rows] * acc_ref[h] + pv

    block(qi, True)

    def body(j, _):
        block(j, False)
        return 0

    lax.fori_loop(0, qi, body, 0)
    for p in range(HEAD_PAIRS):
        a0 = acc_ref[2 * p]
        a1 = acc_ref[2 * p + 1]
        o0 = a0 / pltpu.roll(a0, MLA_V, axis=1)
        o1 = pltpu.roll(a1 / pltpu.roll(a1, MLA_V, axis=1), MLA_V, axis=1)
        o_ref[0, :, p * LANES:(p + 1) * LANES] = jnp.where(first, o0, o1).astype(o_ref.dtype)


def _mla_attention(qn, qr, kn, kr, vcat):
    b, s, wdt = qn.shape
    blk = ATTN_BLOCK
    qspec = pl.BlockSpec((1, blk, wdt), lambda bi, qi: (bi, qi, 0))
    full = lambda a: pl.BlockSpec((1, s, a.shape[-1]), lambda bi, qi: (bi, 0, 0))
    return pl.pallas_call(
        functools.partial(_mla_kernel, blk=blk),
        grid=(b, s // blk),
        in_specs=[qspec, qspec, full(kn), full(kr), full(vcat)],
        out_specs=qspec,
        out_shape=jax.ShapeDtypeStruct(qn.shape, BF16),
        scratch_shapes=[pltpu.VMEM((MLA_HEADS, blk, LANES), F32),
                        pltpu.VMEM((HEAD_PAIRS, 2 * blk, LANES), F32)],
        compiler_params=_params("parallel", "arbitrary"),
        name="mla_attention",
    )(qn, qr, kn, kr, vcat)


def _route(logits):
    lane = lax.broadcasted_iota(jnp.int32, logits.shape, 1).astype(F32)
    ninf = -jnp.inf
    big = float(ROUTER_LANES)
    is_g = lane < N_GROUPS
    lg = jnp.where(is_g, logits, ninf)
    gmax = jnp.max(lg, axis=1, keepdims=True)
    gsum = jnp.sum(jnp.where(is_g, jnp.exp(lg - gmax), 0.0), axis=1, keepdims=True)
    gp = 1.0 / gsum
    g = jnp.min(jnp.where(lg == gmax, lane, big), axis=1, keepdims=True)
    lo = N_GROUPS + EXPERTS_PER_GROUP * g
    in_grp = (lane >= lo) & (lane < lo + EXPERTS_PER_GROUP)
    le = jnp.where(in_grp, logits, ninf)
    l1 = jnp.max(le, axis=1, keepdims=True)
    i1 = jnp.min(jnp.where(le == l1, lane, big), axis=1, keepdims=True)
    le2 = jnp.where(lane == i1, ninf, le)
    l2 = jnp.max(le2, axis=1, keepdims=True)
    i2 = jnp.min(jnp.where(le2 == l2, lane, big), axis=1, keepdims=True)
    t = jnp.exp(l2 - l1)
    w1 = gp / (1.0 + t)
    w2 = gp * t / (1.0 + t)
    return jnp.where(lane == i1, w1, jnp.where(lane == i2, w2, 0.0))


def _mixer_out_kernel(h_ref, osb_ref, omla_ref, gosb_ref, gomla_ref, wout_ref, gmoe_ref,
                      wr_ref, br_ref, h1_ref, xn_ref, comb_ref):
    nsb = _rms(osb_ref[...].astype(F32), gosb_ref[...]).astype(BF16)
    nmla = _rms(omla_ref[...].astype(F32), gomla_ref[...]).astype(BF16)
    h1 = h_ref[...] + _dot(nsb, wout_ref[0:SB_WIDTH, :]) + _dot(nmla, wout_ref[SB_WIDTH:, :])
    h1_ref[...] = h1
    xn = _rms(h1, gmoe_ref[...])
    xn_ref[...] = xn.astype(BF16)
    logits = jnp.dot(xn, wr_ref[...], preferred_element_type=F32,
                     precision=lax.Precision.HIGHEST) + br_ref[...]
    comb_ref[...] = _route(logits)


def _mixer_out(h, osb, omla, gosb, gomla, wout, gmoe, wr, br):
    t = h.shape[0]
    tm = TOKEN_TILE
    row = lambda w: pl.BlockSpec((tm, w), lambda i: (i, 0))
    full = lambda a: pl.BlockSpec(a.shape, lambda i: (0,) * a.ndim)
    return pl.pallas_call(
        _mixer_out_kernel,
        grid=(t // tm,),
        in_specs=[row(D_MODEL), row(SB_WIDTH), row(MLA_WIDTH), full(gosb), full(gomla), full(wout),
                  full(gmoe), full(wr), full(br)],
        out_specs=[row(D_MODEL), row(D_MODEL), row(ROUTER_LANES)],
        out_shape=[jax.ShapeDtypeStruct((t, D_MODEL), F32), jax.ShapeDtypeStruct((t, D_MODEL), BF16),
                   jax.ShapeDtypeStruct((t, ROUTER_LANES), F32)],
        compiler_params=_params("parallel"),
        name="mixer_out",
    )(h, osb, omla, gosb, gomla, wout, gmoe, wr, br)


def _moe_kernel(x_ref, comb_ref, h_ref, wg_ref, wu_ref, wd_ref, o_ref, acc_ref):
    e = pl.program_id(1)

    @pl.when(e == 0)
    def _():
        acc_ref[...] = jnp.zeros_like(acc_ref)

    x = x_ref[...]
    a = _dot(x, wg_ref[0])
    hid = (a / (1.0 + jnp.exp(-a))) * _dot(x, wu_ref[0])
    y = _dot(hid.astype(BF16), wd_ref[0])
    comb = comb_ref[...]
    lane = lax.broadcasted_iota(jnp.int32, comb.shape, 1)
    ce = jnp.sum(jnp.where(lane == e + N_GROUPS, comb, 0.0), axis=1, keepdims=True)
    acc_ref[...] += ce * y

    @pl.when(e == N_EXPERTS - 1)
    def _():
        o_ref[...] = h_ref[...] + acc_ref[...]


def _moe(xn, comb, h, wg, wu, wd):
    t = h.shape[0]
    tm = TOKEN_TILE
    row = lambda w: pl.BlockSpec((tm, w), lambda i, e: (i, 0))
    return pl.pallas_call(
        _moe_kernel,
        grid=(t // tm, N_EXPERTS),
        in_specs=[row(D_MODEL), row(ROUTER_LANES), row(D_MODEL),
                  pl.BlockSpec((1, D_MODEL, D_EXPERT), lambda i, e: (e, 0, 0)),
                  pl.BlockSpec((1, D_MODEL, D_EXPERT), lambda i, e: (e, 0, 0)),
                  pl.BlockSpec((1, D_EXPERT, D_MODEL), lambda i, e: (e, 0, 0))],
        out_specs=row(D_MODEL),
        out_shape=jax.ShapeDtypeStruct((t, D_MODEL), F32),
        scratch_shapes=[pltpu.VMEM((tm, D_MODEL), F32)],
        compiler_params=_params("parallel", "arbitrary"),
        name="moe",
    )(xn, comb, h, wg, wu, wd)


def _ple_kernel(h_ref, p_ref, g_ref, wpg_ref, wpe_ref, gf_ref, o_ref, *, final):
    h = h_ref[...]
    xn = _rms(h, g_ref[...]).astype(BF16)
    gate = 1.0 / (1.0 + jnp.exp(-_dot(xn, wpg_ref[...])))
    out = h + gate * _dot(p_ref[...].astype(BF16), wpe_ref[...])
    if final:
        out = _rms(out, gf_ref[...])
    o_ref[...] = out


def _ple(h, p, g, wpg, wpe, gf, final):
    t = h.shape[0]
    tm = TOKEN_TILE
    row = lambda w: pl.BlockSpec((tm, w), lambda i: (i, 0))
    full = lambda a: pl.BlockSpec(a.shape, lambda i: (0,) * a.ndim)
    return pl.pallas_call(
        functools.partial(_ple_kernel, final=final),
        grid=(t // tm,),
        in_specs=[row(D_MODEL), row(D_PLE), full(g), full(wpg), full(wpe), full(gf)],
        out_specs=row(D_MODEL),
        out_shape=jax.ShapeDtypeStruct((t, D_MODEL), F32),
        compiler_params=_params("parallel"),
        name="ple",
    )(h, p, g, wpg, wpe, gf)


def _rot_cols(w):
    half = w.shape[-1] // 2
    return jnp.concatenate([-w[:, half:], w[:, :half]], axis=1)


def _prep_in(w_in):
    sb_scale = SB_HEAD_DIM ** -0.5 * LOG2_E
    kr = w_in[:, 3 * SB_WIDTH + MLA_Q_RANK + MLA_KV_RANK:]
    pad = jnp.zeros((w_in.shape[0], LANES - 2 * MLA_ROPE), w_in.dtype)
    kra = jnp.concatenate([kr, kr, pad], axis=1)
    krr = _rot_cols(kr)
    krb = jnp.concatenate([krr, krr, pad], axis=1)
    w1 = jnp.concatenate([w_in[:, :SB_WIDTH] * sb_scale,
                          w_in[:, SB_WIDTH:3 * SB_WIDTH + MLA_Q_RANK + MLA_KV_RANK], kra, krb], axis=1)
    return w1.astype(BF16)


def _prep_uq(w_uq):
    scale = MLA_QK ** -0.5 * LOG2_E
    w = w_uq.reshape(MLA_Q_RANK, MLA_HEADS, MLA_QK) * scale
    wqn = w[:, :, :MLA_NOPE].reshape(MLA_Q_RANK, MLA_HEADS * MLA_NOPE)
    rope = w[:, :, MLA_NOPE:]
    half = MLA_ROPE // 2
    rot = jnp.concatenate([-rope[:, :, half:], rope[:, :, :half]], axis=2)
    pad = jnp.zeros((MLA_Q_RANK, HEAD_PAIRS, LANES - 2 * MLA_ROPE), w.dtype)

    def pairs(r):
        return jnp.concatenate([r.reshape(MLA_Q_RANK, HEAD_PAIRS, 2 * MLA_ROPE), pad], axis=2).reshape(
            MLA_Q_RANK, HEAD_PAIRS * LANES)

    return wqn.astype(BF16), pairs(rope).astype(BF16), pairs(rot).astype(BF16)


def _prep_ukv(w_ukv):
    w = w_ukv.reshape(MLA_KV_RANK, MLA_HEADS, MLA_NOPE + MLA_V)
    wk = w[:, :, :MLA_NOPE].reshape(MLA_KV_RANK, MLA_HEADS * MLA_NOPE)
    pad = jnp.zeros((MLA_KV_RANK, MLA_HEADS, LANES - MLA_V), w.dtype)
    wv = jnp.concatenate([w[:, :, MLA_NOPE:], pad], axis=2).reshape(MLA_KV_RANK, MLA_HEADS * LANES)
    return wk.astype(BF16), wv.astype(BF16)


def _prep_router(w_rg, b_rg, w_re, b_re):
    pad = ROUTER_LANES - N_GROUPS - N_EXPERTS
    wr = jnp.concatenate([w_rg, w_re, jnp.zeros((D_MODEL, pad), F32)], axis=1)
    br = jnp.concatenate([b_rg, b_re, jnp.zeros((pad,), F32)])[None, :]
    return wr, br


def kernel(x, p, positions, g_mix, w_in, g_cq, w_uq, g_ckv, w_ukv, g_osb, g_omla, w_out, g_moe,
           w_rg, b_rg, w_re, b_re, w_gate, w_up, w_down, g_ple, w_pg, w_pe, g_final):
    b, s, d = x.shape
    t = b * s
    depth = w_in.shape[0]

    inv_freq = ROPE_THETA ** (-jnp.arange(0, MLA_ROPE, 2, dtype=F32) / MLA_ROPE)
    ang = positions.astype(F32)[..., None] * inv_freq
    reps = LANES // (MLA_ROPE // 2)
    cos = jnp.tile(jnp.cos(ang), (1, 1, reps)).reshape(t, LANES)
    sin = jnp.tile(jnp.sin(ang), (1, 1, reps)).reshape(t, LANES)

    h = x.reshape(t, d)
    r3 = lambda a: a.reshape(b, s, a.shape[-1])
    r2 = lambda a: a.reshape(t, a.shape[-1])
    for i in range(depth):
        w1 = _prep_in(w_in[i])
        wqn, wqra, wqrb = _prep_uq(w_uq[i])
        wkvk, wkvv = _prep_ukv(w_ukv[i])
        qsb, ksb, vsb, qn, qr, kn, kr, vm = _mixer_in(
            h, g_mix[i][None], cos, sin, g_cq[i][None], g_ckv[i][None], w1, wqn, wqra, wqrb, wkvk, wkvv)
        osb = _sb_attention(r3(qsb), r3(ksb), r3(vsb))
        omla = _mla_attention(r3(qn), r3(qr), r3(kn), r3(kr), r3(vm))
        wr, br = _prep_router(w_rg[i], b_rg[i], w_re[i], b_re[i])
        h1, xn, comb = _mixer_out(h, r2(osb), r2(omla), g_osb[i][None], g_omla[i][None],
                                  w_out[i].astype(BF16), g_moe[i][None], wr, br)
        h2 = _moe(xn, comb, h1, w_gate[i].astype(BF16), w_up[i].astype(BF16), w_down[i].astype(BF16))
        h = _ple(h2, p[i].reshape(t, D_PLE), g_ple[i][None], w_pg[i].astype(BF16), w_pe[i].astype(BF16),
                 g_final[None], final=(i == depth - 1))
    return h.reshape(b, s, d)
```

```python
import functools

import jax
import jax.numpy as jnp
from jax import lax
from jax.experimental import pallas as pl
from jax.experimental.pallas import tpu as pltpu

F32 = jnp.float32
BF16 = jnp.bfloat16

D_MODEL = 1024
CHUNK = 64
D_PLE = 256
EPS = 1e-6
SB_HEADS = 8
SB_HEAD_DIM = 64
SB_WIDTH = SB_HEADS * SB_HEAD_DIM
MLA_HEADS = 8
MLA_NOPE = 64
MLA_ROPE = 32
MLA_QK = MLA_NOPE + MLA_ROPE
MLA_V = 64
MLA_Q_RANK = 384
MLA_KV_RANK = 256
MLA_WIDTH = MLA_HEADS * MLA_V
ROPE_THETA = 10000.0
N_GROUPS = 4
EXPERTS_PER_GROUP = 8
N_EXPERTS = N_GROUPS * EXPERTS_PER_GROUP
D_EXPERT = 256

LANES = 128
HEAD_PAIRS = SB_HEADS // 2
ROUTER_LANES = LANES
TOKEN_TILE = 512
EXPERT_TILE = 256
ATTN_BLOCK = 256
LOG2_E = 1.4426950408889634
SB_UNDERFLOW = 151.0
VMEM_LIMIT = 48 * 1024 * 1024


def _params(*sem):
    return pltpu.CompilerParams(dimension_semantics=sem, vmem_limit_bytes=VMEM_LIMIT)


def _rms(x, g):
    return x * lax.rsqrt(jnp.mean(x * x, axis=-1, keepdims=True) + EPS) * g


def _dot(a, b):
    return jnp.dot(a, b, preferred_element_type=F32)


def _dot_t(a, b):
    return lax.dot_general(a, b, (((1,), (1,)), ((), ())), preferred_element_type=F32)


def _mixer_in_kernel(x_ref, g_ref, cos_ref, sin_ref, gcq_ref, gckv_ref, w1_ref, wqn_ref,
                     wqra_ref, wqrb_ref, wkvk_ref, wkvv_ref,
                     qsb_ref, ksb_ref, vsb_ref, qn_ref, qr_ref, kn_ref, kr_ref, vm_ref):
    xn = _rms(x_ref[...], g_ref[...]).astype(BF16)
    c0 = 0
    qsb_ref[...] = _dot(xn, w1_ref[:, c0:c0 + SB_WIDTH]).astype(BF16)
    c0 += SB_WIDTH
    ksb_ref[...] = _dot(xn, w1_ref[:, c0:c0 + SB_WIDTH]).astype(BF16)
    c0 += SB_WIDTH
    vsb_ref[...] = _dot(xn, w1_ref[:, c0:c0 + SB_WIDTH]).astype(BF16)
    c0 += SB_WIDTH
    cq = _dot(xn, w1_ref[:, c0:c0 + MLA_Q_RANK])
    c0 += MLA_Q_RANK
    ckv = _dot(xn, w1_ref[:, c0:c0 + MLA_KV_RANK])
    c0 += MLA_KV_RANK
    kra = _dot(xn, w1_ref[:, c0:c0 + LANES])
    c0 += LANES
    krb = _dot(xn, w1_ref[:, c0:c0 + LANES])
    cos = cos_ref[...]
    sin = sin_ref[...]
    kr_ref[...] = (kra * cos + krb * sin).astype(BF16)

    cqn = _rms(cq, gcq_ref[...]).astype(BF16)
    qn_ref[...] = _dot(cqn, wqn_ref[...]).astype(BF16)
    ra = _dot(cqn, wqra_ref[...])
    rb = _dot(cqn, wqrb_ref[...])
    for p in range(HEAD_PAIRS):
        sl = slice(p * LANES, (p + 1) * LANES)
        qr_ref[:, sl] = (ra[:, sl] * cos + rb[:, sl] * sin).astype(BF16)

    ckvn = _rms(ckv, gckv_ref[...]).astype(BF16)
    kn_ref[...] = _dot(ckvn, wkvk_ref[...]).astype(BF16)
    lane = lax.broadcasted_iota(jnp.int32, (1, LANES), 1)
    for hd in range(MLA_HEADS):
        sl = slice(hd * LANES, (hd + 1) * LANES)
        vm_ref[:, sl] = jnp.where(lane < MLA_V, _dot(ckvn, wkvv_ref[:, sl]), 1.0).astype(BF16)


def _mixer_in(h, g, cos, sin, gcq, gckv, w1, wqn, wqra, wqrb, wkvk, wkvv):
    t = h.shape[0]
    tm = TOKEN_TILE
    row = lambda w: pl.BlockSpec((tm, w), lambda i: (i, 0))
    full = lambda a: pl.BlockSpec(a.shape, lambda i: (0,) * a.ndim)
    widths = [SB_WIDTH, SB_WIDTH, SB_WIDTH, MLA_WIDTH, MLA_WIDTH, MLA_WIDTH, LANES, MLA_HEADS * LANES]
    return pl.pallas_call(
        _mixer_in_kernel,
        grid=(t // tm,),
        in_specs=[row(D_MODEL), full(g), row(LANES), row(LANES), full(gcq), full(gckv),
                  full(w1), full(wqn), full(wqra), full(wqrb), full(wkvk), full(wkvv)],
        out_specs=[row(w) for w in widths],
        out_shape=[jax.ShapeDtypeStruct((t, w), BF16) for w in widths],
        compiler_params=_params("parallel"),
        name="mixer_in",
    )(h, g, cos, sin, gcq, gckv, w1, wqn, wqra, wqrb, wkvk, wkvv)


def _sb_kernel(q_ref, k_ref, v_ref, o_ref, acc_ref, c_ref, *, blk):
    qi = pl.program_id(1)
    lane = lax.broadcasted_iota(jnp.int32, (1, LANES), 1)
    first = lane < SB_HEAD_DIM
    q_st = []
    for p in range(HEAD_PAIRS):
        q2 = q_ref[0, :, p * LANES:(p + 1) * LANES]
        zero = jnp.zeros_like(q2)
        q_st.append(jnp.concatenate([jnp.where(first, q2, zero), jnp.where(first, zero, q2)], axis=0))
    r = lax.broadcasted_iota(jnp.int32, (blk, blk), 0)
    c = lax.broadcasted_iota(jnp.int32, (blk, blk), 1)
    tri = jnp.where(r >= c, 1.0, 0.0).astype(BF16)
    r2 = lax.broadcasted_iota(jnp.int32, (2 * blk, blk), 0)
    c2 = lax.broadcasted_iota(jnp.int32, (2 * blk, blk), 1)
    causal = c2 < jnp.where(r2 >= blk, r2 - blk, r2)
    hi_mask = jnp.uint32(0xFFFF0000)
    sign_bit = jnp.uint32(0x80000000)

    acc_ref[...] = jnp.zeros_like(acc_ref)
    c_ref[...] = jnp.zeros_like(c_ref)

    def block(kb, diag):
        start = pl.multiple_of(kb * blk, blk)
        cols = lambda ref, p: ref[0, pl.ds(start, blk), p * LANES:(p + 1) * LANES]
        z = [_dot_t(q_st[p], cols(k_ref, p)) for p in range(HEAD_PAIRS)]
        hi, lo = [], []
        for p in range(HEAD_PAIRS):
            neg_abs = pltpu.bitcast(pltpu.bitcast(z[p], jnp.uint32) | sign_bit, F32)
            sp = jnp.maximum(z[p], 0.0) + jnp.log(1.0 + jnp.exp2(neg_abs)) * LOG2_E
            if diag:
                sp = jnp.where(causal, sp, 0.0)
            top = pltpu.bitcast(pltpu.bitcast(sp, jnp.uint32) & hi_mask, F32)
            hi.append(top.astype(BF16))
            lo.append((sp - top).astype(BF16))
        cl = [_dot(hi[p], tri) + _dot(lo[p], tri) for p in range(HEAD_PAIRS)]
        w = []
        for p in range(HEAD_PAIRS):
            wp = jnp.exp2(z[p] - cl[p])
            if diag:
                wp = jnp.where(causal, wp, 0.0)
            w.append(wp.astype(BF16))
        pv = [_dot(w[p], cols(v_ref, p)) for p in range(HEAD_PAIRS)]
        for p in range(HEAD_PAIRS):
            carry = c_ref[p]
            acc_ref[p] += jnp.exp2(-carry) * pv[p]
            c_ref[p] = carry + jnp.broadcast_to(cl[p][:, 0:1], carry.shape)

    def min_carry():
        return jnp.min(jnp.min(c_ref[...], axis=0))

    block(qi, True)

    def cond(st):
        return (st[0] < qi) & (st[1] < SB_UNDERFLOW)

    def body(st):
        block(qi - 1 - st[0], False)
        return st[0] + 1, min_carry()

    lax.while_loop(cond, body, (jnp.int32(0), min_carry()))
    for p in range(HEAD_PAIRS):
        a = acc_ref[p]
        o_ref[0, :, p * LANES:(p + 1) * LANES] = jnp.where(first, a[:blk], a[blk:]).astype(o_ref.dtype)


def _sb_attention(q, k, v):
    b, s, wdt = q.shape
    blk = ATTN_BLOCK
    qspec = pl.BlockSpec((1, blk, wdt), lambda bi, qi: (bi, qi, 0))
    kspec = pl.BlockSpec((1, s, wdt), lambda bi, qi: (bi, 0, 0))
    state = pltpu.VMEM((HEAD_PAIRS, 2 * blk, LANES), F32)
    return pl.pallas_call(
        functools.partial(_sb_kernel, blk=blk),
        grid=(b, s // blk),
        in_specs=[qspec, kspec, kspec],
        out_specs=qspec,
        out_shape=jax.ShapeDtypeStruct(q.shape, BF16),
        scratch_shapes=[state, state],
        compiler_params=_params("parallel", "arbitrary"),
        name="sb_attention",
    )(q, k, v)


def _mla_kernel(qn_ref, qr_ref, kn_ref, kr_ref, v_ref, o_ref, acc_ref, m_ref, *, blk):
    qi = pl.program_id(1)
    lane2 = lax.broadcasted_iota(jnp.int32, (1, 2 * LANES), 1)
    sel0 = (lane2 < MLA_NOPE) | ((lane2 >= LANES) & (lane2 < LANES + MLA_ROPE))
    sel1 = ((lane2 >= MLA_NOPE) & (lane2 < LANES)) | (
        (lane2 >= LANES + MLA_ROPE) & (lane2 < LANES + 2 * MLA_ROPE))
    q_st = []
    for p in range(HEAD_PAIRS):
        sl = slice(p * LANES, (p + 1) * LANES)
        qcat = jnp.concatenate([qn_ref[0, :, sl], qr_ref[0, :, sl]], axis=1)
        zero = jnp.zeros_like(qcat)
        q_st.append(jnp.concatenate([jnp.where(sel0, qcat, zero), jnp.where(sel1, qcat, zero)], axis=0))
    r2 = lax.broadcasted_iota(jnp.int32, (2 * blk, blk), 0)
    c2 = lax.broadcasted_iota(jnp.int32, (2 * blk, blk), 1)
    visible = (c2 // CHUNK) <= (jnp.where(r2 >= blk, r2 - blk, r2) // CHUNK)
    lane = lax.broadcasted_iota(jnp.int32, (1, LANES), 1)
    first = lane < MLA_V

    acc_ref[...] = jnp.zeros_like(acc_ref)
    m_ref[...] = jnp.full(m_ref.shape, -jnp.inf, F32)

    def block(kb, diag):
        start = pl.multiple_of(kb * blk, blk)
        kr_blk = kr_ref[0, pl.ds(start, blk), :]
        sc = []
        for p in range(HEAD_PAIRS):
            kcat = jnp.concatenate([kn_ref[0, pl.ds(start, blk), p * LANES:(p + 1) * LANES], kr_blk], axis=1)
            sc.append(_dot_t(q_st[p], kcat))
        pr, alpha = [], []
        for p in range(HEAD_PAIRS):
            s_p = jnp.where(visible, sc[p], -jnp.inf) if diag else sc[p]
            m_prev = m_ref[p]
            m_new = jnp.maximum(m_prev, jnp.max(s_p, axis=1, keepdims=True))
            alpha.append(jnp.exp2(m_prev - m_new))
            pr.append(jnp.exp2(s_p - jnp.concatenate([m_new] * (blk // LANES), axis=1)).astype(BF16))
            m_ref[p] = m_new
        for h in range(MLA_HEADS):
            p, half = divmod(h, 2)
            rows = slice(half * blk, (half + 1) * blk)
            pv = _dot(pr[p][rows], v_ref[0, pl.ds(start, blk), h * LANES:(h + 1) * LANES])
            acc_ref[h] = alpha[p][rows] * acc_ref[h] + pv

    block(qi, True)

    def body(j, _):
        block(j, False)
        return 0

    lax.fori_loop(0, qi, body, 0)
    for p in range(HEAD_PAIRS):
        a0 = acc_ref[2 * p]
        a1 = acc_ref[2 * p + 1]
        o0 = a0 / pltpu.roll(a0, MLA_V, axis=1)
        o1 = pltpu.roll(a1 / pltpu.roll(a1, MLA_V, axis=1), MLA_V, axis=1)
        o_ref[0, :, p * LANES:(p + 1) * LANES] = jnp.where(first, o0, o1).astype(o_ref.dtype)


def _mla_attention(qn, qr, kn, kr, vcat):
    b, s, wdt = qn.shape
    blk = ATTN_BLOCK
    qspec = pl.BlockSpec((1, blk, wdt), lambda bi, qi: (bi, qi, 0))
    full = lambda a: pl.BlockSpec((1, s, a.shape[-1]), lambda bi, qi: (bi, 0, 0))
    return pl.pallas_call(
        functools.partial(_mla_kernel, blk=blk),
        grid=(b, s // blk),
        in_specs=[qspec, qspec, full(kn), full(kr), full(vcat)],
        out_specs=qspec,
        out_shape=jax.ShapeDtypeStruct(qn.shape, BF16),
        scratch_shapes=[pltpu.VMEM((MLA_HEADS, blk, LANES), F32),
                        pltpu.VMEM((HEAD_PAIRS, 2 * blk, LANES), F32)],
        compiler_params=_params("parallel", "arbitrary"),
        name="mla_attention",
    )(qn, qr, kn, kr, vcat)


def _route(logits):
    lane = lax.broadcasted_iota(jnp.int32, logits.shape, 1).astype(F32)
    ninf = -jnp.inf
    big = float(ROUTER_LANES)
    is_g = lane < N_GROUPS
    lg = jnp.where(is_g, logits, ninf)
    gmax = jnp.max(lg, axis=1, keepdims=True)
    gsum = jnp.sum(jnp.where(is_g, jnp.exp(lg - gmax), 0.0), axis=1, keepdims=True)
    gp = 1.0 / gsum
    g = jnp.min(jnp.where(lg == gmax, lane, big), axis=1, keepdims=True)
    lo = N_GROUPS + EXPERTS_PER_GROUP * g
    in_grp = (lane >= lo) & (lane < lo + EXPERTS_PER_GROUP)
    le = jnp.where(in_grp, logits, ninf)
    l1 = jnp.max(le, axis=1, keepdims=True)
    i1 = jnp.min(jnp.where(le == l1, lane, big), axis=1, keepdims=True)
    le2 = jnp.where(lane == i1, ninf, le)
    l2 = jnp.max(le2, axis=1, keepdims=True)
    i2 = jnp.min(jnp.where(le2 == l2, lane, big), axis=1, keepdims=True)
    t = jnp.exp(l2 - l1)
    w1 = gp / (1.0 + t)
    w2 = gp * t / (1.0 + t)
    return i1 - N_GROUPS, i2 - N_GROUPS, w1, w2


def _pack_bf16_pairs(x):
    n = x.shape[1] // 2
    xb = x.astype(BF16).astype(F32)
    hi = pltpu.bitcast(xb[:, :n], jnp.uint32)
    lo = pltpu.bitcast(xb[:, n:], jnp.uint32)
    return hi | (lo >> 16)


def _unpack_bf16_pairs(w):
    hi = pltpu.bitcast(w & jnp.uint32(0xFFFF0000), F32)
    lo = pltpu.bitcast(w << 16, F32)
    return hi, lo


R_E1, R_E2, R_W1, R_W2, R_RANK1, R_RANK2 = range(6)


def _mixer_out_kernel(h_ref, osb_ref, omla_ref, gosb_ref, gomla_ref, wout_ref, gmoe_ref,
                      wr_ref, br_ref, h1_ref, xp_ref, info_ref, cnt_ref):
    @pl.when(pl.program_id(0) == 0)
    def _():
        cnt_ref[...] = jnp.zeros_like(cnt_ref)

    nsb = _rms(osb_ref[...].astype(F32), gosb_ref[...]).astype(BF16)
    nmla = _rms(omla_ref[...].astype(F32), gomla_ref[...]).astype(BF16)
    h1 = h_ref[...] + _dot(nsb, wout_ref[0:SB_WIDTH, :]) + _dot(nmla, wout_ref[SB_WIDTH:, :])
    h1_ref[...] = h1
    xn = _rms(h1, gmoe_ref[...])
    xp_ref[...] = _pack_bf16_pairs(xn)
    logits = jnp.dot(xn, wr_ref[...], preferred_element_type=F32,
                     precision=lax.Precision.HIGHEST) + br_ref[...]
    e1, e2, w1, w2 = _route(logits)

    tm = logits.shape[0]
    lane = lax.broadcasted_iota(jnp.int32, logits.shape, 1).astype(F32)
    onehot = jnp.where((lane == e1) | (lane == e2), 1.0, 0.0)
    r = lax.broadcasted_iota(jnp.int32, (tm, tm), 0)
    c = lax.broadcasted_iota(jnp.int32, (tm, tm), 1)
    before = jnp.where(c < r, 1.0, 0.0).astype(BF16)
    seen = _dot(before, onehot.astype(BF16)) + cnt_ref[...]
    rank1 = jnp.sum(jnp.where(lane == e1, seen, 0.0), axis=1, keepdims=True)
    rank2 = jnp.sum(jnp.where(lane == e2, seen, 0.0), axis=1, keepdims=True)
    cnt_ref[...] += jnp.sum(onehot, axis=0, keepdims=True)

    info = jnp.zeros_like(logits)
    for idx, val in ((R_E1, e1), (R_E2, e2), (R_W1, w1), (R_W2, w2), (R_RANK1, rank1), (R_RANK2, rank2)):
        info = jnp.where(lane == idx, val, info)
    info_ref[...] = info


def _mixer_out(h, osb, omla, gosb, gomla, wout, gmoe, wr, br):
    t = h.shape[0]
    tm = TOKEN_TILE
    row = lambda w: pl.BlockSpec((tm, w), lambda i: (i, 0))
    full = lambda a: pl.BlockSpec(a.shape, lambda i: (0,) * a.ndim)
    return pl.pallas_call(
        _mixer_out_kernel,
        grid=(t // tm,),
        in_specs=[row(D_MODEL), row(SB_WIDTH), row(MLA_WIDTH), full(gosb), full(gomla), full(wout),
                  full(gmoe), full(wr), full(br)],
        out_specs=[row(D_MODEL), row(D_MODEL // 2), row(ROUTER_LANES),
                   pl.BlockSpec((1, ROUTER_LANES), lambda i: (0, 0))],
        out_shape=[jax.ShapeDtypeStruct((t, D_MODEL), F32), jax.ShapeDtypeStruct((t, D_MODEL // 2), jnp.uint32),
                   jax.ShapeDtypeStruct((t, ROUTER_LANES), F32), jax.ShapeDtypeStruct((1, ROUTER_LANES), F32)],
        compiler_params=_params("arbitrary"),
        name="mixer_out",
    )(h, osb, omla, gosb, gomla, wout, gmoe, wr, br)


def _route_plan(info, counts, n_rows):
    te = EXPERT_TILE
    cnt = counts[0, :N_EXPERTS].astype(jnp.int32)
    padded = (cnt + te - 1) // te * te
    seg_end = jnp.cumsum(padded)
    seg_start = seg_end - padded
    e = info[:, (R_E1, R_E2)].astype(jnp.int32)
    rank = info[:, (R_RANK1, R_RANK2)].astype(jnp.int32)
    pos = jnp.take(seg_start, e) + rank
    tile_start = jnp.arange(n_rows // te, dtype=jnp.int32) * te
    tile_expert = jnp.minimum(jnp.sum(tile_start[:, None] >= seg_end[None, :], axis=1), N_EXPERTS - 1)
    n_valid = (seg_end[-1] // te).reshape(1)
    return pos, tile_expert.astype(jnp.int32), n_valid.astype(jnp.int32)


def _dispatch_kernel(pos_ref, x_ref, init_hbm, xs_hbm, sem):
    del init_hbm
    tm = x_ref.shape[0]

    def issue(t, _):
        for k in range(2):
            pltpu.make_async_copy(x_ref.at[pl.ds(t, 1), :], xs_hbm.at[pl.ds(pos_ref[0, 0, 2 * t + k], 1), :],
                                  sem).start()
        return 0

    lax.fori_loop(0, tm, issue, 0, unroll=8)
    for _ in range(2):
        pltpu.make_async_copy(x_ref, xs_hbm.at[pl.ds(0, tm), :], sem).wait()


def _dispatch(pos, xp, n_rows):
    t, w = xp.shape
    tm = TOKEN_TILE
    init = jnp.zeros((n_rows, w), xp.dtype)
    return pl.pallas_call(
        _dispatch_kernel,
        grid=(t // tm,),
        in_specs=[pl.BlockSpec((1, 1, 2 * tm), lambda i: (i, 0, 0), memory_space=pltpu.SMEM),
                  pl.BlockSpec((tm, w), lambda i: (i, 0)),
                  pl.BlockSpec(memory_space=pl.ANY)],
        out_specs=pl.BlockSpec(memory_space=pl.ANY),
        out_shape=jax.ShapeDtypeStruct((n_rows, w), xp.dtype),
        scratch_shapes=[pltpu.SemaphoreType.DMA(())],
        input_output_aliases={2: 0},
        compiler_params=_params("arbitrary"),
        name="moe_dispatch",
    )(pos.reshape(t // tm, 1, 2 * tm), xp, init)


def _expert_kernel(te_ref, nv_ref, xs_ref, wg_ref, wu_ref, wd_ref, ys_ref):
    del te_ref
    i = pl.program_id(0)

    @pl.when(i < nv_ref[0])
    def _():
        half = D_MODEL // 2
        xa, xb = _unpack_bf16_pairs(xs_ref[...])
        xa = xa.astype(BF16)
        xb = xb.astype(BF16)
        a = _dot(xa, wg_ref[0, :half, :]) + _dot(xb, wg_ref[0, half:, :])
        u = _dot(xa, wu_ref[0, :half, :]) + _dot(xb, wu_ref[0, half:, :])
        hid = (a / (1.0 + jnp.exp(-a))) * u
        ys_ref[...] = _pack_bf16_pairs(_dot(hid.astype(BF16), wd_ref[0]))

    @pl.when(i >= nv_ref[0])
    def _():
        ys_ref[...] = jnp.zeros_like(ys_ref)


def _experts(tile_expert, n_valid, xs, wg, wu, wd):
    n_rows, w = xs.shape
    te = EXPERT_TILE
    last = lambda i, te_ref, nv_ref: jnp.minimum(i, nv_ref[0] - 1)
    wspec = lambda shape: pl.BlockSpec((1,) + shape, lambda i, te_ref, nv_ref: (te_ref[last(i, te_ref, nv_ref)], 0, 0))
    return pl.pallas_call(
        _expert_kernel,
        grid_spec=pltpu.PrefetchScalarGridSpec(
            num_scalar_prefetch=2,
            grid=(n_rows // te,),
            in_specs=[pl.BlockSpec((te, w), lambda i, te_ref, nv_ref: (last(i, te_ref, nv_ref), 0)),
                      wspec((D_MODEL, D_EXPERT)), wspec((D_MODEL, D_EXPERT)), wspec((D_EXPERT, D_MODEL))],
            out_specs=pl.BlockSpec((te, w), lambda i, te_ref, nv_ref: (i, 0)),
        ),
        out_shape=jax.ShapeDtypeStruct((n_rows, w), xs.dtype),
        compiler_params=_params("arbitrary"),
        name="moe_experts",
    )(tile_expert, n_valid, xs, wg, wu, wd)


def _combine_ple_kernel(pos_ref, info_ref, h_ref, p_ref, g_ref, wpg_ref, wpe_ref, gf_ref, ys_hbm,
                        o_ref, buf, sem, *, final):
    tm = h_ref.shape[0]

    def issue(t, _):
        for k in range(2):
            pltpu.make_async_copy(ys_hbm.at[pl.ds(pos_ref[0, 0, 2 * t + k], 1), :],
                                  buf.at[pl.ds(k * tm + t, 1), :], sem).start()
        return 0

    lax.fori_loop(0, tm, issue, 0, unroll=8)
    info = info_ref[...]
    lane = lax.broadcasted_iota(jnp.int32, info.shape, 1)
    gate1 = jnp.sum(jnp.where(lane == R_W1, info, 0.0), axis=1, keepdims=True)
    gate2 = jnp.sum(jnp.where(lane == R_W2, info, 0.0), axis=1, keepdims=True)
    pltpu.make_async_copy(ys_hbm.at[pl.ds(0, 2 * tm), :], buf, sem).wait()
    y1a, y1b = _unpack_bf16_pairs(buf[0:tm, :])
    y2a, y2b = _unpack_bf16_pairs(buf[tm:2 * tm, :])
    y = jnp.concatenate([gate1 * y1a + gate2 * y2a, gate1 * y1b + gate2 * y2b], axis=1)
    h = h_ref[...] + y
    xn = _rms(h, g_ref[...]).astype(BF16)
    gate = 1.0 / (1.0 + jnp.exp(-_dot(xn, wpg_ref[...])))
    out = h + gate * _dot(p_ref[...].astype(BF16), wpe_ref[...])
    if final:
        out = _rms(out, gf_ref[...])
    o_ref[...] = out


def _combine_ple(pos, info, h, p, g, wpg, wpe, gf, ys, final):
    t = h.shape[0]
    tm = TOKEN_TILE
    row = lambda w: pl.BlockSpec((tm, w), lambda i: (i, 0))
    full = lambda a: pl.BlockSpec(a.shape, lambda i: (0,) * a.ndim)
    return pl.pallas_call(
        functools.partial(_combine_ple_kernel, final=final),
        grid=(t // tm,),
        in_specs=[pl.BlockSpec((1, 1, 2 * tm), lambda i: (i, 0, 0), memory_space=pltpu.SMEM),
                  row(ROUTER_LANES), row(D_MODEL), row(D_PLE), full(g), full(wpg), full(wpe), full(gf),
                  pl.BlockSpec(memory_space=pl.ANY)],
        out_specs=row(D_MODEL),
        out_shape=jax.ShapeDtypeStruct((t, D_MODEL), F32),
        scratch_shapes=[pltpu.VMEM((2 * tm, ys.shape[1]), ys.dtype), pltpu.SemaphoreType.DMA(())],
        compiler_params=_params("arbitrary"),
        name="moe_combine_ple",
    )(pos.reshape(t // tm, 1, 2 * tm), info, h, p, g, wpg, wpe, gf, ys)


def _rot_cols(w):
    half = w.shape[-1] // 2
    return jnp.concatenate([-w[:, half:], w[:, :half]], axis=1)


def _prep_in(w_in):
    sb_scale = SB_HEAD_DIM ** -0.5 * LOG2_E
    kr = w_in[:, 3 * SB_WIDTH + MLA_Q_RANK + MLA_KV_RANK:]
    pad = jnp.zeros((w_in.shape[0], LANES - 2 * MLA_ROPE), w_in.dtype)
    kra = jnp.concatenate([kr, kr, pad], axis=1)
    krr = _rot_cols(kr)
    krb = jnp.concatenate([krr, krr, pad], axis=1)
    w1 = jnp.concatenate([w_in[:, :SB_WIDTH] * sb_scale,
                          w_in[:, SB_WIDTH:3 * SB_WIDTH + MLA_Q_RANK + MLA_KV_RANK], kra, krb], axis=1)
    return w1.astype(BF16)


def _prep_uq(w_uq):
    scale = MLA_QK ** -0.5 * LOG2_E
    w = w_uq.reshape(MLA_Q_RANK, MLA_HEADS, MLA_QK) * scale
    wqn = w[:, :, :MLA_NOPE].reshape(MLA_Q_RANK, MLA_HEADS * MLA_NOPE)
    rope = w[:, :, MLA_NOPE:]
    half = MLA_ROPE // 2
    rot = jnp.concatenate([-rope[:, :, half:], rope[:, :, :half]], axis=2)
    pad = jnp.zeros((MLA_Q_RANK, HEAD_PAIRS, LANES - 2 * MLA_ROPE), w.dtype)

    def pairs(r):
        return jnp.concatenate([r.reshape(MLA_Q_RANK, HEAD_PAIRS, 2 * MLA_ROPE), pad], axis=2).reshape(
            MLA_Q_RANK, HEAD_PAIRS * LANES)

    return wqn.astype(BF16), pairs(rope).astype(BF16), pairs(rot).astype(BF16)


def _prep_ukv(w_ukv):
    w = w_ukv.reshape(MLA_KV_RANK, MLA_HEADS, MLA_NOPE + MLA_V)
    wk = w[:, :, :MLA_NOPE].reshape(MLA_KV_RANK, MLA_HEADS * MLA_NOPE)
    pad = jnp.zeros((MLA_KV_RANK, MLA_HEADS, LANES - MLA_V), w.dtype)
    wv = jnp.concatenate([w[:, :, MLA_NOPE:], pad], axis=2).reshape(MLA_KV_RANK, MLA_HEADS * LANES)
    return wk.astype(BF16), wv.astype(BF16)


def _prep_router(w_rg, b_rg, w_re, b_re):
    pad = ROUTER_LANES - N_GROUPS - N_EXPERTS
    wr = jnp.concatenate([w_rg, w_re, jnp.zeros((D_MODEL, pad), F32)], axis=1)
    br = jnp.concatenate([b_rg, b_re, jnp.zeros((pad,), F32)])[None, :]
    return wr, br


def kernel(x, p, positions, g_mix, w_in, g_cq, w_uq, g_ckv, w_ukv, g_osb, g_omla, w_out, g_moe,
           w_rg, b_rg, w_re, b_re, w_gate, w_up, w_down, g_ple, w_pg, w_pe, g_final):
    b, s, d = x.shape
    t = b * s
    depth = w_in.shape[0]

    inv_freq = ROPE_THETA ** (-jnp.arange(0, MLA_ROPE, 2, dtype=F32) / MLA_ROPE)
    ang = positions.astype(F32)[..., None] * inv_freq
    reps = LANES // (MLA_ROPE // 2)
    cos = jnp.tile(jnp.cos(ang), (1, 1, reps)).reshape(t, LANES)
    sin = jnp.tile(jnp.sin(ang), (1, 1, reps)).reshape(t, LANES)

    h = x.reshape(t, d)
    n_rows = -(-(2 * t + N_EXPERTS * (EXPERT_TILE - 1)) // EXPERT_TILE) * EXPERT_TILE
    r3 = lambda a: a.reshape(b, s, a.shape[-1])
    r2 = lambda a: a.reshape(t, a.shape[-1])
    for i in range(depth):
        w1 = _prep_in(w_in[i])
        wqn, wqra, wqrb = _prep_uq(w_uq[i])
        wkvk, wkvv = _prep_ukv(w_ukv[i])
        qsb, ksb, vsb, qn, qr, kn, kr, vm = _mixer_in(
            h, g_mix[i][None], cos, sin, g_cq[i][None], g_ckv[i][None], w1, wqn, wqra, wqrb, wkvk, wkvv)
        osb = _sb_attention(r3(qsb), r3(ksb), r3(vsb))
        omla = _mla_attention(r3(qn), r3(qr), r3(kn), r3(kr), r3(vm))
        wr, br = _prep_router(w_rg[i], b_rg[i], w_re[i], b_re[i])
        h1, xp, info, counts = _mixer_out(h, r2(osb), r2(omla), g_osb[i][None], g_omla[i][None],
                                          w_out[i].astype(BF16), g_moe[i][None], wr, br)
        pos, tile_expert, n_valid = _route_plan(info, counts, n_rows)
        xs = _dispatch(pos, xp, n_rows)
        ys = _experts(tile_expert, n_valid, xs, w_gate[i].astype(BF16), w_up[i].astype(BF16),
                      w_down[i].astype(BF16))
        h = _combine_ple(pos, info, h1, p[i].reshape(t, D_PLE), g_ple[i][None], w_pg[i].astype(BF16),
                         w_pe[i].astype(BF16), g_final[None], ys, final=(i == depth - 1))
    return h.reshape(b, s, d)
```

```python
import functools

import jax
import jax.numpy as jnp
from jax import lax
from jax.experimental import pallas as pl
from jax.experimental.pallas import tpu as pltpu

F32 = jnp.float32
BF16 = jnp.bfloat16

D_MODEL = 1024
CHUNK = 64
D_PLE = 256
EPS = 1e-6
SB_HEADS = 8
SB_HEAD_DIM = 64
SB_WIDTH = SB_HEADS * SB_HEAD_DIM
MLA_HEADS = 8
MLA_NOPE = 64
MLA_ROPE = 32
MLA_QK = MLA_NOPE + MLA_ROPE
MLA_V = 64
MLA_Q_RANK = 384
MLA_KV_RANK = 256
MLA_WIDTH = MLA_HEADS * MLA_V
ROPE_THETA = 10000.0
N_GROUPS = 4
EXPERTS_PER_GROUP = 8
N_EXPERTS = N_GROUPS * EXPERTS_PER_GROUP
D_EXPERT = 256

LANES = 128
HEAD_PAIRS = SB_HEADS // 2
ROUTER_LANES = LANES
TOKEN_TILE = 512
EXPERT_TILE = 256
ATTN_BLOCK = 256
LOG2_E = 1.4426950408889634
SB_UNDERFLOW = 151.0
VMEM_LIMIT = 48 * 1024 * 1024


def _params(*sem):
    return pltpu.CompilerParams(dimension_semantics=sem, vmem_limit_bytes=VMEM_LIMIT)


def _rms(x, g):
    return x * lax.rsqrt(jnp.mean(x * x, axis=-1, keepdims=True) + EPS) * g


def _dot(a, b):
    return jnp.dot(a, b, preferred_element_type=F32)


def _dot_t(a, b):
    return lax.dot_general(a, b, (((1,), (1,)), ((), ())), preferred_element_type=F32)


def _mixer_in_kernel(x_ref, g_ref, cos_ref, sin_ref, gcq_ref, gckv_ref, w1_ref, wqn_ref,
                     wqra_ref, wqrb_ref, wkvk_ref, wkvv_ref,
                     qsb_ref, ksb_ref, vsb_ref, qn_ref, qr_ref, kn_ref, kr_ref, vm_ref):
    xn = _rms(x_ref[...], g_ref[...]).astype(BF16)
    c0 = 0
    qsb_ref[...] = _dot(xn, w1_ref[:, c0:c0 + SB_WIDTH]).astype(BF16)
    c0 += SB_WIDTH
    ksb_ref[...] = _dot(xn, w1_ref[:, c0:c0 + SB_WIDTH]).astype(BF16)
    c0 += SB_WIDTH
    vsb_ref[...] = _dot(xn, w1_ref[:, c0:c0 + SB_WIDTH]).astype(BF16)
    c0 += SB_WIDTH
    cq = _dot(xn, w1_ref[:, c0:c0 + MLA_Q_RANK])
    c0 += MLA_Q_RANK
    ckv = _dot(xn, w1_ref[:, c0:c0 + MLA_KV_RANK])
    c0 += MLA_KV_RANK
    kra = _dot(xn, w1_ref[:, c0:c0 + LANES])
    c0 += LANES
    krb = _dot(xn, w1_ref[:, c0:c0 + LANES])
    cos = cos_ref[...]
    sin = sin_ref[...]
    kr_ref[...] = (kra * cos + krb * sin).astype(BF16)

    cqn = _rms(cq, gcq_ref[...]).astype(BF16)
    qn_ref[...] = _dot(cqn, wqn_ref[...]).astype(BF16)
    ra = _dot(cqn, wqra_ref[...])
    rb = _dot(cqn, wqrb_ref[...])
    for p in range(HEAD_PAIRS):
        sl = slice(p * LANES, (p + 1) * LANES)
        qr_ref[:, sl] = (ra[:, sl] * cos + rb[:, sl] * sin).astype(BF16)

    ckvn = _rms(ckv, gckv_ref[...]).astype(BF16)
    kn_ref[...] = _dot(ckvn, wkvk_ref[...]).astype(BF16)
    lane = lax.broadcasted_iota(jnp.int32, (1, LANES), 1)
    for hd in range(MLA_HEADS):
        sl = slice(hd * LANES, (hd + 1) * LANES)
        vm_ref[:, sl] = jnp.where(lane < MLA_V, _dot(ckvn, wkvv_ref[:, sl]), 1.0).astype(BF16)


def _mixer_in(h, g, cos, sin, gcq, gckv, w1, wqn, wqra, wqrb, wkvk, wkvv):
    t = h.shape[0]
    tm = TOKEN_TILE
    row = lambda w: pl.BlockSpec((tm, w), lambda i: (i, 0))
    full = lambda a: pl.BlockSpec(a.shape, lambda i: (0,) * a.ndim)
    widths = [SB_WIDTH, SB_WIDTH, SB_WIDTH, MLA_WIDTH, MLA_WIDTH, MLA_WIDTH, LANES, MLA_HEADS * LANES]
    return pl.pallas_call(
        _mixer_in_kernel,
        grid=(t // tm,),
        in_specs=[row(D_MODEL), full(g), row(LANES), row(LANES), full(gcq), full(gckv),
                  full(w1), full(wqn), full(wqra), full(wqrb), full(wkvk), full(wkvv)],
        out_specs=[row(w) for w in widths],
        out_shape=[jax.ShapeDtypeStruct((t, w), BF16) for w in widths],
        compiler_params=_params("parallel"),
        name="mixer_in",
    )(h, g, cos, sin, gcq, gckv, w1, wqn, wqra, wqrb, wkvk, wkvv)


def _sb_kernel(q_ref, k_ref, v_ref, o_ref, acc_ref, c_ref, *, blk):
    qi = pl.program_id(1)
    lane = lax.broadcasted_iota(jnp.int32, (1, LANES), 1)
    first = lane < SB_HEAD_DIM
    q_st = []
    for p in range(HEAD_PAIRS):
        q2 = q_ref[0, :, p * LANES:(p + 1) * LANES]
        zero = jnp.zeros_like(q2)
        q_st.append(jnp.concatenate([jnp.where(first, q2, zero), jnp.where(first, zero, q2)], axis=0))
    r = lax.broadcasted_iota(jnp.int32, (blk, blk), 0)
    c = lax.broadcasted_iota(jnp.int32, (blk, blk), 1)
    tri = jnp.where(r >= c, 1.0, 0.0).astype(BF16)
    r2 = lax.broadcasted_iota(jnp.int32, (2 * blk, blk), 0)
    c2 = lax.broadcasted_iota(jnp.int32, (2 * blk, blk), 1)
    causal = c2 < jnp.where(r2 >= blk, r2 - blk, r2)
    hi_mask = jnp.uint32(0xFFFF0000)
    sign_bit = jnp.uint32(0x80000000)

    acc_ref[...] = jnp.zeros_like(acc_ref)
    c_ref[...] = jnp.zeros_like(c_ref)

    def block(kb, diag):
        start = pl.multiple_of(kb * blk, blk)
        cols = lambda ref, p: ref[0, pl.ds(start, blk), p * LANES:(p + 1) * LANES]
        z = [_dot_t(q_st[p], cols(k_ref, p)) for p in range(HEAD_PAIRS)]
        hi, lo = [], []
        for p in range(HEAD_PAIRS):
            neg_abs = pltpu.bitcast(pltpu.bitcast(z[p], jnp.uint32) | sign_bit, F32)
            sp = jnp.maximum(z[p], 0.0) + jnp.log(1.0 + jnp.exp2(neg_abs)) * LOG2_E
            if diag:
                sp = jnp.where(causal, sp, 0.0)
            top = pltpu.bitcast(pltpu.bitcast(sp, jnp.uint32) & hi_mask, F32)
            hi.append(top.astype(BF16))
            lo.append((sp - top).astype(BF16))
        cl = [_dot(hi[p], tri) + _dot(lo[p], tri) for p in range(HEAD_PAIRS)]
        w = []
        for p in range(HEAD_PAIRS):
            wp = jnp.exp2(z[p] - cl[p])
            if diag:
                wp = jnp.where(causal, wp, 0.0)
            w.append(wp.astype(BF16))
        pv = [_dot(w[p], cols(v_ref, p)) for p in range(HEAD_PAIRS)]
        for p in range(HEAD_PAIRS):
            carry = c_ref[p]
            acc_ref[p] += jnp.exp2(-carry) * pv[p]
            c_ref[p] = carry + jnp.broadcast_to(cl[p][:, 0:1], carry.shape)

    def min_carry():
        return jnp.min(jnp.min(c_ref[...], axis=0))

    block(qi, True)

    def cond(st):
        return (st[0] < qi) & (st[1] < SB_UNDERFLOW)

    def body(st):
        block(qi - 1 - st[0], False)
        return st[0] + 1, min_carry()

    lax.while_loop(cond, body, (jnp.int32(0), min_carry()))
    for p in range(HEAD_PAIRS):
        a = acc_ref[p]
        o_ref[0, :, p * LANES:(p + 1) * LANES] = jnp.where(first, a[:blk], a[blk:]).astype(o_ref.dtype)


def _sb_attention(q, k, v):
    b, s, wdt = q.shape
    blk = ATTN_BLOCK
    qspec = pl.BlockSpec((1, blk, wdt), lambda bi, qi: (bi, qi, 0))
    kspec = pl.BlockSpec((1, s, wdt), lambda bi, qi: (bi, 0, 0))
    state = pltpu.VMEM((HEAD_PAIRS, 2 * blk, LANES), F32)
    return pl.pallas_call(
        functools.partial(_sb_kernel, blk=blk),
        grid=(b, s // blk),
        in_specs=[qspec, kspec, kspec],
        out_specs=qspec,
        out_shape=jax.ShapeDtypeStruct(q.shape, BF16),
        scratch_shapes=[state, state],
        compiler_params=_params("parallel", "arbitrary"),
        name="sb_attention",
    )(q, k, v)


def _mla_kernel(qn_ref, qr_ref, kn_ref, kr_ref, v_ref, o_ref, acc_ref, m_ref, *, blk):
    qi = pl.program_id(1)
    lane2 = lax.broadcasted_iota(jnp.int32, (1, 2 * LANES), 1)
    sel0 = (lane2 < MLA_NOPE) | ((lane2 >= LANES) & (lane2 < LANES + MLA_ROPE))
    sel1 = ((lane2 >= MLA_NOPE) & (lane2 < LANES)) | (
        (lane2 >= LANES + MLA_ROPE) & (lane2 < LANES + 2 * MLA_ROPE))
    q_st = []
    for p in range(HEAD_PAIRS):
        sl = slice(p * LANES, (p + 1) * LANES)
        qcat = jnp.concatenate([qn_ref[0, :, sl], qr_ref[0, :, sl]], axis=1)
        zero = jnp.zeros_like(qcat)
        q_st.append(jnp.concatenate([jnp.where(sel0, qcat, zero), jnp.where(sel1, qcat, zero)], axis=0))
    r2 = lax.broadcasted_iota(jnp.int32, (2 * blk, blk), 0)
    c2 = lax.broadcasted_iota(jnp.int32, (2 * blk, blk), 1)
    visible = (c2 // CHUNK) <= (jnp.where(r2 >= blk, r2 - blk, r2) // CHUNK)
    lane = lax.broadcasted_iota(jnp.int32, (1, LANES), 1)
    first = lane < MLA_V

    acc_ref[...] = jnp.zeros_like(acc_ref)
    m_ref[...] = jnp.full(m_ref.shape, -jnp.inf, F32)

    def blocks(kbs, diag):
        starts = [pl.multiple_of(kb * blk, blk) for kb in kbs]
        sc = []
        for start in starts:
            kr_blk = kr_ref[0, pl.ds(start, blk), :]
            sc.append([_dot_t(q_st[p], jnp.concatenate(
                [kn_ref[0, pl.ds(start, blk), p * LANES:(p + 1) * LANES], kr_blk], axis=1))
                for p in range(HEAD_PAIRS)])
        pr, alpha = [], []
        for sc_b in sc:
            pr_b, alpha_b = [], []
            for p in range(HEAD_PAIRS):
                s_p = jnp.where(visible, sc_b[p], -jnp.inf) if diag else sc_b[p]
                m_prev = m_ref[p]
                m_new = jnp.maximum(m_prev, jnp.max(s_p, axis=1, keepdims=True))
                alpha_b.append(jnp.exp2(m_prev - m_new))
                pr_b.append(jnp.exp2(s_p - jnp.concatenate([m_new] * (blk // LANES), axis=1)).astype(BF16))
                m_ref[p] = m_new
            pr.append(pr_b)
            alpha.append(alpha_b)
        for h in range(MLA_HEADS):
            p, half = divmod(h, 2)
            rows = slice(half * blk, (half + 1) * blk)
            a = acc_ref[h]
            for start, pr_b, alpha_b in zip(starts, pr, alpha):
                a = alpha_b[p][rows] * a + _dot(pr_b[p][rows], v_ref[0, pl.ds(start, blk), h * LANES:(h + 1) * LANES])
            acc_ref[h] = a

    blocks([qi], True)

    def body(j, _):
        blocks([2 * j, 2 * j + 1], False)
        return 0

    lax.fori_loop(0, qi // 2, body, 0)

    @pl.when(qi % 2 == 1)
    def _():
        blocks([qi - 1], False)

    for p in range(HEAD_PAIRS):
        a0 = acc_ref[2 * p]
        a1 = acc_ref[2 * p + 1]
        o0 = a0 / pltpu.roll(a0, MLA_V, axis=1)
        o1 = pltpu.roll(a1 / pltpu.roll(a1, MLA_V, axis=1), MLA_V, axis=1)
        o_ref[0, :, p * LANES:(p + 1) * LANES] = jnp.where(first, o0, o1).astype(o_ref.dtype)


def _mla_attention(qn, qr, kn, kr, vcat):
    b, s, wdt = qn.shape
    blk = ATTN_BLOCK
    qspec = pl.BlockSpec((1, blk, wdt), lambda bi, qi: (bi, qi, 0))
    full = lambda a: pl.BlockSpec((1, s, a.shape[-1]), lambda bi, qi: (bi, 0, 0))
    return pl.pallas_call(
        functools.partial(_mla_kernel, blk=blk),
        grid=(b, s // blk),
        in_specs=[qspec, qspec, full(kn), full(kr), full(vcat)],
        out_specs=qspec,
        out_shape=jax.ShapeDtypeStruct(qn.shape, BF16),
        scratch_shapes=[pltpu.VMEM((MLA_HEADS, blk, LANES), F32),
                        pltpu.VMEM((HEAD_PAIRS, 2 * blk, LANES), F32)],
        compiler_params=_params("parallel", "arbitrary"),
        name="mla_attention",
    )(qn, qr, kn, kr, vcat)


def _route(logits):
    lane = lax.broadcasted_iota(jnp.int32, logits.shape, 1).astype(F32)
    ninf = -jnp.inf
    big = float(ROUTER_LANES)
    is_g = lane < N_GROUPS
    lg = jnp.where(is_g, logits, ninf)
    gmax = jnp.max(lg, axis=1, keepdims=True)
    gsum = jnp.sum(jnp.where(is_g, jnp.exp(lg - gmax), 0.0), axis=1, keepdims=True)
    gp = 1.0 / gsum
    g = jnp.min(jnp.where(lg == gmax, lane, big), axis=1, keepdims=True)
    lo = N_GROUPS + EXPERTS_PER_GROUP * g
    in_grp = (lane >= lo) & (lane < lo + EXPERTS_PER_GROUP)
    le = jnp.where(in_grp, logits, ninf)
    l1 = jnp.max(le, axis=1, keepdims=True)
    i1 = jnp.min(jnp.where(le == l1, lane, big), axis=1, keepdims=True)
    le2 = jnp.where(lane == i1, ninf, le)
    l2 = jnp.max(le2, axis=1, keepdims=True)
    i2 = jnp.min(jnp.where(le2 == l2, lane, big), axis=1, keepdims=True)
    t = jnp.exp(l2 - l1)
    w1 = gp / (1.0 + t)
    w2 = gp * t / (1.0 + t)
    return i1 - N_GROUPS, i2 - N_GROUPS, w1, w2


def _pack_bf16_pairs(x):
    n = x.shape[1] // 2
    xb = x.astype(BF16).astype(F32)
    hi = pltpu.bitcast(xb[:, :n], jnp.uint32)
    lo = pltpu.bitcast(xb[:, n:], jnp.uint32)
    return hi | (lo >> 16)


def _unpack_bf16_pairs(w):
    hi = pltpu.bitcast(w & jnp.uint32(0xFFFF0000), F32)
    lo = pltpu.bitcast(w << 16, F32)
    return hi, lo


R_E1, R_E2, R_W1, R_W2, R_RANK1, R_RANK2 = range(6)


def _mixer_out_kernel(h_ref, osb_ref, omla_ref, gosb_ref, gomla_ref, wout_ref, gmoe_ref,
                      wrh_ref, wrl_ref, br_ref, h1_ref, xp_ref, info_ref, cnt_ref):
    @pl.when(pl.program_id(0) == 0)
    def _():
        cnt_ref[...] = jnp.zeros_like(cnt_ref)

    nsb = _rms(osb_ref[...].astype(F32), gosb_ref[...]).astype(BF16)
    nmla = _rms(omla_ref[...].astype(F32), gomla_ref[...]).astype(BF16)
    h1 = h_ref[...] + _dot(nsb, wout_ref[0:SB_WIDTH, :]) + _dot(nmla, wout_ref[SB_WIDTH:, :])
    h1_ref[...] = h1
    xn = _rms(h1, gmoe_ref[...])
    xp_ref[...] = _pack_bf16_pairs(xn)
    x_hi = xn.astype(BF16)
    x_lo = (xn - x_hi.astype(F32)).astype(BF16)
    logits = (_dot(x_hi, wrh_ref[...]) + _dot(x_hi, wrl_ref[...]) + _dot(x_lo, wrh_ref[...])) + br_ref[...]
    e1, e2, w1, w2 = _route(logits)

    tm = logits.shape[0]
    lane = lax.broadcasted_iota(jnp.int32, logits.shape, 1).astype(F32)
    onehot = jnp.where((lane == e1) | (lane == e2), 1.0, 0.0)
    r = lax.broadcasted_iota(jnp.int32, (tm, tm), 0)
    c = lax.broadcasted_iota(jnp.int32, (tm, tm), 1)
    before = jnp.where(c < r, 1.0, 0.0).astype(BF16)
    seen = _dot(before, onehot.astype(BF16)) + cnt_ref[...]
    rank1 = jnp.sum(jnp.where(lane == e1, seen, 0.0), axis=1, keepdims=True)
    rank2 = jnp.sum(jnp.where(lane == e2, seen, 0.0), axis=1, keepdims=True)
    cnt_ref[...] += jnp.sum(onehot, axis=0, keepdims=True)

    info = jnp.zeros_like(logits)
    for idx, val in ((R_E1, e1), (R_E2, e2), (R_W1, w1), (R_W2, w2), (R_RANK1, rank1), (R_RANK2, rank2)):
        info = jnp.where(lane == idx, val, info)
    info_ref[...] = info


def _mixer_out(h, osb, omla, gosb, gomla, wout, gmoe, wrh, wrl, br):
    t = h.shape[0]
    tm = TOKEN_TILE
    row = lambda w: pl.BlockSpec((tm, w), lambda i: (i, 0))
    full = lambda a: pl.BlockSpec(a.shape, lambda i: (0,) * a.ndim)
    return pl.pallas_call(
        _mixer_out_kernel,
        grid=(t // tm,),
        in_specs=[row(D_MODEL), row(SB_WIDTH), row(MLA_WIDTH), full(gosb), full(gomla), full(wout),
                  full(gmoe), full(wrh), full(wrl), full(br)],
        out_specs=[row(D_MODEL), row(D_MODEL // 2), row(ROUTER_LANES),
                   pl.BlockSpec((1, ROUTER_LANES), lambda i: (0, 0))],
        out_shape=[jax.ShapeDtypeStruct((t, D_MODEL), F32), jax.ShapeDtypeStruct((t, D_MODEL // 2), jnp.uint32),
                   jax.ShapeDtypeStruct((t, ROUTER_LANES), F32), jax.ShapeDtypeStruct((1, ROUTER_LANES), F32)],
        compiler_params=_params("arbitrary"),
        name="mixer_out",
    )(h, osb, omla, gosb, gomla, wout, gmoe, wrh, wrl, br)


def _route_plan(info, counts, n_rows):
    te = EXPERT_TILE
    cnt = counts[0, :N_EXPERTS].astype(jnp.int32)
    padded = (cnt + te - 1) // te * te
    seg_end = jnp.cumsum(padded)
    seg_start = seg_end - padded
    e = info[:, (R_E1, R_E2)].astype(jnp.int32)
    rank = info[:, (R_RANK1, R_RANK2)].astype(jnp.int32)
    pos = jnp.take(seg_start, e) + rank
    tile_start = jnp.arange(n_rows // te, dtype=jnp.int32) * te
    tile_expert = jnp.minimum(jnp.sum(tile_start[:, None] >= seg_end[None, :], axis=1), N_EXPERTS - 1)
    n_valid = (seg_end[-1] // te).reshape(1)
    return pos, tile_expert.astype(jnp.int32), n_valid.astype(jnp.int32)


def _dispatch_kernel(pos_ref, x_ref, init_hbm, xs_hbm, sem):
    del init_hbm
    tm = x_ref.shape[0]

    def issue(t, _):
        for k in range(2):
            pltpu.make_async_copy(x_ref.at[pl.ds(t, 1), :], xs_hbm.at[pl.ds(pos_ref[0, 0, 2 * t + k], 1), :],
                                  sem).start(priority=k)
        return 0

    lax.fori_loop(0, tm, issue, 0, unroll=8)
    for _ in range(2):
        pltpu.make_async_copy(x_ref, xs_hbm.at[pl.ds(0, tm), :], sem).wait()


def _dispatch(pos, xp, n_rows):
    t, w = xp.shape
    tm = TOKEN_TILE
    init = jnp.zeros((n_rows, w), xp.dtype)
    return pl.pallas_call(
        _dispatch_kernel,
        grid=(t // tm,),
        in_specs=[pl.BlockSpec((1, 1, 2 * tm), lambda i: (i, 0, 0), memory_space=pltpu.SMEM),
                  pl.BlockSpec((tm, w), lambda i: (i, 0)),
                  pl.BlockSpec(memory_space=pl.ANY)],
        out_specs=pl.BlockSpec(memory_space=pl.ANY),
        out_shape=jax.ShapeDtypeStruct((n_rows, w), xp.dtype),
        scratch_shapes=[pltpu.SemaphoreType.DMA(())],
        input_output_aliases={2: 0},
        compiler_params=_params("arbitrary"),
        name="moe_dispatch",
    )(pos.reshape(t // tm, 1, 2 * tm), xp, init)


def _expert_kernel(te_ref, nv_ref, xs_ref, wg_ref, wu_ref, wd_ref, ys_ref):
    del te_ref
    i = pl.program_id(0)

    @pl.when(i < nv_ref[0])
    def _():
        half = D_MODEL // 2
        xa, xb = _unpack_bf16_pairs(xs_ref[...])
        xa = xa.astype(BF16)
        xb = xb.astype(BF16)
        a = _dot(xa, wg_ref[0, :half, :]) + _dot(xb, wg_ref[0, half:, :])
        u = _dot(xa, wu_ref[0, :half, :]) + _dot(xb, wu_ref[0, half:, :])
        hid = (a / (1.0 + jnp.exp(-a))) * u
        ys_ref[...] = _pack_bf16_pairs(_dot(hid.astype(BF16), wd_ref[0]))

    @pl.when(i >= nv_ref[0])
    def _():
        ys_ref[...] = jnp.zeros_like(ys_ref)


def _experts(tile_expert, n_valid, xs, wg, wu, wd):
    n_rows, w = xs.shape
    te = EXPERT_TILE
    last = lambda i, te_ref, nv_ref: jnp.minimum(i, nv_ref[0] - 1)
    wspec = lambda shape: pl.BlockSpec((1,) + shape, lambda i, te_ref, nv_ref: (te_ref[last(i, te_ref, nv_ref)], 0, 0))
    return pl.pallas_call(
        _expert_kernel,
        grid_spec=pltpu.PrefetchScalarGridSpec(
            num_scalar_prefetch=2,
            grid=(n_rows // te,),
            in_specs=[pl.BlockSpec((te, w), lambda i, te_ref, nv_ref: (last(i, te_ref, nv_ref), 0)),
                      wspec((D_MODEL, D_EXPERT)), wspec((D_MODEL, D_EXPERT)), wspec((D_EXPERT, D_MODEL))],
            out_specs=pl.BlockSpec((te, w), lambda i, te_ref, nv_ref: (i, 0)),
        ),
        out_shape=jax.ShapeDtypeStruct((n_rows, w), xs.dtype),
        compiler_params=_params("arbitrary"),
        name="moe_experts",
    )(tile_expert, n_valid, xs, wg, wu, wd)


def _combine_ple_kernel(pos_ref, pos_next_ref, info_ref, h_ref, p_ref, g_ref, wpg_ref, wpe_ref, gf_ref, ys_hbm,
                        o_ref, buf, sem, *, final):
    i = pl.program_id(0)
    tm = h_ref.shape[0]
    slot = i % 2

    def gather(rows_ref, dst):
        def issue(t, _):
            for k in range(2):
                pltpu.make_async_copy(ys_hbm.at[pl.ds(rows_ref[0, 0, 2 * t + k], 1), :],
                                      buf.at[dst, pl.ds(k * tm + t, 1), :], sem.at[dst]).start(priority=k)
            return 0

        lax.fori_loop(0, tm, issue, 0, unroll=8)

    @pl.when(i == 0)
    def _():
        gather(pos_ref, 0)

    @pl.when(i + 1 < pl.num_programs(0))
    def _():
        gather(pos_next_ref, 1 - slot)

    info = info_ref[...]
    lane = lax.broadcasted_iota(jnp.int32, info.shape, 1)
    gate1 = jnp.sum(jnp.where(lane == R_W1, info, 0.0), axis=1, keepdims=True)
    gate2 = jnp.sum(jnp.where(lane == R_W2, info, 0.0), axis=1, keepdims=True)
    pltpu.make_async_copy(ys_hbm.at[pl.ds(0, 2 * tm), :], buf.at[slot], sem.at[slot]).wait()
    y1a, y1b = _unpack_bf16_pairs(buf[slot, 0:tm, :])
    y2a, y2b = _unpack_bf16_pairs(buf[slot, tm:2 * tm, :])
    y = jnp.concatenate([gate1 * y1a + gate2 * y2a, gate1 * y1b + gate2 * y2b], axis=1)
    h = h_ref[...] + y
    xn = _rms(h, g_ref[...]).astype(BF16)
    gate = 1.0 / (1.0 + jnp.exp(-_dot(xn, wpg_ref[...])))
    out = h + gate * _dot(p_ref[...].astype(BF16), wpe_ref[...])
    if final:
        out = _rms(out, gf_ref[...])
    o_ref[...] = out


def _combine_ple(pos, info, h, p, g, wpg, wpe, gf, ys, final):
    t = h.shape[0]
    tm = TOKEN_TILE
    n = t // tm
    row = lambda w: pl.BlockSpec((tm, w), lambda i: (i, 0))
    full = lambda a: pl.BlockSpec(a.shape, lambda i: (0,) * a.ndim)
    pos3 = pos.reshape(n, 1, 2 * tm)
    return pl.pallas_call(
        functools.partial(_combine_ple_kernel, final=final),
        grid=(n,),
        in_specs=[pl.BlockSpec((1, 1, 2 * tm), lambda i: (i, 0, 0), memory_space=pltpu.SMEM),
                  pl.BlockSpec((1, 1, 2 * tm), lambda i: (jnp.minimum(i + 1, n - 1), 0, 0), memory_space=pltpu.SMEM),
                  row(ROUTER_LANES), row(D_MODEL), row(D_PLE), full(g), full(wpg), full(wpe), full(gf),
                  pl.BlockSpec(memory_space=pl.ANY)],
        out_specs=row(D_MODEL),
        out_shape=jax.ShapeDtypeStruct((t, D_MODEL), F32),
        scratch_shapes=[pltpu.VMEM((2, 2 * tm, ys.shape[1]), ys.dtype), pltpu.SemaphoreType.DMA((2,))],
        compiler_params=_params("arbitrary"),
        name="moe_combine_ple",
    )(pos3, pos3, info, h, p, g, wpg, wpe, gf, ys)


def _rot_cols(w):
    half = w.shape[-1] // 2
    return jnp.concatenate([-w[:, half:], w[:, :half]], axis=1)


def _prep_in(w_in):
    sb_scale = SB_HEAD_DIM ** -0.5 * LOG2_E
    kr = w_in[:, 3 * SB_WIDTH + MLA_Q_RANK + MLA_KV_RANK:]
    pad = jnp.zeros((w_in.shape[0], LANES - 2 * MLA_ROPE), w_in.dtype)
    kra = jnp.concatenate([kr, kr, pad], axis=1)
    krr = _rot_cols(kr)
    krb = jnp.concatenate([krr, krr, pad], axis=1)
    w1 = jnp.concatenate([w_in[:, :SB_WIDTH] * sb_scale,
                          w_in[:, SB_WIDTH:3 * SB_WIDTH + MLA_Q_RANK + MLA_KV_RANK], kra, krb], axis=1)
    return w1.astype(BF16)


def _prep_uq(w_uq):
    scale = MLA_QK ** -0.5 * LOG2_E
    w = w_uq.reshape(MLA_Q_RANK, MLA_HEADS, MLA_QK) * scale
    wqn = w[:, :, :MLA_NOPE].reshape(MLA_Q_RANK, MLA_HEADS * MLA_NOPE)
    rope = w[:, :, MLA_NOPE:]
    half = MLA_ROPE // 2
    rot = jnp.concatenate([-rope[:, :, half:], rope[:, :, :half]], axis=2)
    pad = jnp.zeros((MLA_Q_RANK, HEAD_PAIRS, LANES - 2 * MLA_ROPE), w.dtype)

    def pairs(r):
        return jnp.concatenate([r.reshape(MLA_Q_RANK, HEAD_PAIRS, 2 * MLA_ROPE), pad], axis=2).reshape(
            MLA_Q_RANK, HEAD_PAIRS * LANES)

    return wqn.astype(BF16), pairs(rope).astype(BF16), pairs(rot).astype(BF16)


def _prep_ukv(w_ukv):
    w = w_ukv.reshape(MLA_KV_RANK, MLA_HEADS, MLA_NOPE + MLA_V)
    wk = w[:, :, :MLA_NOPE].reshape(MLA_KV_RANK, MLA_HEADS * MLA_NOPE)
    pad = jnp.zeros((MLA_KV_RANK, MLA_HEADS, LANES - MLA_V), w.dtype)
    wv = jnp.concatenate([w[:, :, MLA_NOPE:], pad], axis=2).reshape(MLA_KV_RANK, MLA_HEADS * LANES)
    return wk.astype(BF16), wv.astype(BF16)


def _prep_router(w_rg, b_rg, w_re, b_re):
    pad = ROUTER_LANES - N_GROUPS - N_EXPERTS
    wr = jnp.concatenate([w_rg, w_re, jnp.zeros((D_MODEL, pad), F32)], axis=1)
    br = jnp.concatenate([b_rg, b_re, jnp.zeros((pad,), F32)])[None, :]
    wr_hi = wr.astype(BF16)
    wr_lo = (wr - wr_hi.astype(F32)).astype(BF16)
    return wr_hi, wr_lo, br


def kernel(x, p, positions, g_mix, w_in, g_cq, w_uq, g_ckv, w_ukv, g_osb, g_omla, w_out, g_moe,
           w_rg, b_rg, w_re, b_re, w_gate, w_up, w_down, g_ple, w_pg, w_pe, g_final):
    b, s, d = x.shape
    t = b * s
    depth = w_in.shape[0]

    inv_freq = ROPE_THETA ** (-jnp.arange(0, MLA_ROPE, 2, dtype=F32) / MLA_ROPE)
    ang = positions.astype(F32)[..., None] * inv_freq
    reps = LANES // (MLA_ROPE // 2)
    cos = jnp.tile(jnp.cos(ang), (1, 1, reps)).reshape(t, LANES)
    sin = jnp.tile(jnp.sin(ang), (1, 1, reps)).reshape(t, LANES)

    h = x.reshape(t, d)
    n_rows = -(-(2 * t + N_EXPERTS * (EXPERT_TILE - 1)) // EXPERT_TILE) * EXPERT_TILE
    r3 = lambda a: a.reshape(b, s, a.shape[-1])
    r2 = lambda a: a.reshape(t, a.shape[-1])
    for i in range(depth):
        w1 = _prep_in(w_in[i])
        wqn, wqra, wqrb = _prep_uq(w_uq[i])
        wkvk, wkvv = _prep_ukv(w_ukv[i])
        qsb, ksb, vsb, qn, qr, kn, kr, vm = _mixer_in(
            h, g_mix[i][None], cos, sin, g_cq[i][None], g_ckv[i][None], w1, wqn, wqra, wqrb, wkvk, wkvv)
        osb = _sb_attention(r3(qsb), r3(ksb), r3(vsb))
        omla = _mla_attention(r3(qn), r3(qr), r3(kn), r3(kr), r3(vm))
        wrh, wrl, br = _prep_router(w_rg[i], b_rg[i], w_re[i], b_re[i])
        h1, xp, info, counts = _mixer_out(h, r2(osb), r2(omla), g_osb[i][None], g_omla[i][None],
                                          w_out[i].astype(BF16), g_moe[i][None], wrh, wrl, br)
        pos, tile_expert, n_valid = _route_plan(info, counts, n_rows)
        xs = _dispatch(pos, xp, n_rows)
        ys = _experts(tile_expert, n_valid, xs, w_gate[i].astype(BF16), w_up[i].astype(BF16),
                      w_down[i].astype(BF16))
        h = _combine_ple(pos, info, h1, p[i].reshape(t, D_PLE), g_ple[i][None], w_pg[i].astype(BF16),
                         w_pe[i].astype(BF16), g_final[None], ys, final=(i == depth - 1))
    return h.reshape(b, s, d)
```

```python
import functools

import jax
import jax.numpy as jnp
from jax import lax
from jax.experimental import pallas as pl
from jax.experimental.pallas import tpu as pltpu

F32 = jnp.float32
BF16 = jnp.bfloat16

D_MODEL = 1024
CHUNK = 64
D_PLE = 256
EPS = 1e-6
SB_HEADS = 8
SB_HEAD_DIM = 64
SB_WIDTH = SB_HEADS * SB_HEAD_DIM
MLA_HEADS = 8
MLA_NOPE = 64
MLA_ROPE = 32
MLA_QK = MLA_NOPE + MLA_ROPE
MLA_V = 64
MLA_Q_RANK = 384
MLA_KV_RANK = 256
MLA_WIDTH = MLA_HEADS * MLA_V
ROPE_THETA = 10000.0
N_GROUPS = 4
EXPERTS_PER_GROUP = 8
N_EXPERTS = N_GROUPS * EXPERTS_PER_GROUP
D_EXPERT = 256

LANES = 128
HEAD_PAIRS = SB_HEADS // 2
ROUTER_LANES = LANES
TOKEN_TILE = 512
EXPERT_TILE = 256
PACK_SUBLANES = D_MODEL // 2 // LANES
ATTN_BLOCK = 256
LOG2_E = 1.4426950408889634
SB_UNDERFLOW = 151.0
VMEM_LIMIT = 48 * 1024 * 1024


def _params(*sem):
    return pltpu.CompilerParams(dimension_semantics=sem, vmem_limit_bytes=VMEM_LIMIT)


def _rms(x, g):
    return x * lax.rsqrt(jnp.mean(x * x, axis=-1, keepdims=True) + EPS) * g


def _dot(a, b):
    return jnp.dot(a, b, preferred_element_type=F32)


def _dot_t(a, b):
    return lax.dot_general(a, b, (((1,), (1,)), ((), ())), preferred_element_type=F32)


def _mixer_in_kernel(x_ref, g_ref, cos_ref, sin_ref, gcq_ref, gckv_ref, w1_ref, wqn_ref,
                     wqra_ref, wqrb_ref, wkvk_ref, wkvv_ref,
                     qsb_ref, ksb_ref, vsb_ref, qn_ref, qr_ref, kn_ref, kr_ref, vm_ref):
    xn = _rms(x_ref[...], g_ref[...]).astype(BF16)
    c0 = 0
    qsb_ref[...] = _dot(xn, w1_ref[:, c0:c0 + SB_WIDTH]).astype(BF16)
    c0 += SB_WIDTH
    ksb_ref[...] = _dot(xn, w1_ref[:, c0:c0 + SB_WIDTH]).astype(BF16)
    c0 += SB_WIDTH
    vsb_ref[...] = _dot(xn, w1_ref[:, c0:c0 + SB_WIDTH]).astype(BF16)
    c0 += SB_WIDTH
    cq = _dot(xn, w1_ref[:, c0:c0 + MLA_Q_RANK])
    c0 += MLA_Q_RANK
    ckv = _dot(xn, w1_ref[:, c0:c0 + MLA_KV_RANK])
    c0 += MLA_KV_RANK
    kra = _dot(xn, w1_ref[:, c0:c0 + LANES])
    c0 += LANES
    krb = _dot(xn, w1_ref[:, c0:c0 + LANES])
    cos = cos_ref[...]
    sin = sin_ref[...]
    kr_ref[...] = (kra * cos + krb * sin).astype(BF16)

    cqn = _rms(cq, gcq_ref[...]).astype(BF16)
    qn_ref[...] = _dot(cqn, wqn_ref[...]).astype(BF16)
    ra = _dot(cqn, wqra_ref[...])
    rb = _dot(cqn, wqrb_ref[...])
    for p in range(HEAD_PAIRS):
        sl = slice(p * LANES, (p + 1) * LANES)
        qr_ref[:, sl] = (ra[:, sl] * cos + rb[:, sl] * sin).astype(BF16)

    ckvn = _rms(ckv, gckv_ref[...]).astype(BF16)
    kn_ref[...] = _dot(ckvn, wkvk_ref[...]).astype(BF16)
    lane = lax.broadcasted_iota(jnp.int32, (1, LANES), 1)
    for hd in range(MLA_HEADS):
        sl = slice(hd * LANES, (hd + 1) * LANES)
        vm_ref[:, sl] = jnp.where(lane < MLA_V, _dot(ckvn, wkvv_ref[:, sl]), 1.0).astype(BF16)


def _mixer_in(h, g, cos, sin, gcq, gckv, w1, wqn, wqra, wqrb, wkvk, wkvv):
    t = h.shape[0]
    tm = TOKEN_TILE
    row = lambda w: pl.BlockSpec((tm, w), lambda i: (i, 0))
    full = lambda a: pl.BlockSpec(a.shape, lambda i: (0,) * a.ndim)
    widths = [SB_WIDTH, SB_WIDTH, SB_WIDTH, MLA_WIDTH, MLA_WIDTH, MLA_WIDTH, LANES, MLA_HEADS * LANES]
    return pl.pallas_call(
        _mixer_in_kernel,
        grid=(t // tm,),
        in_specs=[row(D_MODEL), full(g), row(LANES), row(LANES), full(gcq), full(gckv),
                  full(w1), full(wqn), full(wqra), full(wqrb), full(wkvk), full(wkvv)],
        out_specs=[row(w) for w in widths],
        out_shape=[jax.ShapeDtypeStruct((t, w), BF16) for w in widths],
        compiler_params=_params("parallel"),
        name="mixer_in",
    )(h, g, cos, sin, gcq, gckv, w1, wqn, wqra, wqrb, wkvk, wkvv)


def _sb_kernel(q_ref, k_ref, v_ref, o_ref, acc_ref, c_ref, *, blk):
    qi = pl.program_id(1)
    lane = lax.broadcasted_iota(jnp.int32, (1, LANES), 1)
    first = lane < SB_HEAD_DIM
    q_st = []
    for p in range(HEAD_PAIRS):
        q2 = q_ref[0, :, p * LANES:(p + 1) * LANES]
        zero = jnp.zeros_like(q2)
        q_st.append(jnp.concatenate([jnp.where(first, q2, zero), jnp.where(first, zero, q2)], axis=0))
    r = lax.broadcasted_iota(jnp.int32, (blk, blk), 0)
    c = lax.broadcasted_iota(jnp.int32, (blk, blk), 1)
    tri = jnp.where(r >= c, 1.0, 0.0).astype(BF16)
    r2 = lax.broadcasted_iota(jnp.int32, (2 * blk, blk), 0)
    c2 = lax.broadcasted_iota(jnp.int32, (2 * blk, blk), 1)
    causal = c2 < jnp.where(r2 >= blk, r2 - blk, r2)
    hi_mask = jnp.uint32(0xFFFF0000)
    sign_bit = jnp.uint32(0x80000000)

    acc_ref[...] = jnp.zeros_like(acc_ref)
    c_ref[...] = jnp.zeros_like(c_ref)

    def block(kb, diag):
        start = pl.multiple_of(kb * blk, blk)
        cols = lambda ref, p: ref[0, pl.ds(start, blk), p * LANES:(p + 1) * LANES]
        z = [_dot_t(q_st[p], cols(k_ref, p)) for p in range(HEAD_PAIRS)]
        hi, lo = [], []
        for p in range(HEAD_PAIRS):
            neg_abs = pltpu.bitcast(pltpu.bitcast(z[p], jnp.uint32) | sign_bit, F32)
            sp = jnp.maximum(z[p], 0.0) + jnp.log(1.0 + jnp.exp2(neg_abs)) * LOG2_E
            if diag:
                sp = jnp.where(causal, sp, 0.0)
            top = pltpu.bitcast(pltpu.bitcast(sp, jnp.uint32) & hi_mask, F32)
            hi.append(top.astype(BF16))
            lo.append((sp - top).astype(BF16))
        cl = [_dot(hi[p], tri) + _dot(lo[p], tri) for p in range(HEAD_PAIRS)]
        w = []
        for p in range(HEAD_PAIRS):
            wp = jnp.exp2(z[p] - cl[p])
            if diag:
                wp = jnp.where(causal, wp, 0.0)
            w.append(wp.astype(BF16))
        pv = [_dot(w[p], cols(v_ref, p)) for p in range(HEAD_PAIRS)]
        for p in range(HEAD_PAIRS):
            carry = c_ref[p]
            acc_ref[p] += jnp.exp2(-carry) * pv[p]
            c_ref[p] = carry + jnp.broadcast_to(cl[p][:, 0:1], carry.shape)

    def min_carry():
        return jnp.min(jnp.min(c_ref[...], axis=0))

    block(qi, True)

    def cond(st):
        return (st[0] < qi) & (st[1] < SB_UNDERFLOW)

    def body(st):
        block(qi - 1 - st[0], False)
        return st[0] + 1, min_carry()

    lax.while_loop(cond, body, (jnp.int32(0), min_carry()))
    for p in range(HEAD_PAIRS):
        a = acc_ref[p]
        o_ref[0, :, p * LANES:(p + 1) * LANES] = jnp.where(first, a[:blk], a[blk:]).astype(o_ref.dtype)


def _sb_attention(q, k, v):
    b, s, wdt = q.shape
    blk = ATTN_BLOCK
    qspec = pl.BlockSpec((1, blk, wdt), lambda bi, qi: (bi, qi, 0))
    kspec = pl.BlockSpec((1, s, wdt), lambda bi, qi: (bi, 0, 0))
    state = pltpu.VMEM((HEAD_PAIRS, 2 * blk, LANES), F32)
    return pl.pallas_call(
        functools.partial(_sb_kernel, blk=blk),
        grid=(b, s // blk),
        in_specs=[qspec, kspec, kspec],
        out_specs=qspec,
        out_shape=jax.ShapeDtypeStruct(q.shape, BF16),
        scratch_shapes=[state, state],
        compiler_params=_params("parallel", "arbitrary"),
        name="sb_attention",
    )(q, k, v)


def _mla_kernel(qn_ref, qr_ref, kn_ref, kr_ref, v_ref, o_ref, acc_ref, m_ref, *, blk):
    qi = pl.program_id(1)
    lane2 = lax.broadcasted_iota(jnp.int32, (1, 2 * LANES), 1)
    sel0 = (lane2 < MLA_NOPE) | ((lane2 >= LANES) & (lane2 < LANES + MLA_ROPE))
    sel1 = ((lane2 >= MLA_NOPE) & (lane2 < LANES)) | (
        (lane2 >= LANES + MLA_ROPE) & (lane2 < LANES + 2 * MLA_ROPE))
    q_st = []
    for p in range(HEAD_PAIRS):
        sl = slice(p * LANES, (p + 1) * LANES)
        qcat = jnp.concatenate([qn_ref[0, :, sl], qr_ref[0, :, sl]], axis=1)
        zero = jnp.zeros_like(qcat)
        q_st.append(jnp.concatenate([jnp.where(sel0, qcat, zero), jnp.where(sel1, qcat, zero)], axis=0))
    r2 = lax.broadcasted_iota(jnp.int32, (2 * blk, blk), 0)
    c2 = lax.broadcasted_iota(jnp.int32, (2 * blk, blk), 1)
    visible = (c2 // CHUNK) <= (jnp.where(r2 >= blk, r2 - blk, r2) // CHUNK)
    lane = lax.broadcasted_iota(jnp.int32, (1, LANES), 1)
    first = lane < MLA_V

    acc_ref[...] = jnp.zeros_like(acc_ref)
    m_ref[...] = jnp.full(m_ref.shape, -jnp.inf, F32)

    def blocks(kbs, diag):
        starts = [pl.multiple_of(kb * blk, blk) for kb in kbs]
        sc = []
        for start in starts:
            kr_blk = kr_ref[0, pl.ds(start, blk), :]
            sc.append([_dot_t(q_st[p], jnp.concatenate(
                [kn_ref[0, pl.ds(start, blk), p * LANES:(p + 1) * LANES], kr_blk], axis=1))
                for p in range(HEAD_PAIRS)])
        pr, alpha = [], []
        for sc_b in sc:
            pr_b, alpha_b = [], []
            for p in range(HEAD_PAIRS):
                s_p = jnp.where(visible, sc_b[p], -jnp.inf) if diag else sc_b[p]
                m_prev = m_ref[p]
                m_new = jnp.maximum(m_prev, jnp.max(s_p, axis=1, keepdims=True))
                alpha_b.append(jnp.exp2(m_prev - m_new))
                pr_b.append(jnp.exp2(s_p - jnp.concatenate([m_new] * (blk // LANES), axis=1)).astype(BF16))
                m_ref[p] = m_new
            pr.append(pr_b)
            alpha.append(alpha_b)
        for h in range(MLA_HEADS):
            p, half = divmod(h, 2)
            rows = slice(half * blk, (half + 1) * blk)
            a = acc_ref[h]
            for start, pr_b, alpha_b in zip(starts, pr, alpha):
                a = alpha_b[p][rows] * a + _dot(pr_b[p][rows], v_ref[0, pl.ds(start, blk), h * LANES:(h + 1) * LANES])
            acc_ref[h] = a

    blocks([qi], True)

    def body(j, _):
        blocks([2 * j, 2 * j + 1], False)
        return 0

    lax.fori_loop(0, qi // 2, body, 0)

    @pl.when(qi % 2 == 1)
    def _():
        blocks([qi - 1], False)

    for p in range(HEAD_PAIRS):
        a0 = acc_ref[2 * p]
        a1 = acc_ref[2 * p + 1]
        o0 = a0 / pltpu.roll(a0, MLA_V, axis=1)
        o1 = pltpu.roll(a1 / pltpu.roll(a1, MLA_V, axis=1), MLA_V, axis=1)
        o_ref[0, :, p * LANES:(p + 1) * LANES] = jnp.where(first, o0, o1).astype(o_ref.dtype)


def _mla_attention(qn, qr, kn, kr, vcat):
    b, s, wdt = qn.shape
    blk = ATTN_BLOCK
    qspec = pl.BlockSpec((1, blk, wdt), lambda bi, qi: (bi, qi, 0))
    full = lambda a: pl.BlockSpec((1, s, a.shape[-1]), lambda bi, qi: (bi, 0, 0))
    return pl.pallas_call(
        functools.partial(_mla_kernel, blk=blk),
        grid=(b, s // blk),
        in_specs=[qspec, qspec, full(kn), full(kr), full(vcat)],
        out_specs=qspec,
        out_shape=jax.ShapeDtypeStruct(qn.shape, BF16),
        scratch_shapes=[pltpu.VMEM((MLA_HEADS, blk, LANES), F32),
                        pltpu.VMEM((HEAD_PAIRS, 2 * blk, LANES), F32)],
        compiler_params=_params("parallel", "arbitrary"),
        name="mla_attention",
    )(qn, qr, kn, kr, vcat)


def _route(logits):
    lane = lax.broadcasted_iota(jnp.int32, logits.shape, 1).astype(F32)
    ninf = -jnp.inf
    big = float(ROUTER_LANES)
    is_g = lane < N_GROUPS
    lg = jnp.where(is_g, logits, ninf)
    gmax = jnp.max(lg, axis=1, keepdims=True)
    gsum = jnp.sum(jnp.where(is_g, jnp.exp(lg - gmax), 0.0), axis=1, keepdims=True)
    gp = 1.0 / gsum
    g = jnp.min(jnp.where(lg == gmax, lane, big), axis=1, keepdims=True)
    lo = N_GROUPS + EXPERTS_PER_GROUP * g
    in_grp = (lane >= lo) & (lane < lo + EXPERTS_PER_GROUP)
    le = jnp.where(in_grp, logits, ninf)
    l1 = jnp.max(le, axis=1, keepdims=True)
    i1 = jnp.min(jnp.where(le == l1, lane, big), axis=1, keepdims=True)
    le2 = jnp.where(lane == i1, ninf, le)
    l2 = jnp.max(le2, axis=1, keepdims=True)
    i2 = jnp.min(jnp.where(le2 == l2, lane, big), axis=1, keepdims=True)
    t = jnp.exp(l2 - l1)
    w1 = gp / (1.0 + t)
    w2 = gp * t / (1.0 + t)
    return i1 - N_GROUPS, i2 - N_GROUPS, w1, w2


def _pack_bf16_pairs(x):
    n = x.shape[1] // 2
    xb = x.astype(BF16).astype(F32)
    hi = pltpu.bitcast(xb[:, :n], jnp.uint32)
    lo = pltpu.bitcast(xb[:, n:], jnp.uint32)
    return hi | (lo >> 16)


def _unpack_bf16_pairs(w):
    hi = pltpu.bitcast(w & jnp.uint32(0xFFFF0000), F32)
    lo = pltpu.bitcast(w << 16, F32)
    return hi, lo


def _store_slabs(ref, rows):
    for j in range(PACK_SUBLANES):
        ref[:, j, :] = rows[:, j * LANES:(j + 1) * LANES]


def _load_slabs(ref):
    return jnp.concatenate([ref[:, j, :] for j in range(PACK_SUBLANES)], axis=1)


R_E1, R_E2, R_W1, R_W2, R_RANK1, R_RANK2 = range(6)


def _mixer_out_kernel(h_ref, osb_ref, omla_ref, gosb_ref, gomla_ref, wout_ref, gmoe_ref,
                      wrh_ref, wrl_ref, br_ref, h1_ref, xp_ref, info_ref, cnt_ref):
    @pl.when(pl.program_id(0) == 0)
    def _():
        cnt_ref[...] = jnp.zeros_like(cnt_ref)

    nsb = _rms(osb_ref[...].astype(F32), gosb_ref[...]).astype(BF16)
    nmla = _rms(omla_ref[...].astype(F32), gomla_ref[...]).astype(BF16)
    h1 = h_ref[...] + _dot(nsb, wout_ref[0:SB_WIDTH, :]) + _dot(nmla, wout_ref[SB_WIDTH:, :])
    h1_ref[...] = h1
    xn = _rms(h1, gmoe_ref[...])
    _store_slabs(xp_ref, _pack_bf16_pairs(xn))
    x_hi = xn.astype(BF16)
    x_lo = (xn - x_hi.astype(F32)).astype(BF16)
    logits = (_dot(x_hi, wrh_ref[...]) + _dot(x_hi, wrl_ref[...]) + _dot(x_lo, wrh_ref[...])) + br_ref[...]
    e1, e2, w1, w2 = _route(logits)

    tm = logits.shape[0]
    lane = lax.broadcasted_iota(jnp.int32, logits.shape, 1).astype(F32)
    onehot = jnp.where((lane == e1) | (lane == e2), 1.0, 0.0)
    r = lax.broadcasted_iota(jnp.int32, (tm, tm), 0)
    c = lax.broadcasted_iota(jnp.int32, (tm, tm), 1)
    before = jnp.where(c < r, 1.0, 0.0).astype(BF16)
    seen = _dot(before, onehot.astype(BF16)) + cnt_ref[...]
    rank1 = jnp.sum(jnp.where(lane == e1, seen, 0.0), axis=1, keepdims=True)
    rank2 = jnp.sum(jnp.where(lane == e2, seen, 0.0), axis=1, keepdims=True)
    cnt_ref[...] += jnp.sum(onehot, axis=0, keepdims=True)

    info = jnp.zeros_like(logits)
    for idx, val in ((R_E1, e1), (R_E2, e2), (R_W1, w1), (R_W2, w2), (R_RANK1, rank1), (R_RANK2, rank2)):
        info = jnp.where(lane == idx, val, info)
    info_ref[...] = info


def _mixer_out(h, osb, omla, gosb, gomla, wout, gmoe, wrh, wrl, br):
    t = h.shape[0]
    tm = TOKEN_TILE
    row = lambda w: pl.BlockSpec((tm, w), lambda i: (i, 0))
    full = lambda a: pl.BlockSpec(a.shape, lambda i: (0,) * a.ndim)
    return pl.pallas_call(
        _mixer_out_kernel,
        grid=(t // tm,),
        in_specs=[row(D_MODEL), row(SB_WIDTH), row(MLA_WIDTH), full(gosb), full(gomla), full(wout),
                  full(gmoe), full(wrh), full(wrl), full(br)],
        out_specs=[row(D_MODEL), pl.BlockSpec((tm, PACK_SUBLANES, LANES), lambda i: (i, 0, 0)), row(ROUTER_LANES),
                   pl.BlockSpec((1, ROUTER_LANES), lambda i: (0, 0))],
        out_shape=[jax.ShapeDtypeStruct((t, D_MODEL), F32),
                   jax.ShapeDtypeStruct((t, PACK_SUBLANES, LANES), jnp.uint32),
                   jax.ShapeDtypeStruct((t, ROUTER_LANES), F32), jax.ShapeDtypeStruct((1, ROUTER_LANES), F32)],
        compiler_params=_params("arbitrary"),
        name="mixer_out",
    )(h, osb, omla, gosb, gomla, wout, gmoe, wrh, wrl, br)


def _route_plan(info, counts, n_rows):
    te = EXPERT_TILE
    cnt = counts[0, :N_EXPERTS].astype(jnp.int32)
    padded = (cnt + te - 1) // te * te
    seg_end = jnp.cumsum(padded)
    seg_start = seg_end - padded
    e = info[:, (R_E1, R_E2)].astype(jnp.int32)
    rank = info[:, (R_RANK1, R_RANK2)].astype(jnp.int32)
    pos = jnp.take(seg_start, e) + rank
    tile_start = jnp.arange(n_rows // te, dtype=jnp.int32) * te
    tile_expert = jnp.minimum(jnp.sum(tile_start[:, None] >= seg_end[None, :], axis=1), N_EXPERTS - 1)
    n_valid = (seg_end[-1] // te).reshape(1)
    return pos, tile_expert.astype(jnp.int32), n_valid.astype(jnp.int32)


def _dispatch_kernel(pos_ref, x_ref, init_hbm, xs_hbm, sem):
    del init_hbm
    tm = x_ref.shape[0]

    def issue(t, _):
        for k in range(2):
            pltpu.make_async_copy(x_ref.at[t], xs_hbm.at[pos_ref[0, 0, 2 * t + k]],
                                  sem).start(priority=k)
        return 0

    lax.fori_loop(0, tm, issue, 0, unroll=8)
    for _ in range(2):
        pltpu.make_async_copy(x_ref, xs_hbm.at[pl.ds(0, tm)], sem).wait()


def _dispatch(pos, xp, n_rows):
    t = xp.shape[0]
    tm = TOKEN_TILE
    init = jnp.zeros((n_rows,) + xp.shape[1:], xp.dtype)
    return pl.pallas_call(
        _dispatch_kernel,
        grid=(t // tm,),
        in_specs=[pl.BlockSpec((1, 1, 2 * tm), lambda i: (i, 0, 0), memory_space=pltpu.SMEM),
                  pl.BlockSpec((tm,) + xp.shape[1:], lambda i: (i, 0, 0)),
                  pl.BlockSpec(memory_space=pl.ANY)],
        out_specs=pl.BlockSpec(memory_space=pl.ANY),
        out_shape=jax.ShapeDtypeStruct(init.shape, xp.dtype),
        scratch_shapes=[pltpu.SemaphoreType.DMA(())],
        input_output_aliases={2: 0},
        compiler_params=_params("arbitrary"),
        name="moe_dispatch",
    )(pos.reshape(t // tm, 1, 2 * tm), xp, init)


def _expert_kernel(te_ref, nv_ref, xs_ref, wg_ref, wu_ref, wd_ref, ys_ref, wg16, wu16, wd16):
    i = pl.program_id(0)
    live = i < nv_ref[0]

    @pl.when(live & ((i == 0) | (te_ref[i] != te_ref[jnp.maximum(i - 1, 0)])))
    def _():
        wg16[...] = wg_ref[0].astype(BF16)
        wu16[...] = wu_ref[0].astype(BF16)
        wd16[...] = wd_ref[0].astype(BF16)

    @pl.when(live)
    def _():
        half = D_MODEL // 2
        xa, xb = _unpack_bf16_pairs(_load_slabs(xs_ref))
        xa = xa.astype(BF16)
        xb = xb.astype(BF16)
        a = _dot(xa, wg16[:half, :]) + _dot(xb, wg16[half:, :])
        u = _dot(xa, wu16[:half, :]) + _dot(xb, wu16[half:, :])
        hid = (a / (1.0 + jnp.exp(-a))) * u
        _store_slabs(ys_ref, _pack_bf16_pairs(_dot(hid.astype(BF16), wd16[...])))

    @pl.when(i >= nv_ref[0])
    def _():
        ys_ref[...] = jnp.zeros_like(ys_ref)


def _experts(tile_expert, n_valid, xs, wg, wu, wd, layer):
    n_rows = xs.shape[0]
    te = EXPERT_TILE
    base = layer * N_EXPERTS
    last = lambda i, te_ref, nv_ref: jnp.minimum(i, nv_ref[0] - 1)
    slab = (te,) + xs.shape[1:]
    wspec = lambda shape: pl.BlockSpec(
        (1,) + shape, lambda i, te_ref, nv_ref: (base + te_ref[last(i, te_ref, nv_ref)], 0, 0))
    return pl.pallas_call(
        _expert_kernel,
        grid_spec=pltpu.PrefetchScalarGridSpec(
            num_scalar_prefetch=2,
            grid=(n_rows // te,),
            in_specs=[pl.BlockSpec(slab, lambda i, te_ref, nv_ref: (last(i, te_ref, nv_ref), 0, 0)),
                      wspec((D_MODEL, D_EXPERT)), wspec((D_MODEL, D_EXPERT)), wspec((D_EXPERT, D_MODEL))],
            out_specs=pl.BlockSpec(slab, lambda i, te_ref, nv_ref: (i, 0, 0)),
            scratch_shapes=[pltpu.VMEM((D_MODEL, D_EXPERT), BF16), pltpu.VMEM((D_MODEL, D_EXPERT), BF16),
                            pltpu.VMEM((D_EXPERT, D_MODEL), BF16)],
        ),
        out_shape=jax.ShapeDtypeStruct(xs.shape, xs.dtype),
        compiler_params=_params("arbitrary"),
        name="moe_experts",
    )(tile_expert, n_valid, xs, wg, wu, wd)


def _combine_ple_kernel(pos_ref, pos_next_ref, info_ref, h_ref, p_ref, g_ref, wpg_ref, wpe_ref, gf_ref, ys_hbm,
                        o_ref, buf, sem, *, final):
    i = pl.program_id(0)
    tm = h_ref.shape[0]
    slot = i % 2

    def gather(rows_ref, dst):
        def issue(t, _):
            for k in range(2):
                pltpu.make_async_copy(ys_hbm.at[rows_ref[0, 0, 2 * t + k]],
                                      buf.at[dst, k * tm + t], sem.at[dst]).start(priority=k)
            return 0

        lax.fori_loop(0, tm, issue, 0, unroll=8)

    @pl.when(i == 0)
    def _():
        gather(pos_ref, 0)

    @pl.when(i + 1 < pl.num_programs(0))
    def _():
        gather(pos_next_ref, 1 - slot)

    info = info_ref[...]
    lane = lax.broadcasted_iota(jnp.int32, info.shape, 1)
    gate1 = jnp.sum(jnp.where(lane == R_W1, info, 0.0), axis=1, keepdims=True)
    gate2 = jnp.sum(jnp.where(lane == R_W2, info, 0.0), axis=1, keepdims=True)
    pltpu.make_async_copy(ys_hbm.at[pl.ds(0, 2 * tm)], buf.at[slot], sem.at[slot]).wait()
    y1a, y1b = _unpack_bf16_pairs(_load_slabs(buf.at[slot, pl.ds(0, tm)]))
    y2a, y2b = _unpack_bf16_pairs(_load_slabs(buf.at[slot, pl.ds(tm, tm)]))
    y = jnp.concatenate([gate1 * y1a + gate2 * y2a, gate1 * y1b + gate2 * y2b], axis=1)
    h = h_ref[...] + y
    xn = _rms(h, g_ref[...]).astype(BF16)
    gate = 1.0 / (1.0 + jnp.exp(-_dot(xn, wpg_ref[...])))
    out = h + gate * _dot(p_ref[0].astype(BF16), wpe_ref[...])
    if final:
        out = _rms(out, gf_ref[...])
    o_ref[...] = out


def _combine_ple(pos, info, h, p, layer, g, wpg, wpe, gf, ys, final):
    t = h.shape[0]
    tm = TOKEN_TILE
    n = t // tm
    row = lambda w: pl.BlockSpec((tm, w), lambda i: (i, 0))
    full = lambda a: pl.BlockSpec(a.shape, lambda i: (0,) * a.ndim)
    pos3 = pos.reshape(n, 1, 2 * tm)
    return pl.pallas_call(
        functools.partial(_combine_ple_kernel, final=final),
        grid=(n,),
        in_specs=[pl.BlockSpec((1, 1, 2 * tm), lambda i: (i, 0, 0), memory_space=pltpu.SMEM),
                  pl.BlockSpec((1, 1, 2 * tm), lambda i: (jnp.minimum(i + 1, n - 1), 0, 0), memory_space=pltpu.SMEM),
                  row(ROUTER_LANES), row(D_MODEL), pl.BlockSpec((1, tm, D_PLE), lambda i: (layer, i, 0)),
                  full(g), full(wpg), full(wpe), full(gf),
                  pl.BlockSpec(memory_space=pl.ANY)],
        out_specs=row(D_MODEL),
        out_shape=jax.ShapeDtypeStruct((t, D_MODEL), F32),
        scratch_shapes=[pltpu.VMEM((2, 2 * tm) + ys.shape[1:], ys.dtype), pltpu.SemaphoreType.DMA((2,))],
        compiler_params=_params("arbitrary"),
        name="moe_combine_ple",
    )(pos3, pos3, info, h, p, g, wpg, wpe, gf, ys)


def _rot_cols(w):
    half = w.shape[-1] // 2
    return jnp.concatenate([-w[:, half:], w[:, :half]], axis=1)


def _prep_in(w_in):
    sb_scale = SB_HEAD_DIM ** -0.5 * LOG2_E
    kr = w_in[:, 3 * SB_WIDTH + MLA_Q_RANK + MLA_KV_RANK:]
    pad = jnp.zeros((w_in.shape[0], LANES - 2 * MLA_ROPE), w_in.dtype)
    kra = jnp.concatenate([kr, kr, pad], axis=1)
    krr = _rot_cols(kr)
    krb = jnp.concatenate([krr, krr, pad], axis=1)
    w1 = jnp.concatenate([w_in[:, :SB_WIDTH] * sb_scale,
                          w_in[:, SB_WIDTH:3 * SB_WIDTH + MLA_Q_RANK + MLA_KV_RANK], kra, krb], axis=1)
    return w1.astype(BF16)


def _prep_uq(w_uq):
    scale = MLA_QK ** -0.5 * LOG2_E
    w = w_uq.reshape(MLA_Q_RANK, MLA_HEADS, MLA_QK) * scale
    wqn = w[:, :, :MLA_NOPE].reshape(MLA_Q_RANK, MLA_HEADS * MLA_NOPE)
    rope = w[:, :, MLA_NOPE:]
    half = MLA_ROPE // 2
    rot = jnp.concatenate([-rope[:, :, half:], rope[:, :, :half]], axis=2)
    pad = jnp.zeros((MLA_Q_RANK, HEAD_PAIRS, LANES - 2 * MLA_ROPE), w.dtype)

    def pairs(r):
        return jnp.concatenate([r.reshape(MLA_Q_RANK, HEAD_PAIRS, 2 * MLA_ROPE), pad], axis=2).reshape(
            MLA_Q_RANK, HEAD_PAIRS * LANES)

    return wqn.astype(BF16), pairs(rope).astype(BF16), pairs(rot).astype(BF16)


def _prep_ukv(w_ukv):
    w = w_ukv.reshape(MLA_KV_RANK, MLA_HEADS, MLA_NOPE + MLA_V)
    wk = w[:, :, :MLA_NOPE].reshape(MLA_KV_RANK, MLA_HEADS * MLA_NOPE)
    pad = jnp.zeros((MLA_KV_RANK, MLA_HEADS, LANES - MLA_V), w.dtype)
    wv = jnp.concatenate([w[:, :, MLA_NOPE:], pad], axis=2).reshape(MLA_KV_RANK, MLA_HEADS * LANES)
    return wk.astype(BF16), wv.astype(BF16)


def _prep_router(w_rg, b_rg, w_re, b_re):
    pad = ROUTER_LANES - N_GROUPS - N_EXPERTS
    wr = jnp.concatenate([w_rg, w_re, jnp.zeros((D_MODEL, pad), F32)], axis=1)
    br = jnp.concatenate([b_rg, b_re, jnp.zeros((pad,), F32)])[None, :]
    wr_hi = wr.astype(BF16)
    wr_lo = (wr - wr_hi.astype(F32)).astype(BF16)
    return wr_hi, wr_lo, br


def kernel(x, p, positions, g_mix, w_in, g_cq, w_uq, g_ckv, w_ukv, g_osb, g_omla, w_out, g_moe,
           w_rg, b_rg, w_re, b_re, w_gate, w_up, w_down, g_ple, w_pg, w_pe, g_final):
    b, s, d = x.shape
    t = b * s
    depth = w_in.shape[0]

    inv_freq = ROPE_THETA ** (-jnp.arange(0, MLA_ROPE, 2, dtype=F32) / MLA_ROPE)
    ang = positions.astype(F32)[..., None] * inv_freq
    reps = LANES // (MLA_ROPE // 2)
    cos = jnp.tile(jnp.cos(ang), (1, 1, reps)).reshape(t, LANES)
    sin = jnp.tile(jnp.sin(ang), (1, 1, reps)).reshape(t, LANES)

    h = x.reshape(t, d)
    n_rows = -(-(2 * t + N_EXPERTS * (EXPERT_TILE - 1)) // EXPERT_TILE) * EXPERT_TILE
    p_all = p.reshape(depth, t, D_PLE)
    wg_all = w_gate.reshape(depth * N_EXPERTS, D_MODEL, D_EXPERT)
    wu_all = w_up.reshape(depth * N_EXPERTS, D_MODEL, D_EXPERT)
    wd_all = w_down.reshape(depth * N_EXPERTS, D_EXPERT, D_MODEL)
    r3 = lambda a: a.reshape(b, s, a.shape[-1])
    r2 = lambda a: a.reshape(t, a.shape[-1])
    for i in range(depth):
        w1 = _prep_in(w_in[i])
        wqn, wqra, wqrb = _prep_uq(w_uq[i])
        wkvk, wkvv = _prep_ukv(w_ukv[i])
        qsb, ksb, vsb, qn, qr, kn, kr, vm = _mixer_in(
            h, g_mix[i][None], cos, sin, g_cq[i][None], g_ckv[i][None], w1, wqn, wqra, wqrb, wkvk, wkvv)
        osb = _sb_attention(r3(qsb), r3(ksb), r3(vsb))
        omla = _mla_attention(r3(qn), r3(qr), r3(kn), r3(kr), r3(vm))
        wrh, wrl, br = _prep_router(w_rg[i], b_rg[i], w_re[i], b_re[i])
        h1, xp, info, counts = _mixer_out(h, r2(osb), r2(omla), g_osb[i][None], g_omla[i][None],
                                          w_out[i].astype(BF16), g_moe[i][None], wrh, wrl, br)
        pos, tile_expert, n_valid = _route_plan(info, counts, n_rows)
        xs = _dispatch(pos, xp, n_rows)
        ys = _experts(tile_expert, n_valid, xs, wg_all, wu_all, wd_all, i)
        h = _combine_ple(pos, info, h1, p_all, i, g_ple[i][None], w_pg[i].astype(BF16),
                         w_pe[i].astype(BF16), g_final[None], ys, final=(i == depth - 1))
    return h.reshape(b, s, d)
```

```python
import functools

import jax
import jax.numpy as jnp
from jax import lax
from jax.experimental import pallas as pl
from jax.experimental.pallas import tpu as pltpu

F32 = jnp.float32
BF16 = jnp.bfloat16

D_MODEL = 1024
CHUNK = 64
D_PLE = 256
EPS = 1e-6
SB_HEADS = 8
SB_HEAD_DIM = 64
SB_WIDTH = SB_HEADS * SB_HEAD_DIM
MLA_HEADS = 8
MLA_NOPE = 64
MLA_ROPE = 32
MLA_QK = MLA_NOPE + MLA_ROPE
MLA_V = 64
MLA_Q_RANK = 384
MLA_KV_RANK = 256
MLA_WIDTH = MLA_HEADS * MLA_V
ROPE_THETA = 10000.0
N_GROUPS = 4
EXPERTS_PER_GROUP = 8
N_EXPERTS = N_GROUPS * EXPERTS_PER_GROUP
D_EXPERT = 256

LANES = 128
HEAD_PAIRS = SB_HEADS // 2
ROUTER_LANES = LANES
TOKEN_TILE = 512
EXPERT_TILE = 256
PACK_SUBLANES = D_MODEL // 2 // LANES
ATTN_BLOCK = 256
LOG2_E = 1.4426950408889634
SB_UNDERFLOW = 151.0
VMEM_LIMIT = 48 * 1024 * 1024


def _params(*sem):
    return pltpu.CompilerParams(dimension_semantics=sem, vmem_limit_bytes=VMEM_LIMIT)


def _rms(x, g):
    return x * lax.rsqrt(jnp.mean(x * x, axis=-1, keepdims=True) + EPS) * g


def _dot(a, b):
    return jnp.dot(a, b, preferred_element_type=F32)


def _dot_t(a, b):
    return lax.dot_general(a, b, (((1,), (1,)), ((), ())), preferred_element_type=F32)


def _mixer_in_kernel(x_ref, g_ref, cos_ref, sin_ref, gcq_ref, gckv_ref, w1_ref, wqn_ref,
                     wqra_ref, wqrb_ref, wkvk_ref, wkvv_ref,
                     qsb_ref, ksb_ref, vsb_ref, qn_ref, qr_ref, kn_ref, kr_ref, vm_ref):
    xn = _rms(x_ref[...], g_ref[...]).astype(BF16)
    c0 = 0
    qsb_ref[...] = _dot(xn, w1_ref[:, c0:c0 + SB_WIDTH]).astype(BF16)
    c0 += SB_WIDTH
    ksb_ref[...] = _dot(xn, w1_ref[:, c0:c0 + SB_WIDTH]).astype(BF16)
    c0 += SB_WIDTH
    vsb_ref[...] = _dot(xn, w1_ref[:, c0:c0 + SB_WIDTH]).astype(BF16)
    c0 += SB_WIDTH
    cq = _dot(xn, w1_ref[:, c0:c0 + MLA_Q_RANK])
    c0 += MLA_Q_RANK
    ckv = _dot(xn, w1_ref[:, c0:c0 + MLA_KV_RANK])
    c0 += MLA_KV_RANK
    kra = _dot(xn, w1_ref[:, c0:c0 + LANES])
    c0 += LANES
    krb = _dot(xn, w1_ref[:, c0:c0 + LANES])
    cos = cos_ref[...]
    sin = sin_ref[...]
    kr_ref[...] = (kra * cos + krb * sin).astype(BF16)

    cqn = _rms(cq, gcq_ref[...]).astype(BF16)
    qn_ref[...] = _dot(cqn, wqn_ref[...]).astype(BF16)
    ra = _dot(cqn, wqra_ref[...])
    rb = _dot(cqn, wqrb_ref[...])
    for p in range(HEAD_PAIRS):
        sl = slice(p * LANES, (p + 1) * LANES)
        qr_ref[:, sl] = (ra[:, sl] * cos + rb[:, sl] * sin).astype(BF16)

    ckvn = _rms(ckv, gckv_ref[...]).astype(BF16)
    kn_ref[...] = _dot(ckvn, wkvk_ref[...]).astype(BF16)
    lane = lax.broadcasted_iota(jnp.int32, (1, LANES), 1)
    for hd in range(MLA_HEADS):
        sl = slice(hd * LANES, (hd + 1) * LANES)
        vm_ref[:, sl] = jnp.where(lane < MLA_V, _dot(ckvn, wkvv_ref[:, sl]), 1.0).astype(BF16)


def _mixer_in(h, g, cos, sin, gcq, gckv, w1, wqn, wqra, wqrb, wkvk, wkvv):
    t = h.shape[0]
    tm = TOKEN_TILE
    row = lambda w: pl.BlockSpec((tm, w), lambda i: (i, 0))
    full = lambda a: pl.BlockSpec(a.shape, lambda i: (0,) * a.ndim)
    widths = [SB_WIDTH, SB_WIDTH, SB_WIDTH, MLA_WIDTH, MLA_WIDTH, MLA_WIDTH, LANES, MLA_HEADS * LANES]
    return pl.pallas_call(
        _mixer_in_kernel,
        grid=(t // tm,),
        in_specs=[row(D_MODEL), full(g), row(LANES), row(LANES), full(gcq), full(gckv),
                  full(w1), full(wqn), full(wqra), full(wqrb), full(wkvk), full(wkvv)],
        out_specs=[row(w) for w in widths],
        out_shape=[jax.ShapeDtypeStruct((t, w), BF16) for w in widths],
        compiler_params=_params("parallel"),
        name="mixer_in",
    )(h, g, cos, sin, gcq, gckv, w1, wqn, wqra, wqrb, wkvk, wkvv)


def _sb_kernel(q_ref, k_ref, v_ref, o_ref, acc_ref, c_ref, *, blk):
    qi = pl.program_id(1)
    lane = lax.broadcasted_iota(jnp.int32, (1, LANES), 1)
    first = lane < SB_HEAD_DIM
    q_st = []
    for p in range(HEAD_PAIRS):
        q2 = q_ref[0, :, p * LANES:(p + 1) * LANES]
        zero = jnp.zeros_like(q2)
        q_st.append(jnp.concatenate([jnp.where(first, q2, zero), jnp.where(first, zero, q2)], axis=0))
    r = lax.broadcasted_iota(jnp.int32, (blk, blk), 0)
    c = lax.broadcasted_iota(jnp.int32, (blk, blk), 1)
    tri = jnp.where(r >= c, 1.0, 0.0).astype(BF16)
    tri2 = jnp.concatenate([tri, tri], axis=0)
    r2 = lax.broadcasted_iota(jnp.int32, (2 * blk, blk), 0)
    c2 = lax.broadcasted_iota(jnp.int32, (2 * blk, blk), 1)
    causal = c2 < jnp.where(r2 >= blk, r2 - blk, r2)
    hi_mask = jnp.uint32(0xFFFF0000)
    sign_bit = jnp.uint32(0x80000000)

    acc_ref[...] = jnp.zeros_like(acc_ref)
    c_ref[...] = jnp.zeros_like(c_ref)

    def block(kb, diag):
        start = pl.multiple_of(kb * blk, blk)
        cols = lambda ref, p: ref[0, pl.ds(start, blk), p * LANES:(p + 1) * LANES]
        z = [_dot_t(q_st[p], cols(k_ref, p)) for p in range(HEAD_PAIRS)]
        hi, lo = [], []
        for p in range(HEAD_PAIRS):
            neg_abs = pltpu.bitcast(pltpu.bitcast(z[p], jnp.uint32) | sign_bit, F32)
            sp = jnp.maximum(z[p], 0.0) + jnp.log(1.0 + jnp.exp2(neg_abs)) * LOG2_E
            if diag:
                sp = jnp.where(causal, sp, 0.0)
            top = pltpu.bitcast(pltpu.bitcast(sp, jnp.uint32) & hi_mask, F32)
            hi.append(top.astype(BF16))
            lo.append((sp - top).astype(BF16))
        cl = [_dot(jnp.concatenate([hi[p], lo[p]], axis=1), tri2) for p in range(HEAD_PAIRS)]
        w = []
        for p in range(HEAD_PAIRS):
            wp = jnp.exp2(z[p] - cl[p])
            if diag:
                wp = jnp.where(causal, wp, 0.0)
            w.append(wp.astype(BF16))
        pv = [_dot(w[p], cols(v_ref, p)) for p in range(HEAD_PAIRS)]
        for p in range(HEAD_PAIRS):
            carry = c_ref[p]
            acc_ref[p] += jnp.exp2(-carry) * pv[p]
            c_ref[p] = carry + jnp.broadcast_to(cl[p][:, 0:1], carry.shape)

    def min_carry():
        return jnp.min(jnp.min(c_ref[...], axis=0))

    block(qi, True)

    def cond(st):
        return (st[0] < qi) & (st[1] < SB_UNDERFLOW)

    def body(st):
        block(qi - 1 - st[0], False)
        return st[0] + 1, min_carry()

    lax.while_loop(cond, body, (jnp.int32(0), min_carry()))
    for p in range(HEAD_PAIRS):
        a = acc_ref[p]
        o_ref[0, :, p * LANES:(p + 1) * LANES] = jnp.where(first, a[:blk], a[blk:]).astype(o_ref.dtype)


def _sb_attention(q, k, v):
    b, s, wdt = q.shape
    blk = ATTN_BLOCK
    qspec = pl.BlockSpec((1, blk, wdt), lambda bi, qi: (bi, qi, 0))
    kspec = pl.BlockSpec((1, s, wdt), lambda bi, qi: (bi, 0, 0))
    state = pltpu.VMEM((HEAD_PAIRS, 2 * blk, LANES), F32)
    return pl.pallas_call(
        functools.partial(_sb_kernel, blk=blk),
        grid=(b, s // blk),
        in_specs=[qspec, kspec, kspec],
        out_specs=qspec,
        out_shape=jax.ShapeDtypeStruct(q.shape, BF16),
        scratch_shapes=[state, state],
        compiler_params=_params("parallel", "arbitrary"),
        name="sb_attention",
    )(q, k, v)


def _mla_kernel(qn_ref, qr_ref, kn_ref, kr_ref, v_ref, o_ref, acc_ref, m_ref, *, blk):
    qi = pl.program_id(1)
    lane2 = lax.broadcasted_iota(jnp.int32, (1, 2 * LANES), 1)
    sel0 = (lane2 < MLA_NOPE) | ((lane2 >= LANES) & (lane2 < LANES + MLA_ROPE))
    sel1 = ((lane2 >= MLA_NOPE) & (lane2 < LANES)) | (
        (lane2 >= LANES + MLA_ROPE) & (lane2 < LANES + 2 * MLA_ROPE))
    q_st = []
    for p in range(HEAD_PAIRS):
        sl = slice(p * LANES, (p + 1) * LANES)
        qcat = jnp.concatenate([qn_ref[0, :, sl], qr_ref[0, :, sl]], axis=1)
        zero = jnp.zeros_like(qcat)
        q_st.append(jnp.concatenate([jnp.where(sel0, qcat, zero), jnp.where(sel1, qcat, zero)], axis=0))
    r2 = lax.broadcasted_iota(jnp.int32, (2 * blk, blk), 0)
    c2 = lax.broadcasted_iota(jnp.int32, (2 * blk, blk), 1)
    visible = (c2 // CHUNK) <= (jnp.where(r2 >= blk, r2 - blk, r2) // CHUNK)
    lane = lax.broadcasted_iota(jnp.int32, (1, LANES), 1)
    first = lane < MLA_V

    acc_ref[...] = jnp.zeros_like(acc_ref)
    m_ref[...] = jnp.full(m_ref.shape, -jnp.inf, F32)

    def blocks(kbs, diag):
        starts = [pl.multiple_of(kb * blk, blk) for kb in kbs]
        sc = []
        for start in starts:
            kr_blk = kr_ref[0, pl.ds(start, blk), :]
            sc.append([_dot_t(q_st[p], jnp.concatenate(
                [kn_ref[0, pl.ds(start, blk), p * LANES:(p + 1) * LANES], kr_blk], axis=1))
                for p in range(HEAD_PAIRS)])
        pr, alpha = [], []
        for sc_b in sc:
            pr_b, alpha_b = [], []
            for p in range(HEAD_PAIRS):
                s_p = jnp.where(visible, sc_b[p], -jnp.inf) if diag else sc_b[p]
                m_prev = m_ref[p]
                m_new = jnp.maximum(m_prev, jnp.max(s_p, axis=1, keepdims=True))
                alpha_b.append(jnp.exp2(m_prev - m_new))
                pr_b.append(jnp.exp2(s_p - jnp.concatenate([m_new] * (blk // LANES), axis=1)).astype(BF16))
                m_ref[p] = m_new
            pr.append(pr_b)
            alpha.append(alpha_b)
        for h in range(MLA_HEADS):
            p, half = divmod(h, 2)
            rows = slice(half * blk, (half + 1) * blk)
            a = acc_ref[h]
            for start, pr_b, alpha_b in zip(starts, pr, alpha):
                a = alpha_b[p][rows] * a + _dot(pr_b[p][rows], v_ref[0, pl.ds(start, blk), h * LANES:(h + 1) * LANES])
            acc_ref[h] = a

    blocks([qi], True)

    def body(j, _):
        blocks([2 * j, 2 * j + 1], False)
        return 0

    lax.fori_loop(0, qi // 2, body, 0)

    @pl.when(qi % 2 == 1)
    def _():
        blocks([qi - 1], False)

    for p in range(HEAD_PAIRS):
        a0 = acc_ref[2 * p]
        a1 = acc_ref[2 * p + 1]
        o0 = a0 / pltpu.roll(a0, MLA_V, axis=1)
        o1 = pltpu.roll(a1 / pltpu.roll(a1, MLA_V, axis=1), MLA_V, axis=1)
        o_ref[0, :, p * LANES:(p + 1) * LANES] = jnp.where(first, o0, o1).astype(o_ref.dtype)


def _mla_attention(qn, qr, kn, kr, vcat):
    b, s, wdt = qn.shape
    blk = ATTN_BLOCK
    qspec = pl.BlockSpec((1, blk, wdt), lambda bi, qi: (bi, qi, 0))
    full = lambda a: pl.BlockSpec((1, s, a.shape[-1]), lambda bi, qi: (bi, 0, 0))
    return pl.pallas_call(
        functools.partial(_mla_kernel, blk=blk),
        grid=(b, s // blk),
        in_specs=[qspec, qspec, full(kn), full(kr), full(vcat)],
        out_specs=qspec,
        out_shape=jax.ShapeDtypeStruct(qn.shape, BF16),
        scratch_shapes=[pltpu.VMEM((MLA_HEADS, blk, LANES), F32),
                        pltpu.VMEM((HEAD_PAIRS, 2 * blk, LANES), F32)],
        compiler_params=_params("parallel", "arbitrary"),
        name="mla_attention",
    )(qn, qr, kn, kr, vcat)


def _route(logits):
    lane = lax.broadcasted_iota(jnp.int32, logits.shape, 1).astype(F32)
    ninf = -jnp.inf
    big = float(ROUTER_LANES)
    is_g = lane < N_GROUPS
    lg = jnp.where(is_g, logits, ninf)
    gmax = jnp.max(lg, axis=1, keepdims=True)
    gsum = jnp.sum(jnp.where(is_g, jnp.exp(lg - gmax), 0.0), axis=1, keepdims=True)
    gp = 1.0 / gsum
    g = jnp.min(jnp.where(lg == gmax, lane, big), axis=1, keepdims=True)
    lo = N_GROUPS + EXPERTS_PER_GROUP * g
    in_grp = (lane >= lo) & (lane < lo + EXPERTS_PER_GROUP)
    le = jnp.where(in_grp, logits, ninf)
    l1 = jnp.max(le, axis=1, keepdims=True)
    i1 = jnp.min(jnp.where(le == l1, lane, big), axis=1, keepdims=True)
    le2 = jnp.where(lane == i1, ninf, le)
    l2 = jnp.max(le2, axis=1, keepdims=True)
    i2 = jnp.min(jnp.where(le2 == l2, lane, big), axis=1, keepdims=True)
    t = jnp.exp(l2 - l1)
    w1 = gp / (1.0 + t)
    w2 = gp * t / (1.0 + t)
    return i1 - N_GROUPS, i2 - N_GROUPS, w1, w2


def _pack_bf16_pairs(x):
    n = x.shape[1] // 2
    xb = x.astype(BF16).astype(F32)
    hi = pltpu.bitcast(xb[:, :n], jnp.uint32)
    lo = pltpu.bitcast(xb[:, n:], jnp.uint32)
    return hi | (lo >> 16)


def _unpack_bf16_pairs(w):
    hi = pltpu.bitcast(w & jnp.uint32(0xFFFF0000), F32)
    lo = pltpu.bitcast(w << 16, F32)
    return hi, lo


def _to_slabs(rows):
    return rows.reshape(rows.shape[0], PACK_SUBLANES, LANES)


def _from_slabs(slabs):
    return slabs.reshape(slabs.shape[0], PACK_SUBLANES * LANES)


R_E1, R_E2, R_W1, R_W2, R_RANK1, R_RANK2 = range(6)
META_ROWS = 8


def _mixer_out_kernel(h_ref, osb_ref, omla_ref, gosb_ref, gomla_ref, wout_ref, gmoe_ref,
                      wrh_ref, wrl_ref, br_ref, h1_ref, xp_ref, info_ref, meta_ref, cnt_ref):
    @pl.when(pl.program_id(0) == 0)
    def _():
        cnt_ref[...] = jnp.zeros_like(cnt_ref)

    nsb = _rms(osb_ref[...].astype(F32), gosb_ref[...]).astype(BF16)
    nmla = _rms(omla_ref[...].astype(F32), gomla_ref[...]).astype(BF16)
    h1 = h_ref[...] + _dot(nsb, wout_ref[0:SB_WIDTH, :]) + _dot(nmla, wout_ref[SB_WIDTH:, :])
    h1_ref[...] = h1
    xn = _rms(h1, gmoe_ref[...])
    xp_ref[...] = _to_slabs(_pack_bf16_pairs(xn))
    x_hi = xn.astype(BF16)
    x_lo = (xn - x_hi.astype(F32)).astype(BF16)
    logits = (_dot(x_hi, wrh_ref[...]) + _dot(x_hi, wrl_ref[...]) + _dot(x_lo, wrh_ref[...])) + br_ref[...]
    e1, e2, w1, w2 = _route(logits)

    tm = logits.shape[0]
    lane = lax.broadcasted_iota(jnp.int32, logits.shape, 1).astype(F32)
    onehot = jnp.where((lane == e1) | (lane == e2), 1.0, 0.0)
    r = lax.broadcasted_iota(jnp.int32, (tm, tm), 0)
    c = lax.broadcasted_iota(jnp.int32, (tm, tm), 1)
    before = jnp.where(c < r, 1.0, 0.0).astype(BF16)
    seen = _dot(before, onehot.astype(BF16)) + cnt_ref[...]
    rank1 = jnp.sum(jnp.where(lane == e1, seen, 0.0), axis=1, keepdims=True)
    rank2 = jnp.sum(jnp.where(lane == e2, seen, 0.0), axis=1, keepdims=True)
    cnt_ref[...] += jnp.sum(onehot, axis=0, keepdims=True)

    info = jnp.zeros_like(logits)
    for idx, val in ((R_E1, e1), (R_E2, e2), (R_W1, w1), (R_W2, w2), (R_RANK1, rank1), (R_RANK2, rank2)):
        info = jnp.where(lane == idx, val, info)
    info_ref[...] = info
    meta_ref[...] = info.T[0:META_ROWS, :]


def _mixer_out(h, osb, omla, gosb, gomla, wout, gmoe, wrh, wrl, br):
    t = h.shape[0]
    tm = TOKEN_TILE
    row = lambda w: pl.BlockSpec((tm, w), lambda i: (i, 0))
    full = lambda a: pl.BlockSpec(a.shape, lambda i: (0,) * a.ndim)
    return pl.pallas_call(
        _mixer_out_kernel,
        grid=(t // tm,),
        in_specs=[row(D_MODEL), row(SB_WIDTH), row(MLA_WIDTH), full(gosb), full(gomla), full(wout),
                  full(gmoe), full(wrh), full(wrl), full(br)],
        out_specs=[row(D_MODEL), pl.BlockSpec((tm, PACK_SUBLANES, LANES), lambda i: (i, 0, 0)), row(ROUTER_LANES),
                   pl.BlockSpec((META_ROWS, tm), lambda i: (0, i)), pl.BlockSpec((1, ROUTER_LANES), lambda i: (0, 0))],
        out_shape=[jax.ShapeDtypeStruct((t, D_MODEL), F32),
                   jax.ShapeDtypeStruct((t, PACK_SUBLANES, LANES), jnp.uint32),
                   jax.ShapeDtypeStruct((t, ROUTER_LANES), F32), jax.ShapeDtypeStruct((META_ROWS, t), F32),
                   jax.ShapeDtypeStruct((1, ROUTER_LANES), F32)],
        compiler_params=_params("arbitrary"),
        name="mixer_out",
    )(h, osb, omla, gosb, gomla, wout, gmoe, wrh, wrl, br)


def _route_plan(meta, counts, n_rows):
    te = EXPERT_TILE
    cnt = counts[0, :N_EXPERTS].astype(jnp.int32)
    padded = (cnt + te - 1) // te * te
    seg_end = jnp.cumsum(padded)
    seg_start = seg_end - padded
    e = meta[R_E1:R_E2 + 1].astype(jnp.int32)
    rank = meta[R_RANK1:R_RANK2 + 1].astype(jnp.int32)
    pos = jnp.take(seg_start, e) + rank
    tiles = pos.shape[1] // TOKEN_TILE
    pos = pos.reshape(2, tiles, TOKEN_TILE).transpose(1, 0, 2).reshape(tiles, 1, 2 * TOKEN_TILE)
    tile_start = jnp.arange(n_rows // te, dtype=jnp.int32) * te
    tile_expert = jnp.minimum(jnp.sum(tile_start[:, None] >= seg_end[None, :], axis=1), N_EXPERTS - 1)
    n_valid = (seg_end[-1] // te).reshape(1)
    return pos, tile_expert.astype(jnp.int32), n_valid.astype(jnp.int32)


def _dispatch_kernel(pos_ref, x_ref, init_hbm, xs_hbm, sem):
    del init_hbm
    tm = x_ref.shape[0]

    def issue(t, _):
        for k in range(2):
            pltpu.make_async_copy(x_ref.at[t], xs_hbm.at[pos_ref[0, 0, k * tm + t]],
                                  sem).start(priority=k)
        return 0

    lax.fori_loop(0, tm, issue, 0, unroll=8)
    for _ in range(2):
        pltpu.make_async_copy(x_ref, xs_hbm.at[pl.ds(0, tm)], sem).wait()


def _dispatch(pos, xp, n_rows):
    t = xp.shape[0]
    tm = TOKEN_TILE
    init = jnp.zeros((n_rows,) + xp.shape[1:], xp.dtype)
    return pl.pallas_call(
        _dispatch_kernel,
        grid=(t // tm,),
        in_specs=[pl.BlockSpec((1, 1, 2 * tm), lambda i: (i, 0, 0), memory_space=pltpu.SMEM),
                  pl.BlockSpec((tm,) + xp.shape[1:], lambda i: (i, 0, 0)),
                  pl.BlockSpec(memory_space=pl.ANY)],
        out_specs=pl.BlockSpec(memory_space=pl.ANY),
        out_shape=jax.ShapeDtypeStruct(init.shape, xp.dtype),
        scratch_shapes=[pltpu.SemaphoreType.DMA(())],
        input_output_aliases={2: 0},
        compiler_params=_params("arbitrary"),
        name="moe_dispatch",
    )(pos, xp, init)


def _expert_kernel(te_ref, nv_ref, xs_ref, wg_ref, wu_ref, wd_ref, ys_ref, wg16, wu16, wd16):
    i = pl.program_id(0)
    live = i < nv_ref[0]

    @pl.when(live & ((i == 0) | (te_ref[i] != te_ref[jnp.maximum(i - 1, 0)])))
    def _():
        wg16[...] = wg_ref[0].astype(BF16)
        wu16[...] = wu_ref[0].astype(BF16)
        wd16[...] = wd_ref[0].astype(BF16)

    @pl.when(live)
    def _():
        half = D_MODEL // 2
        xa, xb = _unpack_bf16_pairs(_from_slabs(xs_ref[...]))
        xa = xa.astype(BF16)
        xb = xb.astype(BF16)
        a = _dot(xa, wg16[:half, :]) + _dot(xb, wg16[half:, :])
        u = _dot(xa, wu16[:half, :]) + _dot(xb, wu16[half:, :])
        hid = (a / (1.0 + jnp.exp(-a))) * u
        ys_ref[...] = _to_slabs(_pack_bf16_pairs(_dot(hid.astype(BF16), wd16[...])))

    @pl.when(i >= nv_ref[0])
    def _():
        ys_ref[...] = jnp.zeros_like(ys_ref)


def _experts(tile_expert, n_valid, xs, wg, wu, wd, layer):
    n_rows = xs.shape[0]
    te = EXPERT_TILE
    base = layer * N_EXPERTS
    last = lambda i, te_ref, nv_ref: jnp.minimum(i, nv_ref[0] - 1)
    slab = (te,) + xs.shape[1:]
    wspec = lambda shape: pl.BlockSpec(
        (1,) + shape, lambda i, te_ref, nv_ref: (base + te_ref[last(i, te_ref, nv_ref)], 0, 0))
    return pl.pallas_call(
        _expert_kernel,
        grid_spec=pltpu.PrefetchScalarGridSpec(
            num_scalar_prefetch=2,
            grid=(n_rows // te,),
            in_specs=[pl.BlockSpec(slab, lambda i, te_ref, nv_ref: (last(i, te_ref, nv_ref), 0, 0)),
                      wspec((D_MODEL, D_EXPERT)), wspec((D_MODEL, D_EXPERT)), wspec((D_EXPERT, D_MODEL))],
            out_specs=pl.BlockSpec(slab, lambda i, te_ref, nv_ref: (i, 0, 0)),
            scratch_shapes=[pltpu.VMEM((D_MODEL, D_EXPERT), BF16), pltpu.VMEM((D_MODEL, D_EXPERT), BF16),
                            pltpu.VMEM((D_EXPERT, D_MODEL), BF16)],
        ),
        out_shape=jax.ShapeDtypeStruct(xs.shape, xs.dtype),
        compiler_params=_params("arbitrary"),
        name="moe_experts",
    )(tile_expert, n_valid, xs, wg, wu, wd)


def _combine_ple_kernel(pos_ref, pos_next_ref, info_ref, h_ref, p_ref, g_ref, wpg_ref, wpe_ref, gf_ref, ys_hbm,
                        o_ref, buf, sem, *, final):
    i = pl.program_id(0)
    tm = h_ref.shape[0]
    slot = i % 2

    def gather(rows_ref, dst):
        def issue(t, _):
            for k in range(2):
                pltpu.make_async_copy(ys_hbm.at[rows_ref[0, 0, k * tm + t]],
                                      buf.at[dst, k * tm + t], sem.at[dst]).start(priority=k)
            return 0

        lax.fori_loop(0, tm, issue, 0, unroll=8)

    @pl.when(i == 0)
    def _():
        gather(pos_ref, 0)

    @pl.when(i + 1 < pl.num_programs(0))
    def _():
        gather(pos_next_ref, 1 - slot)

    info = info_ref[...]
    lane = lax.broadcasted_iota(jnp.int32, info.shape, 1)
    gate1 = jnp.sum(jnp.where(lane == R_W1, info, 0.0), axis=1, keepdims=True)
    gate2 = jnp.sum(jnp.where(lane == R_W2, info, 0.0), axis=1, keepdims=True)
    pltpu.make_async_copy(ys_hbm.at[pl.ds(0, 2 * tm)], buf.at[slot], sem.at[slot]).wait()
    y1a, y1b = _unpack_bf16_pairs(_from_slabs(buf[slot, 0:tm]))
    y2a, y2b = _unpack_bf16_pairs(_from_slabs(buf[slot, tm:2 * tm]))
    y = jnp.concatenate([gate1 * y1a + gate2 * y2a, gate1 * y1b + gate2 * y2b], axis=1)
    h = h_ref[...] + y
    xn = _rms(h, g_ref[...]).astype(BF16)
    gate = 1.0 / (1.0 + jnp.exp(-_dot(xn, wpg_ref[...])))
    out = h + gate * _dot(p_ref[0].astype(BF16), wpe_ref[...])
    if final:
        out = _rms(out, gf_ref[...])
    o_ref[...] = out


def _combine_ple(pos, info, h, p, layer, g, wpg, wpe, gf, ys, final):
    t = h.shape[0]
    tm = TOKEN_TILE
    n = t // tm
    row = lambda w: pl.BlockSpec((tm, w), lambda i: (i, 0))
    full = lambda a: pl.BlockSpec(a.shape, lambda i: (0,) * a.ndim)
    return pl.pallas_call(
        functools.partial(_combine_ple_kernel, final=final),
        grid=(n,),
        in_specs=[pl.BlockSpec((1, 1, 2 * tm), lambda i: (i, 0, 0), memory_space=pltpu.SMEM),
                  pl.BlockSpec((1, 1, 2 * tm), lambda i: (jnp.minimum(i + 1, n - 1), 0, 0), memory_space=pltpu.SMEM),
                  row(ROUTER_LANES), row(D_MODEL), pl.BlockSpec((1, tm, D_PLE), lambda i: (layer, i, 0)),
                  full(g), full(wpg), full(wpe), full(gf),
                  pl.BlockSpec(memory_space=pl.ANY)],
        out_specs=row(D_MODEL),
        out_shape=jax.ShapeDtypeStruct((t, D_MODEL), F32),
        scratch_shapes=[pltpu.VMEM((2, 2 * tm) + ys.shape[1:], ys.dtype), pltpu.SemaphoreType.DMA((2,))],
        compiler_params=_params("arbitrary"),
        name="moe_combine_ple",
    )(pos, pos, info, h, p, g, wpg, wpe, gf, ys)


def _rot_cols(w):
    half = w.shape[-1] // 2
    return jnp.concatenate([-w[:, half:], w[:, :half]], axis=1)


def _prep_in(w_in):
    sb_scale = SB_HEAD_DIM ** -0.5 * LOG2_E
    kr = w_in[:, 3 * SB_WIDTH + MLA_Q_RANK + MLA_KV_RANK:]
    pad = jnp.zeros((w_in.shape[0], LANES - 2 * MLA_ROPE), w_in.dtype)
    kra = jnp.concatenate([kr, kr, pad], axis=1)
    krr = _rot_cols(kr)
    krb = jnp.concatenate([krr, krr, pad], axis=1)
    w1 = jnp.concatenate([w_in[:, :SB_WIDTH] * sb_scale,
                          w_in[:, SB_WIDTH:3 * SB_WIDTH + MLA_Q_RANK + MLA_KV_RANK], kra, krb], axis=1)
    return w1.astype(BF16)


def _prep_uq(w_uq):
    scale = MLA_QK ** -0.5 * LOG2_E
    w = w_uq.reshape(MLA_Q_RANK, MLA_HEADS, MLA_QK) * scale
    wqn = w[:, :, :MLA_NOPE].reshape(MLA_Q_RANK, MLA_HEADS * MLA_NOPE)
    rope = w[:, :, MLA_NOPE:]
    half = MLA_ROPE // 2
    rot = jnp.concatenate([-rope[:, :, half:], rope[:, :, :half]], axis=2)
    pad = jnp.zeros((MLA_Q_RANK, HEAD_PAIRS, LANES - 2 * MLA_ROPE), w.dtype)

    def pairs(r):
        return jnp.concatenate([r.reshape(MLA_Q_RANK, HEAD_PAIRS, 2 * MLA_ROPE), pad], axis=2).reshape(
            MLA_Q_RANK, HEAD_PAIRS * LANES)

    return wqn.astype(BF16), pairs(rope).astype(BF16), pairs(rot).astype(BF16)


def _prep_ukv(w_ukv):
    w = w_ukv.reshape(MLA_KV_RANK, MLA_HEADS, MLA_NOPE + MLA_V)
    wk = w[:, :, :MLA_NOPE].reshape(MLA_KV_RANK, MLA_HEADS * MLA_NOPE)
    pad = jnp.zeros((MLA_KV_RANK, MLA_HEADS, LANES - MLA_V), w.dtype)
    wv = jnp.concatenate([w[:, :, MLA_NOPE:], pad], axis=2).reshape(MLA_KV_RANK, MLA_HEADS * LANES)
    return wk.astype(BF16), wv.astype(BF16)


def _prep_router(w_rg, b_rg, w_re, b_re):
    pad = ROUTER_LANES - N_GROUPS - N_EXPERTS
    wr = jnp.concatenate([w_rg, w_re, jnp.zeros((D_MODEL, pad), F32)], axis=1)
    br = jnp.concatenate([b_rg, b_re, jnp.zeros((pad,), F32)])[None, :]
    wr_hi = wr.astype(BF16)
    wr_lo = (wr - wr_hi.astype(F32)).astype(BF16)
    return wr_hi, wr_lo, br


def kernel(x, p, positions, g_mix, w_in, g_cq, w_uq, g_ckv, w_ukv, g_osb, g_omla, w_out, g_moe,
           w_rg, b_rg, w_re, b_re, w_gate, w_up, w_down, g_ple, w_pg, w_pe, g_final):
    b, s, d = x.shape
    t = b * s
    depth = w_in.shape[0]

    inv_freq = ROPE_THETA ** (-jnp.arange(0, MLA_ROPE, 2, dtype=F32) / MLA_ROPE)
    ang = positions.astype(F32)[..., None] * inv_freq
    reps = LANES // (MLA_ROPE // 2)
    cos = jnp.tile(jnp.cos(ang), (1, 1, reps)).reshape(t, LANES)
    sin = jnp.tile(jnp.sin(ang), (1, 1, reps)).reshape(t, LANES)

    h = x.reshape(t, d)
    n_rows = -(-(2 * t + N_EXPERTS * (EXPERT_TILE - 1)) // EXPERT_TILE) * EXPERT_TILE
    p_all = p.reshape(depth, t, D_PLE)
    wg_all = w_gate.reshape(depth * N_EXPERTS, D_MODEL, D_EXPERT)
    wu_all = w_up.reshape(depth * N_EXPERTS, D_MODEL, D_EXPERT)
    wd_all = w_down.reshape(depth * N_EXPERTS, D_EXPERT, D_MODEL)
    r3 = lambda a: a.reshape(b, s, a.shape[-1])
    r2 = lambda a: a.reshape(t, a.shape[-1])
    for i in range(depth):
        w1 = _prep_in(w_in[i])
        wqn, wqra, wqrb = _prep_uq(w_uq[i])
        wkvk, wkvv = _prep_ukv(w_ukv[i])
        qsb, ksb, vsb, qn, qr, kn, kr, vm = _mixer_in(
            h, g_mix[i][None], cos, sin, g_cq[i][None], g_ckv[i][None], w1, wqn, wqra, wqrb, wkvk, wkvv)
        osb = _sb_attention(r3(qsb), r3(ksb), r3(vsb))
        omla = _mla_attention(r3(qn), r3(qr), r3(kn), r3(kr), r3(vm))
        wrh, wrl, br = _prep_router(w_rg[i], b_rg[i], w_re[i], b_re[i])
        h1, xp, info, meta, counts = _mixer_out(h, r2(osb), r2(omla), g_osb[i][None], g_omla[i][None],
                                          w_out[i].astype(BF16), g_moe[i][None], wrh, wrl, br)
        pos, tile_expert, n_valid = _route_plan(meta, counts, n_rows)
        xs = _dispatch(pos, xp, n_rows)
        ys = _experts(tile_expert, n_valid, xs, wg_all, wu_all, wd_all, i)
        h = _combine_ple(pos, info, h1, p_all, i, g_ple[i][None], w_pg[i].astype(BF16),
                         w_pe[i].astype(BF16), g_final[None], ys, final=(i == depth - 1))
    return h.reshape(b, s, d)
```

```python
import functools

import jax
import jax.numpy as jnp
from jax import lax
from jax.experimental import pallas as pl
from jax.experimental.pallas import tpu as pltpu

F32 = jnp.float32
BF16 = jnp.bfloat16

D_MODEL = 1024
CHUNK = 64
D_PLE = 256
EPS = 1e-6
SB_HEADS = 8
SB_HEAD_DIM = 64
SB_WIDTH = SB_HEADS * SB_HEAD_DIM
MLA_HEADS = 8
MLA_NOPE = 64
MLA_ROPE = 32
MLA_QK = MLA_NOPE + MLA_ROPE
MLA_V = 64
MLA_Q_RANK = 384
MLA_KV_RANK = 256
MLA_WIDTH = MLA_HEADS * MLA_V
ROPE_THETA = 10000.0
N_GROUPS = 4
EXPERTS_PER_GROUP = 8
N_EXPERTS = N_GROUPS * EXPERTS_PER_GROUP
D_EXPERT = 256

LANES = 128
HEAD_PAIRS = SB_HEADS // 2
ROUTER_LANES = LANES
TOKEN_TILE = 512
EXPERT_TILE = 256
PACK_SUBLANES = D_MODEL // 2 // LANES
ATTN_BLOCK = 256
LOG2_E = 1.4426950408889634
SB_UNDERFLOW = 151.0
VMEM_LIMIT = 48 * 1024 * 1024


def _params(*sem):
    return pltpu.CompilerParams(dimension_semantics=sem, vmem_limit_bytes=VMEM_LIMIT)


def _rms(x, g):
    return x * lax.rsqrt(jnp.mean(x * x, axis=-1, keepdims=True) + EPS) * g


def _dot(a, b):
    return jnp.dot(a, b, preferred_element_type=F32)


def _dot_t(a, b):
    return lax.dot_general(a, b, (((1,), (1,)), ((), ())), preferred_element_type=F32)


def _mixer_in_kernel(x_ref, g_ref, cos_ref, sin_ref, gcq_ref, gckv_ref, w1_ref, wqn_ref,
                     wqra_ref, wqrb_ref, wkvk_ref, wkvv_ref,
                     qsb_ref, ksb_ref, vsb_ref, qn_ref, qr_ref, kn_ref, kr_ref, vm_ref):
    xn = _rms(x_ref[...], g_ref[...]).astype(BF16)
    c0 = 0
    qsb_ref[...] = _dot(xn, w1_ref[:, c0:c0 + SB_WIDTH]).astype(BF16)
    c0 += SB_WIDTH
    ksb_ref[...] = _dot(xn, w1_ref[:, c0:c0 + SB_WIDTH]).astype(BF16)
    c0 += SB_WIDTH
    vsb_ref[...] = _dot(xn, w1_ref[:, c0:c0 + SB_WIDTH]).astype(BF16)
    c0 += SB_WIDTH
    cq = _dot(xn, w1_ref[:, c0:c0 + MLA_Q_RANK])
    c0 += MLA_Q_RANK
    ckv = _dot(xn, w1_ref[:, c0:c0 + MLA_KV_RANK])
    c0 += MLA_KV_RANK
    kra = _dot(xn, w1_ref[:, c0:c0 + LANES])
    c0 += LANES
    krb = _dot(xn, w1_ref[:, c0:c0 + LANES])
    cos = cos_ref[...]
    sin = sin_ref[...]
    kr_ref[...] = (kra * cos + krb * sin).astype(BF16)

    cqn = _rms(cq, gcq_ref[...]).astype(BF16)
    qn_ref[...] = _dot(cqn, wqn_ref[...]).astype(BF16)
    ra = _dot(cqn, wqra_ref[...])
    rb = _dot(cqn, wqrb_ref[...])
    for p in range(HEAD_PAIRS):
        sl = slice(p * LANES, (p + 1) * LANES)
        qr_ref[:, sl] = (ra[:, sl] * cos + rb[:, sl] * sin).astype(BF16)

    ckvn = _rms(ckv, gckv_ref[...]).astype(BF16)
    kn_ref[...] = _dot(ckvn, wkvk_ref[...]).astype(BF16)
    lane = lax.broadcasted_iota(jnp.int32, (1, LANES), 1)
    for hd in range(MLA_HEADS):
        sl = slice(hd * LANES, (hd + 1) * LANES)
        vm_ref[:, sl] = jnp.where(lane < MLA_V, _dot(ckvn, wkvv_ref[:, sl]), 1.0).astype(BF16)


def _mixer_in(h, g, cos, sin, gcq, gckv, w1, wqn, wqra, wqrb, wkvk, wkvv):
    t = h.shape[0]
    tm = TOKEN_TILE
    row = lambda w: pl.BlockSpec((tm, w), lambda i: (i, 0))
    full = lambda a: pl.BlockSpec(a.shape, lambda i: (0,) * a.ndim)
    widths = [SB_WIDTH, SB_WIDTH, SB_WIDTH, MLA_WIDTH, MLA_WIDTH, MLA_WIDTH, LANES, MLA_HEADS * LANES]
    return pl.pallas_call(
        _mixer_in_kernel,
        grid=(t // tm,),
        in_specs=[row(D_MODEL), full(g), row(LANES), row(LANES), full(gcq), full(gckv),
                  full(w1), full(wqn), full(wqra), full(wqrb), full(wkvk), full(wkvv)],
        out_specs=[row(w) for w in widths],
        out_shape=[jax.ShapeDtypeStruct((t, w), BF16) for w in widths],
        compiler_params=_params("parallel"),
        name="mixer_in",
    )(h, g, cos, sin, gcq, gckv, w1, wqn, wqra, wqrb, wkvk, wkvv)


def _sb_kernel(q_ref, k_ref, v_ref, o_ref, acc_ref, c_ref, *, blk):
    qi = pl.program_id(1)
    lane = lax.broadcasted_iota(jnp.int32, (1, LANES), 1)
    first = lane < SB_HEAD_DIM
    q_st = []
    for p in range(HEAD_PAIRS):
        q2 = q_ref[0, :, p * LANES:(p + 1) * LANES]
        zero = jnp.zeros_like(q2)
        q_st.append(jnp.concatenate([jnp.where(first, q2, zero), jnp.where(first, zero, q2)], axis=0))
    r = lax.broadcasted_iota(jnp.int32, (blk, blk), 0)
    c = lax.broadcasted_iota(jnp.int32, (blk, blk), 1)
    tri = jnp.where(r >= c, 1.0, 0.0).astype(BF16)
    tri2 = jnp.concatenate([tri, tri], axis=0)
    r2 = lax.broadcasted_iota(jnp.int32, (2 * blk, blk), 0)
    c2 = lax.broadcasted_iota(jnp.int32, (2 * blk, blk), 1)
    causal = c2 < jnp.where(r2 >= blk, r2 - blk, r2)
    hi_mask = jnp.uint32(0xFFFF0000)
    sign_bit = jnp.uint32(0x80000000)

    acc_ref[...] = jnp.zeros_like(acc_ref)
    c_ref[...] = jnp.zeros_like(c_ref)

    def block(kb, diag):
        start = pl.multiple_of(kb * blk, blk)
        cols = lambda ref, p: ref[0, pl.ds(start, blk), p * LANES:(p + 1) * LANES]
        z = [_dot_t(q_st[p], cols(k_ref, p)) for p in range(HEAD_PAIRS)]
        hi, lo = [], []
        for p in range(HEAD_PAIRS):
            neg_abs = pltpu.bitcast(pltpu.bitcast(z[p], jnp.uint32) | sign_bit, F32)
            sp = jnp.maximum(z[p], 0.0) + jnp.log(1.0 + jnp.exp2(neg_abs)) * LOG2_E
            if diag:
                sp = jnp.where(causal, sp, 0.0)
            top = pltpu.bitcast(pltpu.bitcast(sp, jnp.uint32) & hi_mask, F32)
            hi.append(top.astype(BF16))
            lo.append((sp - top).astype(BF16))
        cl = [_dot(jnp.concatenate([hi[p], lo[p]], axis=1), tri2) for p in range(HEAD_PAIRS)]
        w = []
        for p in range(HEAD_PAIRS):
            wp = jnp.exp2(z[p] - cl[p])
            if diag:
                wp = jnp.where(causal, wp, 0.0)
            w.append(wp.astype(BF16))
        pv = [_dot(w[p], cols(v_ref, p)) for p in range(HEAD_PAIRS)]
        for p in range(HEAD_PAIRS):
            carry = c_ref[p]
            acc_ref[p] += jnp.exp2(-carry) * pv[p]
            c_ref[p] = carry + jnp.broadcast_to(cl[p][:, 0:1], carry.shape)

    def min_carry():
        return jnp.min(jnp.min(c_ref[...], axis=0))

    block(qi, True)

    def cond(st):
        return (st[0] < qi) & (st[1] < SB_UNDERFLOW)

    def body(st):
        block(qi - 1 - st[0], False)
        return st[0] + 1, min_carry()

    lax.while_loop(cond, body, (jnp.int32(0), min_carry()))
    for p in range(HEAD_PAIRS):
        a = acc_ref[p]
        o_ref[0, :, p * LANES:(p + 1) * LANES] = jnp.where(first, a[:blk], a[blk:]).astype(o_ref.dtype)


def _sb_attention(q, k, v):
    b, s, wdt = q.shape
    blk = ATTN_BLOCK
    qspec = pl.BlockSpec((1, blk, wdt), lambda bi, qi: (bi, qi, 0))
    kspec = pl.BlockSpec((1, s, wdt), lambda bi, qi: (bi, 0, 0))
    state = pltpu.VMEM((HEAD_PAIRS, 2 * blk, LANES), F32)
    return pl.pallas_call(
        functools.partial(_sb_kernel, blk=blk),
        grid=(b, s // blk),
        in_specs=[qspec, kspec, kspec],
        out_specs=qspec,
        out_shape=jax.ShapeDtypeStruct(q.shape, BF16),
        scratch_shapes=[state, state],
        compiler_params=_params("parallel", "arbitrary"),
        name="sb_attention",
    )(q, k, v)


def _mla_kernel(qn_ref, qr_ref, kn_ref, kr_ref, v_ref, o_ref, acc_ref, m_ref, *, blk):
    qi = pl.program_id(1)
    lane2 = lax.broadcasted_iota(jnp.int32, (1, 2 * LANES), 1)
    sel0 = (lane2 < MLA_NOPE) | ((lane2 >= LANES) & (lane2 < LANES + MLA_ROPE))
    sel1 = ((lane2 >= MLA_NOPE) & (lane2 < LANES)) | (
        (lane2 >= LANES + MLA_ROPE) & (lane2 < LANES + 2 * MLA_ROPE))
    q_st = []
    for p in range(HEAD_PAIRS):
        sl = slice(p * LANES, (p + 1) * LANES)
        qcat = jnp.concatenate([qn_ref[0, :, sl], qr_ref[0, :, sl]], axis=1)
        zero = jnp.zeros_like(qcat)
        q_st.append(jnp.concatenate([jnp.where(sel0, qcat, zero), jnp.where(sel1, qcat, zero)], axis=0))
    r2 = lax.broadcasted_iota(jnp.int32, (2 * blk, blk), 0)
    c2 = lax.broadcasted_iota(jnp.int32, (2 * blk, blk), 1)
    visible = (c2 // CHUNK) <= (jnp.where(r2 >= blk, r2 - blk, r2) // CHUNK)
    lane = lax.broadcasted_iota(jnp.int32, (1, LANES), 1)
    first = lane < MLA_V

    acc_ref[...] = jnp.zeros_like(acc_ref)
    m_ref[...] = jnp.full(m_ref.shape, -jnp.inf, F32)

    def blocks(kbs, diag):
        starts = [pl.multiple_of(kb * blk, blk) for kb in kbs]
        sc = []
        for start in starts:
            kr_blk = kr_ref[0, pl.ds(start, blk), :]
            sc.append([_dot_t(q_st[p], jnp.concatenate(
                [kn_ref[0, pl.ds(start, blk), p * LANES:(p + 1) * LANES], kr_blk], axis=1))
                for p in range(HEAD_PAIRS)])
        pr, alpha = [], []
        for sc_b in sc:
            pr_b, alpha_b = [], []
            for p in range(HEAD_PAIRS):
                s_p = jnp.where(visible, sc_b[p], -jnp.inf) if diag else sc_b[p]
                m_prev = m_ref[p]
                m_new = jnp.maximum(m_prev, jnp.max(s_p, axis=1, keepdims=True))
                alpha_b.append(jnp.exp2(m_prev - m_new))
                pr_b.append(jnp.exp2(s_p - jnp.concatenate([m_new] * (blk // LANES), axis=1)).astype(BF16))
                m_ref[p] = m_new
            pr.append(pr_b)
            alpha.append(alpha_b)
        for h in range(MLA_HEADS):
            p, half = divmod(h, 2)
            rows = slice(half * blk, (half + 1) * blk)
            a = acc_ref[h]
            for start, pr_b, alpha_b in zip(starts, pr, alpha):
                a = alpha_b[p][rows] * a + _dot(pr_b[p][rows], v_ref[0, pl.ds(start, blk), h * LANES:(h + 1) * LANES])
            acc_ref[h] = a

    blocks([qi], True)

    def body(j, _):
        blocks([2 * j, 2 * j + 1], False)
        return 0

    lax.fori_loop(0, qi // 2, body, 0)

    @pl.when(qi % 2 == 1)
    def _():
        blocks([qi - 1], False)

    for p in range(HEAD_PAIRS):
        a0 = acc_ref[2 * p]
        a1 = acc_ref[2 * p + 1]
        o0 = a0 / pltpu.roll(a0, MLA_V, axis=1)
        o1 = pltpu.roll(a1 / pltpu.roll(a1, MLA_V, axis=1), MLA_V, axis=1)
        o_ref[0, :, p * LANES:(p + 1) * LANES] = jnp.where(first, o0, o1).astype(o_ref.dtype)


def _mla_attention(qn, qr, kn, kr, vcat):
    b, s, wdt = qn.shape
    blk = ATTN_BLOCK
    qspec = pl.BlockSpec((1, blk, wdt), lambda bi, qi: (bi, qi, 0))
    full = lambda a: pl.BlockSpec((1, s, a.shape[-1]), lambda bi, qi: (bi, 0, 0))
    return pl.pallas_call(
        functools.partial(_mla_kernel, blk=blk),
        grid=(b, s // blk),
        in_specs=[qspec, qspec, full(kn), full(kr), full(vcat)],
        out_specs=qspec,
        out_shape=jax.ShapeDtypeStruct(qn.shape, BF16),
        scratch_shapes=[pltpu.VMEM((MLA_HEADS, blk, LANES), F32),
                        pltpu.VMEM((HEAD_PAIRS, 2 * blk, LANES), F32)],
        compiler_params=_params("parallel", "arbitrary"),
        name="mla_attention",
    )(qn, qr, kn, kr, vcat)


def _route(logits):
    lane = lax.broadcasted_iota(jnp.int32, logits.shape, 1).astype(F32)
    ninf = -jnp.inf
    big = float(ROUTER_LANES)
    is_g = lane < N_GROUPS
    lg = jnp.where(is_g, logits, ninf)
    gmax = jnp.max(lg, axis=1, keepdims=True)
    gsum = jnp.sum(jnp.where(is_g, jnp.exp(lg - gmax), 0.0), axis=1, keepdims=True)
    gp = 1.0 / gsum
    g = jnp.min(jnp.where(lg == gmax, lane, big), axis=1, keepdims=True)
    lo = N_GROUPS + EXPERTS_PER_GROUP * g
    in_grp = (lane >= lo) & (lane < lo + EXPERTS_PER_GROUP)
    le = jnp.where(in_grp, logits, ninf)
    l1 = jnp.max(le, axis=1, keepdims=True)
    i1 = jnp.min(jnp.where(le == l1, lane, big), axis=1, keepdims=True)
    le2 = jnp.where(lane == i1, ninf, le)
    l2 = jnp.max(le2, axis=1, keepdims=True)
    i2 = jnp.min(jnp.where(le2 == l2, lane, big), axis=1, keepdims=True)
    t = jnp.exp(l2 - l1)
    w1 = gp / (1.0 + t)
    w2 = gp * t / (1.0 + t)
    return i1 - N_GROUPS, i2 - N_GROUPS, w1, w2


def _pack_bf16_pairs(x):
    n = x.shape[1] // 2
    xb = x.astype(BF16).astype(F32)
    hi = pltpu.bitcast(xb[:, :n], jnp.uint32)
    lo = pltpu.bitcast(xb[:, n:], jnp.uint32)
    return hi | (lo >> 16)


def _unpack_bf16_pairs(w):
    hi = pltpu.bitcast(w & jnp.uint32(0xFFFF0000), F32)
    lo = pltpu.bitcast(w << 16, F32)
    return hi, lo


def _to_slabs(rows):
    return rows.reshape(rows.shape[0], PACK_SUBLANES, LANES)


def _from_slabs(slabs):
    return slabs.reshape(slabs.shape[0], PACK_SUBLANES * LANES)


R_E1, R_E2, R_W1, R_W2, R_RANK1, R_RANK2 = range(6)
META_ROWS = 8


def _mixer_out_kernel(h_ref, osb_ref, omla_ref, gosb_ref, gomla_ref, wout_ref, gmoe_ref,
                      wrh_ref, wrl_ref, br_ref, h1_ref, xp_ref, info_ref, meta_ref, cnt_ref):
    @pl.when(pl.program_id(0) == 0)
    def _():
        cnt_ref[...] = jnp.zeros_like(cnt_ref)

    nsb = _rms(osb_ref[...].astype(F32), gosb_ref[...]).astype(BF16)
    nmla = _rms(omla_ref[...].astype(F32), gomla_ref[...]).astype(BF16)
    h1 = h_ref[...] + _dot(nsb, wout_ref[0:SB_WIDTH, :]) + _dot(nmla, wout_ref[SB_WIDTH:, :])
    h1_ref[...] = h1
    xn = _rms(h1, gmoe_ref[...])
    xp_ref[...] = _to_slabs(_pack_bf16_pairs(xn))
    x_hi = xn.astype(BF16)
    x_lo = (xn - x_hi.astype(F32)).astype(BF16)
    logits = (_dot(x_hi, wrh_ref[...]) + _dot(x_hi, wrl_ref[...]) + _dot(x_lo, wrh_ref[...])) + br_ref[...]
    e1, e2, w1, w2 = _route(logits)

    tm = logits.shape[0]
    lane = lax.broadcasted_iota(jnp.int32, logits.shape, 1).astype(F32)
    onehot = jnp.where((lane == e1) | (lane == e2), 1.0, 0.0)
    r = lax.broadcasted_iota(jnp.int32, (tm, tm), 0)
    c = lax.broadcasted_iota(jnp.int32, (tm, tm), 1)
    before = jnp.where(c < r, 1.0, 0.0).astype(BF16)
    seen = _dot(before, onehot.astype(BF16)) + cnt_ref[...]
    rank1 = jnp.sum(jnp.where(lane == e1, seen, 0.0), axis=1, keepdims=True)
    rank2 = jnp.sum(jnp.where(lane == e2, seen, 0.0), axis=1, keepdims=True)
    cnt_ref[...] += jnp.sum(onehot, axis=0, keepdims=True)

    info = jnp.zeros_like(logits)
    for idx, val in ((R_E1, e1), (R_E2, e2), (R_W1, w1), (R_W2, w2), (R_RANK1, rank1), (R_RANK2, rank2)):
        info = jnp.where(lane == idx, val, info)
    info_ref[...] = info
    meta_ref[...] = info.T[0:META_ROWS, :]


def _mixer_out(h, osb, omla, gosb, gomla, wout, gmoe, wrh, wrl, br):
    t = h.shape[0]
    tm = TOKEN_TILE
    row = lambda w: pl.BlockSpec((tm, w), lambda i: (i, 0))
    full = lambda a: pl.BlockSpec(a.shape, lambda i: (0,) * a.ndim)
    return pl.pallas_call(
        _mixer_out_kernel,
        grid=(t // tm,),
        in_specs=[row(D_MODEL), row(SB_WIDTH), row(MLA_WIDTH), full(gosb), full(gomla), full(wout),
                  full(gmoe), full(wrh), full(wrl), full(br)],
        out_specs=[row(D_MODEL), pl.BlockSpec((tm, PACK_SUBLANES, LANES), lambda i: (i, 0, 0)), row(ROUTER_LANES),
                   pl.BlockSpec((META_ROWS, tm), lambda i: (0, i)), pl.BlockSpec((1, ROUTER_LANES), lambda i: (0, 0))],
        out_shape=[jax.ShapeDtypeStruct((t, D_MODEL), F32),
                   jax.ShapeDtypeStruct((t, PACK_SUBLANES, LANES), jnp.uint32),
                   jax.ShapeDtypeStruct((t, ROUTER_LANES), F32), jax.ShapeDtypeStruct((META_ROWS, t), F32),
                   jax.ShapeDtypeStruct((1, ROUTER_LANES), F32)],
        compiler_params=_params("arbitrary"),
        name="mixer_out",
    )(h, osb, omla, gosb, gomla, wout, gmoe, wrh, wrl, br)


def _route_plan(meta, counts, n_rows):
    te = EXPERT_TILE
    cnt = counts[0, :N_EXPERTS].astype(jnp.int32)
    padded = (cnt + te - 1) // te * te
    seg_end = jnp.cumsum(padded)
    seg_start = seg_end - padded
    e = meta[R_E1:R_E2 + 1].astype(jnp.int32)
    rank = meta[R_RANK1:R_RANK2 + 1].astype(jnp.int32)
    ids = jnp.arange(N_EXPERTS, dtype=jnp.int32)
    pos = jnp.sum(jnp.where(e[..., None] == ids, seg_start, 0), axis=-1) + rank
    tiles = pos.shape[1] // TOKEN_TILE
    pos = pos.reshape(2, tiles, TOKEN_TILE).transpose(1, 0, 2).reshape(tiles, 1, 2 * TOKEN_TILE)
    tile_start = jnp.arange(n_rows // te, dtype=jnp.int32) * te
    tile_expert = jnp.minimum(jnp.sum(tile_start[:, None] >= seg_end[None, :], axis=1), N_EXPERTS - 1)
    n_valid = (seg_end[-1] // te).reshape(1)
    pad_start = seg_start + cnt
    pad_len = padded - cnt
    return pos, tile_expert.astype(jnp.int32), n_valid.astype(jnp.int32), pad_start, pad_len


def _dispatch_kernel(pad_start_ref, pad_len_ref, pos_ref, x_ref, xs_hbm, zero, sem, zsem):
    tm = x_ref.shape[0]

    @pl.when(pl.program_id(0) == 0)
    def _():
        zero[...] = jnp.zeros_like(zero)

        def per_expert(e, _):
            start = pad_start_ref[e]
            n = pad_len_ref[e]

            def put(j, _):
                pltpu.make_async_copy(zero.at[0], xs_hbm.at[start + j], zsem).start()
                return 0

            def done(j, _):
                pltpu.make_async_copy(zero.at[0], xs_hbm.at[0], zsem).wait()
                return 0

            lax.fori_loop(0, n, put, 0)
            lax.fori_loop(0, n, done, 0)
            return 0

        lax.fori_loop(0, N_EXPERTS, per_expert, 0)

    def issue(t, _):
        for k in range(2):
            pltpu.make_async_copy(x_ref.at[t], xs_hbm.at[pos_ref[0, 0, k * tm + t]],
                                  sem).start(priority=k)
        return 0

    lax.fori_loop(0, tm, issue, 0, unroll=8)
    for _ in range(2):
        pltpu.make_async_copy(x_ref, xs_hbm.at[pl.ds(0, tm)], sem).wait()


def _dispatch(pad_start, pad_len, pos, xp, n_rows):
    t = xp.shape[0]
    tm = TOKEN_TILE
    slab = xp.shape[1:]
    return pl.pallas_call(
        _dispatch_kernel,
        grid_spec=pltpu.PrefetchScalarGridSpec(
            num_scalar_prefetch=2,
            grid=(t // tm,),
            in_specs=[pl.BlockSpec((1, 1, 2 * tm), lambda i, ps, pn: (i, 0, 0), memory_space=pltpu.SMEM),
                      pl.BlockSpec((tm,) + slab, lambda i, ps, pn: (i, 0, 0))],
            out_specs=pl.BlockSpec(memory_space=pl.ANY),
            scratch_shapes=[pltpu.VMEM((1,) + slab, xp.dtype), pltpu.SemaphoreType.DMA(()),
                            pltpu.SemaphoreType.DMA(())],
        ),
        out_shape=jax.ShapeDtypeStruct((n_rows,) + slab, xp.dtype),
        compiler_params=_params("arbitrary"),
        name="moe_dispatch",
    )(pad_start, pad_len, pos, xp)


def _expert_kernel(te_ref, nv_ref, xs_ref, wg_ref, wu_ref, wd_ref, ys_ref, wg16, wu16, wd16):
    i = pl.program_id(0)
    live = i < nv_ref[0]

    @pl.when(live & ((i == 0) | (te_ref[i] != te_ref[jnp.maximum(i - 1, 0)])))
    def _():
        wg16[...] = wg_ref[0].astype(BF16)
        wu16[...] = wu_ref[0].astype(BF16)
        wd16[...] = wd_ref[0].astype(BF16)

    @pl.when(live)
    def _():
        half = D_MODEL // 2
        xa, xb = _unpack_bf16_pairs(_from_slabs(xs_ref[...]))
        xa = xa.astype(BF16)
        xb = xb.astype(BF16)
        a = _dot(xa, wg16[:half, :]) + _dot(xb, wg16[half:, :])
        u = _dot(xa, wu16[:half, :]) + _dot(xb, wu16[half:, :])
        hid = (a / (1.0 + jnp.exp(-a))) * u
        ys_ref[...] = _to_slabs(_pack_bf16_pairs(_dot(hid.astype(BF16), wd16[...])))

    @pl.when(i >= nv_ref[0])
    def _():
        ys_ref[...] = jnp.zeros_like(ys_ref)


def _experts(tile_expert, n_valid, xs, wg, wu, wd, layer):
    n_rows = xs.shape[0]
    te = EXPERT_TILE
    base = layer * N_EXPERTS
    last = lambda i, te_ref, nv_ref: jnp.minimum(i, nv_ref[0] - 1)
    slab = (te,) + xs.shape[1:]
    wspec = lambda shape: pl.BlockSpec(
        (1,) + shape, lambda i, te_ref, nv_ref: (base + te_ref[last(i, te_ref, nv_ref)], 0, 0))
    return pl.pallas_call(
        _expert_kernel,
        grid_spec=pltpu.PrefetchScalarGridSpec(
            num_scalar_prefetch=2,
            grid=(n_rows // te,),
            in_specs=[pl.BlockSpec(slab, lambda i, te_ref, nv_ref: (last(i, te_ref, nv_ref), 0, 0)),
                      wspec((D_MODEL, D_EXPERT)), wspec((D_MODEL, D_EXPERT)), wspec((D_EXPERT, D_MODEL))],
            out_specs=pl.BlockSpec(slab, lambda i, te_ref, nv_ref: (i, 0, 0)),
            scratch_shapes=[pltpu.VMEM((D_MODEL, D_EXPERT), BF16), pltpu.VMEM((D_MODEL, D_EXPERT), BF16),
                            pltpu.VMEM((D_EXPERT, D_MODEL), BF16)],
        ),
        out_shape=jax.ShapeDtypeStruct(xs.shape, xs.dtype),
        compiler_params=_params("arbitrary"),
        name="moe_experts",
    )(tile_expert, n_valid, xs, wg, wu, wd)


def _combine_ple_kernel(pos_ref, pos_next_ref, info_ref, h_ref, p_ref, g_ref, wpg_ref, wpe_ref, gf_ref, ys_hbm,
                        o_ref, buf, sem, *, final):
    i = pl.program_id(0)
    tm = h_ref.shape[0]
    slot = i % 2

    def gather(rows_ref, dst):
        def issue(t, _):
            for k in range(2):
                pltpu.make_async_copy(ys_hbm.at[rows_ref[0, 0, k * tm + t]],
                                      buf.at[dst, k * tm + t], sem.at[dst]).start(priority=k)
            return 0

        lax.fori_loop(0, tm, issue, 0, unroll=8)

    @pl.when(i == 0)
    def _():
        gather(pos_ref, 0)

    @pl.when(i + 1 < pl.num_programs(0))
    def _():
        gather(pos_next_ref, 1 - slot)

    info = info_ref[...]
    lane = lax.broadcasted_iota(jnp.int32, info.shape, 1)
    gate1 = jnp.sum(jnp.where(lane == R_W1, info, 0.0), axis=1, keepdims=True)
    gate2 = jnp.sum(jnp.where(lane == R_W2, info, 0.0), axis=1, keepdims=True)
    pltpu.make_async_copy(ys_hbm.at[pl.ds(0, 2 * tm)], buf.at[slot], sem.at[slot]).wait()
    y1a, y1b = _unpack_bf16_pairs(_from_slabs(buf[slot, 0:tm]))
    y2a, y2b = _unpack_bf16_pairs(_from_slabs(buf[slot, tm:2 * tm]))
    y = jnp.concatenate([gate1 * y1a + gate2 * y2a, gate1 * y1b + gate2 * y2b], axis=1)
    h = h_ref[...] + y
    xn = _rms(h, g_ref[...]).astype(BF16)
    gate = 1.0 / (1.0 + jnp.exp(-_dot(xn, wpg_ref[...])))
    out = h + gate * _dot(p_ref[0].astype(BF16), wpe_ref[...])
    if final:
        out = _rms(out, gf_ref[...])
    o_ref[...] = out


def _combine_ple(pos, info, h, p, layer, g, wpg, wpe, gf, ys, final):
    t = h.shape[0]
    tm = TOKEN_TILE
    n = t // tm
    row = lambda w: pl.BlockSpec((tm, w), lambda i: (i, 0))
    full = lambda a: pl.BlockSpec(a.shape, lambda i: (0,) * a.ndim)
    return pl.pallas_call(
        functools.partial(_combine_ple_kernel, final=final),
        grid=(n,),
        in_specs=[pl.BlockSpec((1, 1, 2 * tm), lambda i: (i, 0, 0), memory_space=pltpu.SMEM),
                  pl.BlockSpec((1, 1, 2 * tm), lambda i: (jnp.minimum(i + 1, n - 1), 0, 0), memory_space=pltpu.SMEM),
                  row(ROUTER_LANES), row(D_MODEL), pl.BlockSpec((1, tm, D_PLE), lambda i: (layer, i, 0)),
                  full(g), full(wpg), full(wpe), full(gf),
                  pl.BlockSpec(memory_space=pl.ANY)],
        out_specs=row(D_MODEL),
        out_shape=jax.ShapeDtypeStruct((t, D_MODEL), F32),
        scratch_shapes=[pltpu.VMEM((2, 2 * tm) + ys.shape[1:], ys.dtype), pltpu.SemaphoreType.DMA((2,))],
        compiler_params=_params("arbitrary"),
        name="moe_combine_ple",
    )(pos, pos, info, h, p, g, wpg, wpe, gf, ys)


def _rot_cols(w):
    half = w.shape[-1] // 2
    return jnp.concatenate([-w[:, half:], w[:, :half]], axis=1)


def _prep_in(w_in):
    sb_scale = SB_HEAD_DIM ** -0.5 * LOG2_E
    kr = w_in[:, 3 * SB_WIDTH + MLA_Q_RANK + MLA_KV_RANK:]
    pad = jnp.zeros((w_in.shape[0], LANES - 2 * MLA_ROPE), w_in.dtype)
    kra = jnp.concatenate([kr, kr, pad], axis=1)
    krr = _rot_cols(kr)
    krb = jnp.concatenate([krr, krr, pad], axis=1)
    w1 = jnp.concatenate([w_in[:, :SB_WIDTH] * sb_scale,
                          w_in[:, SB_WIDTH:3 * SB_WIDTH + MLA_Q_RANK + MLA_KV_RANK], kra, krb], axis=1)
    return w1.astype(BF16)


def _prep_uq(w_uq):
    scale = MLA_QK ** -0.5 * LOG2_E
    w = w_uq.reshape(MLA_Q_RANK, MLA_HEADS, MLA_QK) * scale
    wqn = w[:, :, :MLA_NOPE].reshape(MLA_Q_RANK, MLA_HEADS * MLA_NOPE)
    rope = w[:, :, MLA_NOPE:]
    half = MLA_ROPE // 2
    rot = jnp.concatenate([-rope[:, :, half:], rope[:, :, :half]], axis=2)
    pad = jnp.zeros((MLA_Q_RANK, HEAD_PAIRS, LANES - 2 * MLA_ROPE), w.dtype)

    def pairs(r):
        return jnp.concatenate([r.reshape(MLA_Q_RANK, HEAD_PAIRS, 2 * MLA_ROPE), pad], axis=2).reshape(
            MLA_Q_RANK, HEAD_PAIRS * LANES)

    return wqn.astype(BF16), pairs(rope).astype(BF16), pairs(rot).astype(BF16)


def _prep_ukv(w_ukv):
    w = w_ukv.reshape(MLA_KV_RANK, MLA_HEADS, MLA_NOPE + MLA_V)
    wk = w[:, :, :MLA_NOPE].reshape(MLA_KV_RANK, MLA_HEADS * MLA_NOPE)
    pad = jnp.zeros((MLA_KV_RANK, MLA_HEADS, LANES - MLA_V), w.dtype)
    wv = jnp.concatenate([w[:, :, MLA_NOPE:], pad], axis=2).reshape(MLA_KV_RANK, MLA_HEADS * LANES)
    return wk.astype(BF16), wv.astype(BF16)


def _prep_router(w_rg, b_rg, w_re, b_re):
    pad = ROUTER_LANES - N_GROUPS - N_EXPERTS
    wr = jnp.concatenate([w_rg, w_re, jnp.zeros((D_MODEL, pad), F32)], axis=1)
    br = jnp.concatenate([b_rg, b_re, jnp.zeros((pad,), F32)])[None, :]
    wr_hi = wr.astype(BF16)
    wr_lo = (wr - wr_hi.astype(F32)).astype(BF16)
    return wr_hi, wr_lo, br


def kernel(x, p, positions, g_mix, w_in, g_cq, w_uq, g_ckv, w_ukv, g_osb, g_omla, w_out, g_moe,
           w_rg, b_rg, w_re, b_re, w_gate, w_up, w_down, g_ple, w_pg, w_pe, g_final):
    b, s, d = x.shape
    t = b * s
    depth = w_in.shape[0]

    inv_freq = ROPE_THETA ** (-jnp.arange(0, MLA_ROPE, 2, dtype=F32) / MLA_ROPE)
    ang = positions.astype(F32)[..., None] * inv_freq
    reps = LANES // (MLA_ROPE // 2)
    cos = jnp.tile(jnp.cos(ang), (1, 1, reps)).reshape(t, LANES)
    sin = jnp.tile(jnp.sin(ang), (1, 1, reps)).reshape(t, LANES)

    h = x.reshape(t, d)
    n_rows = -(-(2 * t + N_EXPERTS * (EXPERT_TILE - 1)) // EXPERT_TILE) * EXPERT_TILE
    p_all = p.reshape(depth, t, D_PLE)
    wg_all = w_gate.reshape(depth * N_EXPERTS, D_MODEL, D_EXPERT)
    wu_all = w_up.reshape(depth * N_EXPERTS, D_MODEL, D_EXPERT)
    wd_all = w_down.reshape(depth * N_EXPERTS, D_EXPERT, D_MODEL)
    r3 = lambda a: a.reshape(b, s, a.shape[-1])
    r2 = lambda a: a.reshape(t, a.shape[-1])
    for i in range(depth):
        w1 = _prep_in(w_in[i])
        wqn, wqra, wqrb = _prep_uq(w_uq[i])
        wkvk, wkvv = _prep_ukv(w_ukv[i])
        qsb, ksb, vsb, qn, qr, kn, kr, vm = _mixer_in(
            h, g_mix[i][None], cos, sin, g_cq[i][None], g_ckv[i][None], w1, wqn, wqra, wqrb, wkvk, wkvv)
        osb = _sb_attention(r3(qsb), r3(ksb), r3(vsb))
        omla = _mla_attention(r3(qn), r3(qr), r3(kn), r3(kr), r3(vm))
        wrh, wrl, br = _prep_router(w_rg[i], b_rg[i], w_re[i], b_re[i])
        h1, xp, info, meta, counts = _mixer_out(h, r2(osb), r2(omla), g_osb[i][None], g_omla[i][None],
                                          w_out[i].astype(BF16), g_moe[i][None], wrh, wrl, br)
        pos, tile_expert, n_valid, pad_start, pad_len = _route_plan(meta, counts, n_rows)
        xs = _dispatch(pad_start, pad_len, pos, xp, n_rows)
        ys = _experts(tile_expert, n_valid, xs, wg_all, wu_all, wd_all, i)
        h = _combine_ple(pos, info, h1, p_all, i, g_ple[i][None], w_pg[i].astype(BF16),
                         w_pe[i].astype(BF16), g_final[None], ys, final=(i == depth - 1))
    return h.reshape(b, s, d)
```

```python
import functools

import jax
import jax.numpy as jnp
from jax import lax
from jax.experimental import pallas as pl
from jax.experimental.pallas import tpu as pltpu

F32 = jnp.float32
BF16 = jnp.bfloat16

D_MODEL = 1024
CHUNK = 64
D_PLE = 256
EPS = 1e-6
SB_HEADS = 8
SB_HEAD_DIM = 64
SB_WIDTH = SB_HEADS * SB_HEAD_DIM
MLA_HEADS = 8
MLA_NOPE = 64
MLA_ROPE = 32
MLA_QK = MLA_NOPE + MLA_ROPE
MLA_V = 64
MLA_Q_RANK = 384
MLA_KV_RANK = 256
MLA_WIDTH = MLA_HEADS * MLA_V
ROPE_THETA = 10000.0
N_GROUPS = 4
EXPERTS_PER_GROUP = 8
N_EXPERTS = N_GROUPS * EXPERTS_PER_GROUP
D_EXPERT = 256

LANES = 128
HEAD_PAIRS = SB_HEADS // 2
ROUTER_LANES = LANES
TOKEN_TILE = 512
EXPERT_TILE = 256
PACK_SUBLANES = D_MODEL // 2 // LANES
ZERO_ROWS = 64
ATTN_BLOCK = 256
LOG2_E = 1.4426950408889634
SB_UNDERFLOW = 151.0
VMEM_LIMIT = 48 * 1024 * 1024


def _params(*sem):
    return pltpu.CompilerParams(dimension_semantics=sem, vmem_limit_bytes=VMEM_LIMIT)


def _rms(x, g):
    return x * lax.rsqrt(jnp.mean(x * x, axis=-1, keepdims=True) + EPS) * g


def _dot(a, b):
    return jnp.dot(a, b, preferred_element_type=F32)


def _dot_t(a, b):
    return lax.dot_general(a, b, (((1,), (1,)), ((), ())), preferred_element_type=F32)


def _mixer_in_kernel(x_ref, g_ref, cos_ref, sin_ref, gcq_ref, gckv_ref, w1_ref, wqn_ref,
                     wqra_ref, wqrb_ref, wkvk_ref, wkvv_ref,
                     qsb_ref, ksb_ref, vsb_ref, qn_ref, qr_ref, kn_ref, kr_ref, vm_ref):
    xn = _rms(x_ref[...], g_ref[...]).astype(BF16)
    c0 = 0
    qsb_ref[...] = _dot(xn, w1_ref[:, c0:c0 + SB_WIDTH]).astype(BF16)
    c0 += SB_WIDTH
    ksb_ref[...] = _dot(xn, w1_ref[:, c0:c0 + SB_WIDTH]).astype(BF16)
    c0 += SB_WIDTH
    vsb_ref[...] = _dot(xn, w1_ref[:, c0:c0 + SB_WIDTH]).astype(BF16)
    c0 += SB_WIDTH
    cq = _dot(xn, w1_ref[:, c0:c0 + MLA_Q_RANK])
    c0 += MLA_Q_RANK
    ckv = _dot(xn, w1_ref[:, c0:c0 + MLA_KV_RANK])
    c0 += MLA_KV_RANK
    kra = _dot(xn, w1_ref[:, c0:c0 + LANES])
    c0 += LANES
    krb = _dot(xn, w1_ref[:, c0:c0 + LANES])
    cos = cos_ref[...]
    sin = sin_ref[...]
    kr_ref[...] = (kra * cos + krb * sin).astype(BF16)

    cqn = _rms(cq, gcq_ref[...]).astype(BF16)
    qn_ref[...] = _dot(cqn, wqn_ref[...]).astype(BF16)
    ra = _dot(cqn, wqra_ref[...])
    rb = _dot(cqn, wqrb_ref[...])
    for p in range(HEAD_PAIRS):
        sl = slice(p * LANES, (p + 1) * LANES)
        qr_ref[:, sl] = (ra[:, sl] * cos + rb[:, sl] * sin).astype(BF16)

    ckvn = _rms(ckv, gckv_ref[...]).astype(BF16)
    kn_ref[...] = _dot(ckvn, wkvk_ref[...]).astype(BF16)
    lane = lax.broadcasted_iota(jnp.int32, (1, LANES), 1)
    for hd in range(MLA_HEADS):
        sl = slice(hd * LANES, (hd + 1) * LANES)
        vm_ref[:, sl] = jnp.where(lane < MLA_V, _dot(ckvn, wkvv_ref[:, sl]), 1.0).astype(BF16)


def _mixer_in(h, g, cos, sin, gcq, gckv, w1, wqn, wqra, wqrb, wkvk, wkvv):
    t = h.shape[0]
    tm = TOKEN_TILE
    row = lambda w: pl.BlockSpec((tm, w), lambda i: (i, 0))
    full = lambda a: pl.BlockSpec(a.shape, lambda i: (0,) * a.ndim)
    widths = [SB_WIDTH, SB_WIDTH, SB_WIDTH, MLA_WIDTH, MLA_WIDTH, MLA_WIDTH, LANES, MLA_HEADS * LANES]
    return pl.pallas_call(
        _mixer_in_kernel,
        grid=(t // tm,),
        in_specs=[row(D_MODEL), full(g), row(LANES), row(LANES), full(gcq), full(gckv),
                  full(w1), full(wqn), full(wqra), full(wqrb), full(wkvk), full(wkvv)],
        out_specs=[row(w) for w in widths],
        out_shape=[jax.ShapeDtypeStruct((t, w), BF16) for w in widths],
        compiler_params=_params("parallel"),
        name="mixer_in",
    )(h, g, cos, sin, gcq, gckv, w1, wqn, wqra, wqrb, wkvk, wkvv)


def _sb_kernel(q_ref, k_ref, v_ref, o_ref, acc_ref, c_ref, *, blk):
    qi = pl.program_id(1)
    lane = lax.broadcasted_iota(jnp.int32, (1, LANES), 1)
    first = lane < SB_HEAD_DIM
    q_st = []
    for p in range(HEAD_PAIRS):
        q2 = q_ref[0, :, p * LANES:(p + 1) * LANES]
        zero = jnp.zeros_like(q2)
        q_st.append(jnp.concatenate([jnp.where(first, q2, zero), jnp.where(first, zero, q2)], axis=0))
    r = lax.broadcasted_iota(jnp.int32, (blk, blk), 0)
    c = lax.broadcasted_iota(jnp.int32, (blk, blk), 1)
    tri = jnp.where(r >= c, 1.0, 0.0).astype(BF16)
    tri2 = jnp.concatenate([tri, tri], axis=0)
    r2 = lax.broadcasted_iota(jnp.int32, (2 * blk, blk), 0)
    c2 = lax.broadcasted_iota(jnp.int32, (2 * blk, blk), 1)
    causal = c2 < jnp.where(r2 >= blk, r2 - blk, r2)
    hi_mask = jnp.uint32(0xFFFF0000)
    sign_bit = jnp.uint32(0x80000000)

    acc_ref[...] = jnp.zeros_like(acc_ref)
    c_ref[...] = jnp.zeros_like(c_ref)

    def block(kb, diag):
        start = pl.multiple_of(kb * blk, blk)
        cols = lambda ref, p: ref[0, pl.ds(start, blk), p * LANES:(p + 1) * LANES]
        z = [_dot_t(q_st[p], cols(k_ref, p)) for p in range(HEAD_PAIRS)]
        hi, lo = [], []
        for p in range(HEAD_PAIRS):
            neg_abs = pltpu.bitcast(pltpu.bitcast(z[p], jnp.uint32) | sign_bit, F32)
            sp = jnp.maximum(z[p], 0.0) + jnp.log(1.0 + jnp.exp2(neg_abs)) * LOG2_E
            if diag:
                sp = jnp.where(causal, sp, 0.0)
            top = pltpu.bitcast(pltpu.bitcast(sp, jnp.uint32) & hi_mask, F32)
            hi.append(top.astype(BF16))
            lo.append((sp - top).astype(BF16))
        cl = [_dot(jnp.concatenate([hi[p], lo[p]], axis=1), tri2) for p in range(HEAD_PAIRS)]
        w = []
        for p in range(HEAD_PAIRS):
            wp = jnp.exp2(z[p] - cl[p])
            if diag:
                wp = jnp.where(causal, wp, 0.0)
            w.append(wp.astype(BF16))
        pv = [_dot(w[p], cols(v_ref, p)) for p in range(HEAD_PAIRS)]
        for p in range(HEAD_PAIRS):
            carry = c_ref[p]
            acc_ref[p] += jnp.exp2(-carry) * pv[p]
            c_ref[p] = carry + jnp.broadcast_to(cl[p][:, 0:1], carry.shape)

    def min_carry():
        return jnp.min(jnp.min(c_ref[...], axis=0))

    block(qi, True)

    def cond(st):
        return (st[0] < qi) & (st[1] < SB_UNDERFLOW)

    def body(st):
        block(qi - 1 - st[0], False)
        return st[0] + 1, min_carry()

    lax.while_loop(cond, body, (jnp.int32(0), min_carry()))
    for p in range(HEAD_PAIRS):
        a = acc_ref[p]
        o_ref[0, :, p * LANES:(p + 1) * LANES] = jnp.where(first, a[:blk], a[blk:]).astype(o_ref.dtype)


def _sb_attention(q, k, v):
    b, s, wdt = q.shape
    blk = ATTN_BLOCK
    qspec = pl.BlockSpec((1, blk, wdt), lambda bi, qi: (bi, qi, 0))
    kspec = pl.BlockSpec((1, s, wdt), lambda bi, qi: (bi, 0, 0))
    state = pltpu.VMEM((HEAD_PAIRS, 2 * blk, LANES), F32)
    return pl.pallas_call(
        functools.partial(_sb_kernel, blk=blk),
        grid=(b, s // blk),
        in_specs=[qspec, kspec, kspec],
        out_specs=qspec,
        out_shape=jax.ShapeDtypeStruct(q.shape, BF16),
        scratch_shapes=[state, state],
        compiler_params=_params("parallel", "arbitrary"),
        name="sb_attention",
    )(q, k, v)


def _mla_kernel(qn_ref, qr_ref, kn_ref, kr_ref, v_ref, o_ref, acc_ref, m_ref, *, blk):
    qi = pl.program_id(1)
    lane2 = lax.broadcasted_iota(jnp.int32, (1, 2 * LANES), 1)
    sel0 = (lane2 < MLA_NOPE) | ((lane2 >= LANES) & (lane2 < LANES + MLA_ROPE))
    sel1 = ((lane2 >= MLA_NOPE) & (lane2 < LANES)) | (
        (lane2 >= LANES + MLA_ROPE) & (lane2 < LANES + 2 * MLA_ROPE))
    q_st = []
    for p in range(HEAD_PAIRS):
        sl = slice(p * LANES, (p + 1) * LANES)
        qcat = jnp.concatenate([qn_ref[0, :, sl], qr_ref[0, :, sl]], axis=1)
        zero = jnp.zeros_like(qcat)
        q_st.append(jnp.concatenate([jnp.where(sel0, qcat, zero), jnp.where(sel1, qcat, zero)], axis=0))
    r2 = lax.broadcasted_iota(jnp.int32, (2 * blk, blk), 0)
    c2 = lax.broadcasted_iota(jnp.int32, (2 * blk, blk), 1)
    visible = (c2 // CHUNK) <= (jnp.where(r2 >= blk, r2 - blk, r2) // CHUNK)
    lane = lax.broadcasted_iota(jnp.int32, (1, LANES), 1)
    first = lane < MLA_V

    acc_ref[...] = jnp.zeros_like(acc_ref)
    m_ref[...] = jnp.full(m_ref.shape, -jnp.inf, F32)

    def blocks(kbs, diag):
        starts = [pl.multiple_of(kb * blk, blk) for kb in kbs]
        sc = []
        for start in starts:
            kr_blk = kr_ref[0, pl.ds(start, blk), :]
            sc.append([_dot_t(q_st[p], jnp.concatenate(
                [kn_ref[0, pl.ds(start, blk), p * LANES:(p + 1) * LANES], kr_blk], axis=1))
                for p in range(HEAD_PAIRS)])
        pr, alpha = [], []
        for sc_b in sc:
            pr_b, alpha_b = [], []
            for p in range(HEAD_PAIRS):
                s_p = jnp.where(visible, sc_b[p], -jnp.inf) if diag else sc_b[p]
                m_prev = m_ref[p]
                m_new = jnp.maximum(m_prev, jnp.max(s_p, axis=1, keepdims=True))
                alpha_b.append(jnp.exp2(m_prev - m_new))
                pr_b.append(jnp.exp2(s_p - jnp.concatenate([m_new] * (blk // LANES), axis=1)).astype(BF16))
                m_ref[p] = m_new
            pr.append(pr_b)
            alpha.append(alpha_b)
        for h in range(MLA_HEADS):
            p, half = divmod(h, 2)
            rows = slice(half * blk, (half + 1) * blk)
            a = acc_ref[h]
            for start, pr_b, alpha_b in zip(starts, pr, alpha):
                a = alpha_b[p][rows] * a + _dot(pr_b[p][rows], v_ref[0, pl.ds(start, blk), h * LANES:(h + 1) * LANES])
            acc_ref[h] = a

    blocks([qi], True)

    def body(j, _):
        blocks([2 * j, 2 * j + 1], False)
        return 0

    lax.fori_loop(0, qi // 2, body, 0)

    @pl.when(qi % 2 == 1)
    def _():
        blocks([qi - 1], False)

    for p in range(HEAD_PAIRS):
        a0 = acc_ref[2 * p]
        a1 = acc_ref[2 * p + 1]
        o0 = a0 / pltpu.roll(a0, MLA_V, axis=1)
        o1 = pltpu.roll(a1 / pltpu.roll(a1, MLA_V, axis=1), MLA_V, axis=1)
        o_ref[0, :, p * LANES:(p + 1) * LANES] = jnp.where(first, o0, o1).astype(o_ref.dtype)


def _mla_attention(qn, qr, kn, kr, vcat):
    b, s, wdt = qn.shape
    blk = ATTN_BLOCK
    qspec = pl.BlockSpec((1, blk, wdt), lambda bi, qi: (bi, qi, 0))
    full = lambda a: pl.BlockSpec((1, s, a.shape[-1]), lambda bi, qi: (bi, 0, 0))
    return pl.pallas_call(
        functools.partial(_mla_kernel, blk=blk),
        grid=(b, s // blk),
        in_specs=[qspec, qspec, full(kn), full(kr), full(vcat)],
        out_specs=qspec,
        out_shape=jax.ShapeDtypeStruct(qn.shape, BF16),
        scratch_shapes=[pltpu.VMEM((MLA_HEADS, blk, LANES), F32),
                        pltpu.VMEM((HEAD_PAIRS, 2 * blk, LANES), F32)],
        compiler_params=_params("parallel", "arbitrary"),
        name="mla_attention",
    )(qn, qr, kn, kr, vcat)


def _route(logits):
    lane = lax.broadcasted_iota(jnp.int32, logits.shape, 1).astype(F32)
    ninf = -jnp.inf
    big = float(ROUTER_LANES)
    is_g = lane < N_GROUPS
    lg = jnp.where(is_g, logits, ninf)
    gmax = jnp.max(lg, axis=1, keepdims=True)
    gsum = jnp.sum(jnp.where(is_g, jnp.exp(lg - gmax), 0.0), axis=1, keepdims=True)
    gp = 1.0 / gsum
    g = jnp.min(jnp.where(lg == gmax, lane, big), axis=1, keepdims=True)
    lo = N_GROUPS + EXPERTS_PER_GROUP * g
    in_grp = (lane >= lo) & (lane < lo + EXPERTS_PER_GROUP)
    le = jnp.where(in_grp, logits, ninf)
    l1 = jnp.max(le, axis=1, keepdims=True)
    i1 = jnp.min(jnp.where(le == l1, lane, big), axis=1, keepdims=True)
    le2 = jnp.where(lane == i1, ninf, le)
    l2 = jnp.max(le2, axis=1, keepdims=True)
    i2 = jnp.min(jnp.where(le2 == l2, lane, big), axis=1, keepdims=True)
    t = jnp.exp(l2 - l1)
    w1 = gp / (1.0 + t)
    w2 = gp * t / (1.0 + t)
    return i1 - N_GROUPS, i2 - N_GROUPS, w1, w2


def _pack_bf16_pairs(x):
    n = x.shape[1] // 2
    xb = x.astype(BF16).astype(F32)
    hi = pltpu.bitcast(xb[:, :n], jnp.uint32)
    lo = pltpu.bitcast(xb[:, n:], jnp.uint32)
    return hi | (lo >> 16)


def _unpack_bf16_pairs(w):
    hi = pltpu.bitcast(w & jnp.uint32(0xFFFF0000), F32)
    lo = pltpu.bitcast(w << 16, F32)
    return hi, lo


def _to_slabs(rows):
    return rows.reshape(rows.shape[0], PACK_SUBLANES, LANES)


def _from_slabs(slabs):
    return slabs.reshape(slabs.shape[0], PACK_SUBLANES * LANES)


R_E1, R_E2, R_W1, R_W2, R_RANK1, R_RANK2 = range(6)
META_ROWS = 8


def _mixer_out_kernel(h_ref, osb_ref, omla_ref, gosb_ref, gomla_ref, wout_ref, gmoe_ref,
                      wrh_ref, wrl_ref, br_ref, h1_ref, xp_ref, info_ref, meta_ref, cnt_ref):
    @pl.when(pl.program_id(0) == 0)
    def _():
        cnt_ref[...] = jnp.zeros_like(cnt_ref)

    nsb = _rms(osb_ref[...].astype(F32), gosb_ref[...]).astype(BF16)
    nmla = _rms(omla_ref[...].astype(F32), gomla_ref[...]).astype(BF16)
    h1 = h_ref[...] + _dot(nsb, wout_ref[0:SB_WIDTH, :]) + _dot(nmla, wout_ref[SB_WIDTH:, :])
    h1_ref[...] = h1
    xn = _rms(h1, gmoe_ref[...])
    xp_ref[...] = _to_slabs(_pack_bf16_pairs(xn))
    x_hi = xn.astype(BF16)
    x_lo = (xn - x_hi.astype(F32)).astype(BF16)
    logits = (_dot(x_hi, wrh_ref[...]) + _dot(x_hi, wrl_ref[...]) + _dot(x_lo, wrh_ref[...])) + br_ref[...]
    e1, e2, w1, w2 = _route(logits)

    tm = logits.shape[0]
    lane = lax.broadcasted_iota(jnp.int32, logits.shape, 1).astype(F32)
    onehot = jnp.where((lane == e1) | (lane == e2), 1.0, 0.0)
    r = lax.broadcasted_iota(jnp.int32, (tm, tm), 0)
    c = lax.broadcasted_iota(jnp.int32, (tm, tm), 1)
    before = jnp.where(c < r, 1.0, 0.0).astype(BF16)
    seen = _dot(before, onehot.astype(BF16)) + cnt_ref[...]
    rank1 = jnp.sum(jnp.where(lane == e1, seen, 0.0), axis=1, keepdims=True)
    rank2 = jnp.sum(jnp.where(lane == e2, seen, 0.0), axis=1, keepdims=True)
    cnt_ref[...] += jnp.sum(onehot, axis=0, keepdims=True)

    info = jnp.zeros_like(logits)
    for idx, val in ((R_E1, e1), (R_E2, e2), (R_W1, w1), (R_W2, w2), (R_RANK1, rank1), (R_RANK2, rank2)):
        info = jnp.where(lane == idx, val, info)
    info_ref[...] = info
    meta_ref[...] = info.T[0:META_ROWS, :]


def _mixer_out(h, osb, omla, gosb, gomla, wout, gmoe, wrh, wrl, br):
    t = h.shape[0]
    tm = TOKEN_TILE
    row = lambda w: pl.BlockSpec((tm, w), lambda i: (i, 0))
    full = lambda a: pl.BlockSpec(a.shape, lambda i: (0,) * a.ndim)
    return pl.pallas_call(
        _mixer_out_kernel,
        grid=(t // tm,),
        in_specs=[row(D_MODEL), row(SB_WIDTH), row(MLA_WIDTH), full(gosb), full(gomla), full(wout),
                  full(gmoe), full(wrh), full(wrl), full(br)],
        out_specs=[row(D_MODEL), pl.BlockSpec((tm, PACK_SUBLANES, LANES), lambda i: (i, 0, 0)), row(ROUTER_LANES),
                   pl.BlockSpec((META_ROWS, tm), lambda i: (0, i)), pl.BlockSpec((1, ROUTER_LANES), lambda i: (0, 0))],
        out_shape=[jax.ShapeDtypeStruct((t, D_MODEL), F32),
                   jax.ShapeDtypeStruct((t, PACK_SUBLANES, LANES), jnp.uint32),
                   jax.ShapeDtypeStruct((t, ROUTER_LANES), F32), jax.ShapeDtypeStruct((META_ROWS, t), F32),
                   jax.ShapeDtypeStruct((1, ROUTER_LANES), F32)],
        compiler_params=_params("arbitrary"),
        name="mixer_out",
    )(h, osb, omla, gosb, gomla, wout, gmoe, wrh, wrl, br)


def _route_plan(meta, counts, n_rows):
    te = EXPERT_TILE
    cnt = counts[0, :N_EXPERTS].astype(jnp.int32)
    padded = (cnt + te - 1) // te * te
    seg_end = jnp.cumsum(padded)
    seg_start = seg_end - padded
    e = meta[R_E1:R_E2 + 1].astype(jnp.int32)
    rank = meta[R_RANK1:R_RANK2 + 1].astype(jnp.int32)
    ids = jnp.arange(N_EXPERTS, dtype=jnp.int32)
    pos = jnp.sum(jnp.where(e[..., None] == ids, seg_start, 0), axis=-1) + rank
    tiles = pos.shape[1] // TOKEN_TILE
    pos = pos.reshape(2, tiles, TOKEN_TILE).transpose(1, 0, 2).reshape(tiles, 1, 2 * TOKEN_TILE)
    tile_start = jnp.arange(n_rows // te, dtype=jnp.int32) * te
    tile_expert = jnp.minimum(jnp.sum(tile_start[:, None] >= seg_end[None, :], axis=1), N_EXPERTS - 1)
    n_valid = (seg_end[-1] // te).reshape(1)
    pad_start = jnp.concatenate([seg_start + cnt, seg_end[-1:]])
    pad_len = jnp.concatenate([padded - cnt, n_rows - seg_end[-1:]])
    return pos, tile_expert.astype(jnp.int32), n_valid.astype(jnp.int32), pad_start, pad_len


def _dispatch_kernel(pad_start_ref, pad_len_ref, pos_ref, x_ref, xs_hbm, zero, sem, zsem):
    tm = x_ref.shape[0]

    @pl.when(pl.program_id(0) == 0)
    def _():
        zero[...] = jnp.zeros_like(zero)

        def each_range(copy_fn):
            def per_range(e, _):
                start = pad_start_ref[e]
                n = pad_len_ref[e]
                runs = n // ZERO_ROWS
                lax.fori_loop(0, runs, lambda j, c: copy_fn(
                    pltpu.make_async_copy(zero, xs_hbm.at[pl.ds(start + j * ZERO_ROWS, ZERO_ROWS)], zsem)), 0)
                lax.fori_loop(runs * ZERO_ROWS, n, lambda j, c: copy_fn(
                    pltpu.make_async_copy(zero.at[0], xs_hbm.at[start + j], zsem)), 0)
                return 0

            lax.fori_loop(0, pad_start_ref.shape[0], per_range, 0)

        def start_copy(cp):
            cp.start()
            return 0

        def wait_copy(cp):
            cp.wait()
            return 0

        each_range(start_copy)
        each_range(wait_copy)

    def issue(t, _):
        for k in range(2):
            pltpu.make_async_copy(x_ref.at[t], xs_hbm.at[pos_ref[0, 0, k * tm + t]],
                                  sem).start(priority=k)
        return 0

    lax.fori_loop(0, tm, issue, 0, unroll=8)
    for _ in range(2):
        pltpu.make_async_copy(x_ref, xs_hbm.at[pl.ds(0, tm)], sem).wait()


def _dispatch(pad_start, pad_len, pos, xp, n_rows):
    t = xp.shape[0]
    tm = TOKEN_TILE
    slab = xp.shape[1:]
    return pl.pallas_call(
        _dispatch_kernel,
        grid_spec=pltpu.PrefetchScalarGridSpec(
            num_scalar_prefetch=2,
            grid=(t // tm,),
            in_specs=[pl.BlockSpec((1, 1, 2 * tm), lambda i, ps, pn: (i, 0, 0), memory_space=pltpu.SMEM),
                      pl.BlockSpec((tm,) + slab, lambda i, ps, pn: (i, 0, 0))],
            out_specs=pl.BlockSpec(memory_space=pl.ANY),
            scratch_shapes=[pltpu.VMEM((ZERO_ROWS,) + slab, xp.dtype), pltpu.SemaphoreType.DMA(()),
                            pltpu.SemaphoreType.DMA(())],
        ),
        out_shape=jax.ShapeDtypeStruct((n_rows,) + slab, xp.dtype),
        compiler_params=_params("arbitrary"),
        name="moe_dispatch",
    )(pad_start, pad_len, pos, xp)


def _expert_kernel(te_ref, nv_ref, xs_ref, wg_ref, wu_ref, wd_ref, ys_ref, wg16, wu16, wd16):
    i = pl.program_id(0)
    live = i < nv_ref[0]

    @pl.when(live & ((i == 0) | (te_ref[i] != te_ref[jnp.maximum(i - 1, 0)])))
    def _():
        wg16[...] = wg_ref[0].astype(BF16)
        wu16[...] = wu_ref[0].astype(BF16)
        wd16[...] = wd_ref[0].astype(BF16)

    @pl.when(live)
    def _():
        half = D_MODEL // 2
        xa, xb = _unpack_bf16_pairs(_from_slabs(xs_ref[...]))
        xa = xa.astype(BF16)
        xb = xb.astype(BF16)
        a = _dot(xa, wg16[:half, :]) + _dot(xb, wg16[half:, :])
        u = _dot(xa, wu16[:half, :]) + _dot(xb, wu16[half:, :])
        hid = (a / (1.0 + jnp.exp(-a))) * u
        ys_ref[...] = _to_slabs(_pack_bf16_pairs(_dot(hid.astype(BF16), wd16[...])))

    @pl.when(i >= nv_ref[0])
    def _():
        ys_ref[...] = jnp.zeros_like(ys_ref)


def _experts(tile_expert, n_valid, xs, wg, wu, wd, layer):
    n_rows = xs.shape[0]
    te = EXPERT_TILE
    base = layer * N_EXPERTS
    last = lambda i, te_ref, nv_ref: jnp.minimum(i, nv_ref[0] - 1)
    slab = (te,) + xs.shape[1:]
    wspec = lambda shape: pl.BlockSpec(
        (1,) + shape, lambda i, te_ref, nv_ref: (base + te_ref[last(i, te_ref, nv_ref)], 0, 0))
    return pl.pallas_call(
        _expert_kernel,
        grid_spec=pltpu.PrefetchScalarGridSpec(
            num_scalar_prefetch=2,
            grid=(n_rows // te,),
            in_specs=[pl.BlockSpec(slab, lambda i, te_ref, nv_ref: (last(i, te_ref, nv_ref), 0, 0)),
                      wspec((D_MODEL, D_EXPERT)), wspec((D_MODEL, D_EXPERT)), wspec((D_EXPERT, D_MODEL))],
            out_specs=pl.BlockSpec(slab, lambda i, te_ref, nv_ref: (i, 0, 0)),
            scratch_shapes=[pltpu.VMEM((D_MODEL, D_EXPERT), BF16), pltpu.VMEM((D_MODEL, D_EXPERT), BF16),
                            pltpu.VMEM((D_EXPERT, D_MODEL), BF16)],
        ),
        out_shape=jax.ShapeDtypeStruct(xs.shape, xs.dtype),
        compiler_params=_params("arbitrary"),
        name="moe_experts",
    )(tile_expert, n_valid, xs, wg, wu, wd)


def _combine_ple_kernel(pos_ref, pos_next_ref, info_ref, h_ref, p_ref, g_ref, wpg_ref, wpe_ref, gf_ref, ys_hbm,
                        o_ref, buf, sem, *, final):
    i = pl.program_id(0)
    tm = h_ref.shape[0]
    slot = i % 2

    def gather(rows_ref, dst):
        def issue(t, _):
            for k in range(2):
                pltpu.make_async_copy(ys_hbm.at[rows_ref[0, 0, k * tm + t]],
                                      buf.at[dst, k * tm + t], sem.at[dst]).start(priority=k)
            return 0

        lax.fori_loop(0, tm, issue, 0, unroll=8)

    @pl.when(i == 0)
    def _():
        gather(pos_ref, 0)

    @pl.when(i + 1 < pl.num_programs(0))
    def _():
        gather(pos_next_ref, 1 - slot)

    info = info_ref[...]
    lane = lax.broadcasted_iota(jnp.int32, info.shape, 1)
    gate1 = jnp.sum(jnp.where(lane == R_W1, info, 0.0), axis=1, keepdims=True)
    gate2 = jnp.sum(jnp.where(lane == R_W2, info, 0.0), axis=1, keepdims=True)
    pltpu.make_async_copy(ys_hbm.at[pl.ds(0, 2 * tm)], buf.at[slot], sem.at[slot]).wait()
    y1a, y1b = _unpack_bf16_pairs(_from_slabs(buf[slot, 0:tm]))
    y2a, y2b = _unpack_bf16_pairs(_from_slabs(buf[slot, tm:2 * tm]))
    y = jnp.concatenate([gate1 * y1a + gate2 * y2a, gate1 * y1b + gate2 * y2b], axis=1)
    h = h_ref[...] + y
    xn = _rms(h, g_ref[...]).astype(BF16)
    gate = 1.0 / (1.0 + jnp.exp(-_dot(xn, wpg_ref[...])))
    out = h + gate * _dot(p_ref[0].astype(BF16), wpe_ref[...])
    if final:
        out = _rms(out, gf_ref[...])
    o_ref[...] = out


def _combine_ple(pos, info, h, p, layer, g, wpg, wpe, gf, ys, final):
    t = h.shape[0]
    tm = TOKEN_TILE
    n = t // tm
    row = lambda w: pl.BlockSpec((tm, w), lambda i: (i, 0))
    full = lambda a: pl.BlockSpec(a.shape, lambda i: (0,) * a.ndim)
    return pl.pallas_call(
        functools.partial(_combine_ple_kernel, final=final),
        grid=(n,),
        in_specs=[pl.BlockSpec((1, 1, 2 * tm), lambda i: (i, 0, 0), memory_space=pltpu.SMEM),
                  pl.BlockSpec((1, 1, 2 * tm), lambda i: (jnp.minimum(i + 1, n - 1), 0, 0), memory_space=pltpu.SMEM),
                  row(ROUTER_LANES), row(D_MODEL), pl.BlockSpec((1, tm, D_PLE), lambda i: (layer, i, 0)),
                  full(g), full(wpg), full(wpe), full(gf),
                  pl.BlockSpec(memory_space=pl.ANY)],
        out_specs=row(D_MODEL),
        out_shape=jax.ShapeDtypeStruct((t, D_MODEL), F32),
        scratch_shapes=[pltpu.VMEM((2, 2 * tm) + ys.shape[1:], ys.dtype), pltpu.SemaphoreType.DMA((2,))],
        compiler_params=_params("arbitrary"),
        name="moe_combine_ple",
    )(pos, pos, info, h, p, g, wpg, wpe, gf, ys)


def _rot_cols(w):
    half = w.shape[-1] // 2
    return jnp.concatenate([-w[:, half:], w[:, :half]], axis=1)


def _prep_in(w_in):
    sb_scale = SB_HEAD_DIM ** -0.5 * LOG2_E
    kr = w_in[:, 3 * SB_WIDTH + MLA_Q_RANK + MLA_KV_RANK:]
    pad = jnp.zeros((w_in.shape[0], LANES - 2 * MLA_ROPE), w_in.dtype)
    kra = jnp.concatenate([kr, kr, pad], axis=1)
    krr = _rot_cols(kr)
    krb = jnp.concatenate([krr, krr, pad], axis=1)
    w1 = jnp.concatenate([w_in[:, :SB_WIDTH] * sb_scale,
                          w_in[:, SB_WIDTH:3 * SB_WIDTH + MLA_Q_RANK + MLA_KV_RANK], kra, krb], axis=1)
    return w1.astype(BF16)


def _prep_uq(w_uq):
    scale = MLA_QK ** -0.5 * LOG2_E
    w = w_uq.reshape(MLA_Q_RANK, MLA_HEADS, MLA_QK) * scale
    wqn = w[:, :, :MLA_NOPE].reshape(MLA_Q_RANK, MLA_HEADS * MLA_NOPE)
    rope = w[:, :, MLA_NOPE:]
    half = MLA_ROPE // 2
    rot = jnp.concatenate([-rope[:, :, half:], rope[:, :, :half]], axis=2)
    pad = jnp.zeros((MLA_Q_RANK, HEAD_PAIRS, LANES - 2 * MLA_ROPE), w.dtype)

    def pairs(r):
        return jnp.concatenate([r.reshape(MLA_Q_RANK, HEAD_PAIRS, 2 * MLA_ROPE), pad], axis=2).reshape(
            MLA_Q_RANK, HEAD_PAIRS * LANES)

    return wqn.astype(BF16), pairs(rope).astype(BF16), pairs(rot).astype(BF16)


def _prep_ukv(w_ukv):
    w = w_ukv.reshape(MLA_KV_RANK, MLA_HEADS, MLA_NOPE + MLA_V)
    wk = w[:, :, :MLA_NOPE].reshape(MLA_KV_RANK, MLA_HEADS * MLA_NOPE)
    pad = jnp.zeros((MLA_KV_RANK, MLA_HEADS, LANES - MLA_V), w.dtype)
    wv = jnp.concatenate([w[:, :, MLA_NOPE:], pad], axis=2).reshape(MLA_KV_RANK, MLA_HEADS * LANES)
    return wk.astype(BF16), wv.astype(BF16)


def _prep_router(w_rg, b_rg, w_re, b_re):
    pad = ROUTER_LANES - N_GROUPS - N_EXPERTS
    wr = jnp.concatenate([w_rg, w_re, jnp.zeros((D_MODEL, pad), F32)], axis=1)
    br = jnp.concatenate([b_rg, b_re, jnp.zeros((pad,), F32)])[None, :]
    wr_hi = wr.astype(BF16)
    wr_lo = (wr - wr_hi.astype(F32)).astype(BF16)
    return wr_hi, wr_lo, br


def kernel(x, p, positions, g_mix, w_in, g_cq, w_uq, g_ckv, w_ukv, g_osb, g_omla, w_out, g_moe,
           w_rg, b_rg, w_re, b_re, w_gate, w_up, w_down, g_ple, w_pg, w_pe, g_final):
    b, s, d = x.shape
    t = b * s
    depth = w_in.shape[0]

    inv_freq = ROPE_THETA ** (-jnp.arange(0, MLA_ROPE, 2, dtype=F32) / MLA_ROPE)
    ang = positions.astype(F32)[..., None] * inv_freq
    reps = LANES // (MLA_ROPE // 2)
    cos = jnp.tile(jnp.cos(ang), (1, 1, reps)).reshape(t, LANES)
    sin = jnp.tile(jnp.sin(ang), (1, 1, reps)).reshape(t, LANES)

    h = x.reshape(t, d)
    n_rows = -(-(2 * t + N_EXPERTS * (EXPERT_TILE - 1)) // EXPERT_TILE) * EXPERT_TILE
    p_all = p.reshape(depth, t, D_PLE)
    wg_all = w_gate.reshape(depth * N_EXPERTS, D_MODEL, D_EXPERT)
    wu_all = w_up.reshape(depth * N_EXPERTS, D_MODEL, D_EXPERT)
    wd_all = w_down.reshape(depth * N_EXPERTS, D_EXPERT, D_MODEL)
    r3 = lambda a: a.reshape(b, s, a.shape[-1])
    r2 = lambda a: a.reshape(t, a.shape[-1])
    for i in range(depth):
        w1 = _prep_in(w_in[i])
        wqn, wqra, wqrb = _prep_uq(w_uq[i])
        wkvk, wkvv = _prep_ukv(w_ukv[i])
        qsb, ksb, vsb, qn, qr, kn, kr, vm = _mixer_in(
            h, g_mix[i][None], cos, sin, g_cq[i][None], g_ckv[i][None], w1, wqn, wqra, wqrb, wkvk, wkvv)
        osb = _sb_attention(r3(qsb), r3(ksb), r3(vsb))
        omla = _mla_attention(r3(qn), r3(qr), r3(kn), r3(kr), r3(vm))
        wrh, wrl, br = _prep_router(w_rg[i], b_rg[i], w_re[i], b_re[i])
        h1, xp, info, meta, counts = _mixer_out(h, r2(osb), r2(omla), g_osb[i][None], g_omla[i][None],
                                          w_out[i].astype(BF16), g_moe[i][None], wrh, wrl, br)
        pos, tile_expert, n_valid, pad_start, pad_len = _route_plan(meta, counts, n_rows)
        xs = _dispatch(pad_start, pad_len, pos, xp, n_rows)
        ys = _experts(tile_expert, n_valid, xs, wg_all, wu_all, wd_all, i)
        h = _combine_ple(pos, info, h1, p_all, i, g_ple[i][None], w_pg[i].astype(BF16),
                         w_pe[i].astype(BF16), g_final[None], ys, final=(i == depth - 1))
    return h.reshape(b, s, d)
```

```python
import functools

import jax
import jax.numpy as jnp
from jax import lax
from jax.experimental import pallas as pl
from jax.experimental.pallas import tpu as pltpu

F32 = jnp.float32
BF16 = jnp.bfloat16

D_MODEL = 1024
CHUNK = 64
D_PLE = 256
EPS = 1e-6
SB_HEADS = 8
SB_HEAD_DIM = 64
SB_WIDTH = SB_HEADS * SB_HEAD_DIM
MLA_HEADS = 8
MLA_NOPE = 64
MLA_ROPE = 32
MLA_QK = MLA_NOPE + MLA_ROPE
MLA_V = 64
MLA_Q_RANK = 384
MLA_KV_RANK = 256
MLA_WIDTH = MLA_HEADS * MLA_V
ROPE_THETA = 10000.0
N_GROUPS = 4
EXPERTS_PER_GROUP = 8
N_EXPERTS = N_GROUPS * EXPERTS_PER_GROUP
D_EXPERT = 256

LANES = 128
HEAD_PAIRS = SB_HEADS // 2
ROUTER_LANES = LANES
TOKEN_TILE = 512
EXPERT_TILE = 512
PACK_SUBLANES = D_MODEL // 2 // LANES
ZERO_ROWS = 64
ATTN_BLOCK = 256
LOG2_E = 1.4426950408889634
SB_UNDERFLOW = 151.0
VMEM_LIMIT = 48 * 1024 * 1024


def _params(*sem):
    return pltpu.CompilerParams(dimension_semantics=sem, vmem_limit_bytes=VMEM_LIMIT)


def _rms(x, g):
    return x * lax.rsqrt(jnp.mean(x * x, axis=-1, keepdims=True) + EPS) * g


def _dot(a, b):
    return jnp.dot(a, b, preferred_element_type=F32)


def _dot_t(a, b):
    return lax.dot_general(a, b, (((1,), (1,)), ((), ())), preferred_element_type=F32)


def _mixer_in_kernel(x_ref, g_ref, cos_ref, sin_ref, gcq_ref, gckv_ref, w1_ref, wqn_ref,
                     wqra_ref, wqrb_ref, wkvk_ref, wkvv_ref,
                     qsb_ref, ksb_ref, vsb_ref, qn_ref, qr_ref, kn_ref, kr_ref, vm_ref):
    xn = _rms(x_ref[...], g_ref[...]).astype(BF16)
    c0 = 0
    qsb_ref[...] = _dot(xn, w1_ref[:, c0:c0 + SB_WIDTH]).astype(BF16)
    c0 += SB_WIDTH
    ksb_ref[...] = _dot(xn, w1_ref[:, c0:c0 + SB_WIDTH]).astype(BF16)
    c0 += SB_WIDTH
    vsb_ref[...] = _dot(xn, w1_ref[:, c0:c0 + SB_WIDTH]).astype(BF16)
    c0 += SB_WIDTH
    cq = _dot(xn, w1_ref[:, c0:c0 + MLA_Q_RANK])
    c0 += MLA_Q_RANK
    ckv = _dot(xn, w1_ref[:, c0:c0 + MLA_KV_RANK])
    c0 += MLA_KV_RANK
    kra = _dot(xn, w1_ref[:, c0:c0 + LANES])
    c0 += LANES
    krb = _dot(xn, w1_ref[:, c0:c0 + LANES])
    cos = cos_ref[...]
    sin = sin_ref[...]
    kr_ref[...] = (kra * cos + krb * sin).astype(BF16)

    cqn = _rms(cq, gcq_ref[...]).astype(BF16)
    qn_ref[...] = _dot(cqn, wqn_ref[...]).astype(BF16)
    ra = _dot(cqn, wqra_ref[...])
    rb = _dot(cqn, wqrb_ref[...])
    for p in range(HEAD_PAIRS):
        sl = slice(p * LANES, (p + 1) * LANES)
        qr_ref[:, sl] = (ra[:, sl] * cos + rb[:, sl] * sin).astype(BF16)

    ckvn = _rms(ckv, gckv_ref[...]).astype(BF16)
    kn_ref[...] = _dot(ckvn, wkvk_ref[...]).astype(BF16)
    lane = lax.broadcasted_iota(jnp.int32, (1, LANES), 1)
    for hd in range(MLA_HEADS):
        sl = slice(hd * LANES, (hd + 1) * LANES)
        vm_ref[:, sl] = jnp.where(lane < MLA_V, _dot(ckvn, wkvv_ref[:, sl]), 1.0).astype(BF16)


def _mixer_in(h, g, cos, sin, gcq, gckv, w1, wqn, wqra, wqrb, wkvk, wkvv):
    t = h.shape[0]
    tm = TOKEN_TILE
    row = lambda w: pl.BlockSpec((tm, w), lambda i: (i, 0))
    full = lambda a: pl.BlockSpec(a.shape, lambda i: (0,) * a.ndim)
    widths = [SB_WIDTH, SB_WIDTH, SB_WIDTH, MLA_WIDTH, MLA_WIDTH, MLA_WIDTH, LANES, MLA_HEADS * LANES]
    return pl.pallas_call(
        _mixer_in_kernel,
        grid=(t // tm,),
        in_specs=[row(D_MODEL), full(g), row(LANES), row(LANES), full(gcq), full(gckv),
                  full(w1), full(wqn), full(wqra), full(wqrb), full(wkvk), full(wkvv)],
        out_specs=[row(w) for w in widths],
        out_shape=[jax.ShapeDtypeStruct((t, w), BF16) for w in widths],
        compiler_params=_params("parallel"),
        name="mixer_in",
    )(h, g, cos, sin, gcq, gckv, w1, wqn, wqra, wqrb, wkvk, wkvv)


def _sb_kernel(q_ref, k_ref, v_ref, o_ref, acc_ref, c_ref, *, blk):
    qi = pl.program_id(1)
    lane = lax.broadcasted_iota(jnp.int32, (1, LANES), 1)
    first = lane < SB_HEAD_DIM
    q_st = []
    for p in range(HEAD_PAIRS):
        q2 = q_ref[0, :, p * LANES:(p + 1) * LANES]
        zero = jnp.zeros_like(q2)
        q_st.append(jnp.concatenate([jnp.where(first, q2, zero), jnp.where(first, zero, q2)], axis=0))
    r = lax.broadcasted_iota(jnp.int32, (blk, blk), 0)
    c = lax.broadcasted_iota(jnp.int32, (blk, blk), 1)
    tri = jnp.where(r >= c, 1.0, 0.0).astype(BF16)
    tri2 = jnp.concatenate([tri, tri], axis=0)
    r2 = lax.broadcasted_iota(jnp.int32, (2 * blk, blk), 0)
    c2 = lax.broadcasted_iota(jnp.int32, (2 * blk, blk), 1)
    causal = c2 < jnp.where(r2 >= blk, r2 - blk, r2)
    hi_mask = jnp.uint32(0xFFFF0000)
    sign_bit = jnp.uint32(0x80000000)

    acc_ref[...] = jnp.zeros_like(acc_ref)
    c_ref[...] = jnp.zeros_like(c_ref)

    def block(kb, diag):
        start = pl.multiple_of(kb * blk, blk)
        cols = lambda ref, p: ref[0, pl.ds(start, blk), p * LANES:(p + 1) * LANES]
        z = [_dot_t(q_st[p], cols(k_ref, p)) for p in range(HEAD_PAIRS)]
        hi, lo = [], []
        for p in range(HEAD_PAIRS):
            neg_abs = pltpu.bitcast(pltpu.bitcast(z[p], jnp.uint32) | sign_bit, F32)
            sp = jnp.maximum(z[p], 0.0) + jnp.log(1.0 + jnp.exp2(neg_abs)) * LOG2_E
            if diag:
                sp = jnp.where(causal, sp, 0.0)
            top = pltpu.bitcast(pltpu.bitcast(sp, jnp.uint32) & hi_mask, F32)
            hi.append(top.astype(BF16))
            lo.append((sp - top).astype(BF16))
        cl = [_dot(jnp.concatenate([hi[p], lo[p]], axis=1), tri2) for p in range(HEAD_PAIRS)]
        w = []
        for p in range(HEAD_PAIRS):
            wp = jnp.exp2(z[p] - cl[p])
            if diag:
                wp = jnp.where(causal, wp, 0.0)
            w.append(wp.astype(BF16))
        pv = [_dot(w[p], cols(v_ref, p)) for p in range(HEAD_PAIRS)]
        for p in range(HEAD_PAIRS):
            carry = c_ref[p]
            acc_ref[p] += jnp.exp2(-carry) * pv[p]
            c_ref[p] = carry + jnp.broadcast_to(cl[p][:, 0:1], carry.shape)

    def min_carry():
        return jnp.min(jnp.min(c_ref[...], axis=0))

    block(qi, True)

    def cond(st):
        return (st[0] < qi) & (st[1] < SB_UNDERFLOW)

    def body(st):
        block(qi - 1 - st[0], False)
        return st[0] + 1, min_carry()

    lax.while_loop(cond, body, (jnp.int32(0), min_carry()))
    for p in range(HEAD_PAIRS):
        a = acc_ref[p]
        o_ref[0, :, p * LANES:(p + 1) * LANES] = jnp.where(first, a[:blk], a[blk:]).astype(o_ref.dtype)


def _sb_attention(q, k, v):
    b, s, wdt = q.shape
    blk = ATTN_BLOCK
    qspec = pl.BlockSpec((1, blk, wdt), lambda bi, qi: (bi, qi, 0))
    kspec = pl.BlockSpec((1, s, wdt), lambda bi, qi: (bi, 0, 0))
    state = pltpu.VMEM((HEAD_PAIRS, 2 * blk, LANES), F32)
    return pl.pallas_call(
        functools.partial(_sb_kernel, blk=blk),
        grid=(b, s // blk),
        in_specs=[qspec, kspec, kspec],
        out_specs=qspec,
        out_shape=jax.ShapeDtypeStruct(q.shape, BF16),
        scratch_shapes=[state, state],
        compiler_params=_params("parallel", "arbitrary"),
        name="sb_attention",
    )(q, k, v)


def _mla_kernel(qn_ref, qr_ref, kn_ref, kr_ref, v_ref, o_ref, acc_ref, m_ref, *, blk):
    qi = pl.program_id(1)
    lane2 = lax.broadcasted_iota(jnp.int32, (1, 2 * LANES), 1)
    sel0 = (lane2 < MLA_NOPE) | ((lane2 >= LANES) & (lane2 < LANES + MLA_ROPE))
    sel1 = ((lane2 >= MLA_NOPE) & (lane2 < LANES)) | (
        (lane2 >= LANES + MLA_ROPE) & (lane2 < LANES + 2 * MLA_ROPE))
    q_st = []
    for p in range(HEAD_PAIRS):
        sl = slice(p * LANES, (p + 1) * LANES)
        qcat = jnp.concatenate([qn_ref[0, :, sl], qr_ref[0, :, sl]], axis=1)
        zero = jnp.zeros_like(qcat)
        q_st.append(jnp.concatenate([jnp.where(sel0, qcat, zero), jnp.where(sel1, qcat, zero)], axis=0))
    r2 = lax.broadcasted_iota(jnp.int32, (2 * blk, blk), 0)
    c2 = lax.broadcasted_iota(jnp.int32, (2 * blk, blk), 1)
    visible = (c2 // CHUNK) <= (jnp.where(r2 >= blk, r2 - blk, r2) // CHUNK)
    lane = lax.broadcasted_iota(jnp.int32, (1, LANES), 1)
    first = lane < MLA_V

    acc_ref[...] = jnp.zeros_like(acc_ref)
    m_ref[...] = jnp.full(m_ref.shape, -jnp.inf, F32)

    def blocks(kbs, diag):
        starts = [pl.multiple_of(kb * blk, blk) for kb in kbs]
        sc = []
        for start in starts:
            kr_blk = kr_ref[0, pl.ds(start, blk), :]
            sc.append([_dot_t(q_st[p], jnp.concatenate(
                [kn_ref[0, pl.ds(start, blk), p * LANES:(p + 1) * LANES], kr_blk], axis=1))
                for p in range(HEAD_PAIRS)])
        pr, alpha = [], []
        for sc_b in sc:
            pr_b, alpha_b = [], []
            for p in range(HEAD_PAIRS):
                s_p = jnp.where(visible, sc_b[p], -jnp.inf) if diag else sc_b[p]
                m_prev = m_ref[p]
                m_new = jnp.maximum(m_prev, jnp.max(s_p, axis=1, keepdims=True))
                alpha_b.append(jnp.exp2(m_prev - m_new))
                pr_b.append(jnp.exp2(s_p - jnp.concatenate([m_new] * (blk // LANES), axis=1)).astype(BF16))
                m_ref[p] = m_new
            pr.append(pr_b)
            alpha.append(alpha_b)
        for h in range(MLA_HEADS):
            p, half = divmod(h, 2)
            rows = slice(half * blk, (half + 1) * blk)
            a = acc_ref[h]
            for start, pr_b, alpha_b in zip(starts, pr, alpha):
                a = alpha_b[p][rows] * a + _dot(pr_b[p][rows], v_ref[0, pl.ds(start, blk), h * LANES:(h + 1) * LANES])
            acc_ref[h] = a

    blocks([qi], True)

    def body(j, _):
        blocks([2 * j, 2 * j + 1], False)
        return 0

    lax.fori_loop(0, qi // 2, body, 0)

    @pl.when(qi % 2 == 1)
    def _():
        blocks([qi - 1], False)

    for p in range(HEAD_PAIRS):
        a0 = acc_ref[2 * p]
        a1 = acc_ref[2 * p + 1]
        o0 = a0 / pltpu.roll(a0, MLA_V, axis=1)
        o1 = pltpu.roll(a1 / pltpu.roll(a1, MLA_V, axis=1), MLA_V, axis=1)
        o_ref[0, :, p * LANES:(p + 1) * LANES] = jnp.where(first, o0, o1).astype(o_ref.dtype)


def _mla_attention(qn, qr, kn, kr, vcat):
    b, s, wdt = qn.shape
    blk = ATTN_BLOCK
    qspec = pl.BlockSpec((1, blk, wdt), lambda bi, qi: (bi, qi, 0))
    full = lambda a: pl.BlockSpec((1, s, a.shape[-1]), lambda bi, qi: (bi, 0, 0))
    return pl.pallas_call(
        functools.partial(_mla_kernel, blk=blk),
        grid=(b, s // blk),
        in_specs=[qspec, qspec, full(kn), full(kr), full(vcat)],
        out_specs=qspec,
        out_shape=jax.ShapeDtypeStruct(qn.shape, BF16),
        scratch_shapes=[pltpu.VMEM((MLA_HEADS, blk, LANES), F32),
                        pltpu.VMEM((HEAD_PAIRS, 2 * blk, LANES), F32)],
        compiler_params=_params("parallel", "arbitrary"),
        name="mla_attention",
    )(qn, qr, kn, kr, vcat)


def _route(logits):
    lane = lax.broadcasted_iota(jnp.int32, logits.shape, 1).astype(F32)
    ninf = -jnp.inf
    big = float(ROUTER_LANES)
    is_g = lane < N_GROUPS
    lg = jnp.where(is_g, logits, ninf)
    gmax = jnp.max(lg, axis=1, keepdims=True)
    gsum = jnp.sum(jnp.where(is_g, jnp.exp(lg - gmax), 0.0), axis=1, keepdims=True)
    gp = 1.0 / gsum
    g = jnp.min(jnp.where(lg == gmax, lane, big), axis=1, keepdims=True)
    lo = N_GROUPS + EXPERTS_PER_GROUP * g
    in_grp = (lane >= lo) & (lane < lo + EXPERTS_PER_GROUP)
    le = jnp.where(in_grp, logits, ninf)
    l1 = jnp.max(le, axis=1, keepdims=True)
    i1 = jnp.min(jnp.where(le == l1, lane, big), axis=1, keepdims=True)
    le2 = jnp.where(lane == i1, ninf, le)
    l2 = jnp.max(le2, axis=1, keepdims=True)
    i2 = jnp.min(jnp.where(le2 == l2, lane, big), axis=1, keepdims=True)
    t = jnp.exp(l2 - l1)
    w1 = gp / (1.0 + t)
    w2 = gp * t / (1.0 + t)
    return i1 - N_GROUPS, i2 - N_GROUPS, w1, w2


def _pack_bf16_pairs(x):
    n = x.shape[1] // 2
    xb = x.astype(BF16).astype(F32)
    hi = pltpu.bitcast(xb[:, :n], jnp.uint32)
    lo = pltpu.bitcast(xb[:, n:], jnp.uint32)
    return hi | (lo >> 16)


def _unpack_bf16_pairs(w):
    hi = pltpu.bitcast(w & jnp.uint32(0xFFFF0000), F32)
    lo = pltpu.bitcast(w << 16, F32)
    return hi, lo


def _to_slabs(rows):
    return rows.reshape(rows.shape[0], PACK_SUBLANES, LANES)


def _from_slabs(slabs):
    return slabs.reshape(slabs.shape[0], PACK_SUBLANES * LANES)


R_E1, R_E2, R_W1, R_W2, R_RANK1, R_RANK2 = range(6)
META_ROWS = 8


def _mixer_out_kernel(h_ref, osb_ref, omla_ref, gosb_ref, gomla_ref, wout_ref, gmoe_ref,
                      wrh_ref, wrl_ref, br_ref, h1_ref, xp_ref, info_ref, meta_ref, cnt_ref):
    @pl.when(pl.program_id(0) == 0)
    def _():
        cnt_ref[...] = jnp.zeros_like(cnt_ref)

    tm = h_ref.shape[0]
    hm = tm // 2
    halves = [slice(0, hm), slice(hm, tm)]
    nsb = [_rms(osb_ref[hs, :].astype(F32), gosb_ref[...]).astype(BF16) for hs in halves]
    nmla = [_rms(omla_ref[hs, :].astype(F32), gomla_ref[...]).astype(BF16) for hs in halves]
    h1 = [h_ref[hs, :] + _dot(nsb[i], wout_ref[0:SB_WIDTH, :]) + _dot(nmla[i], wout_ref[SB_WIDTH:, :])
          for i, hs in enumerate(halves)]
    xn = []
    for i, hs in enumerate(halves):
        h1_ref[hs, :] = h1[i]
        xn.append(_rms(h1[i], gmoe_ref[...]))
        xp_ref[hs] = _to_slabs(_pack_bf16_pairs(xn[i]))
    x_hi = [x.astype(BF16) for x in xn]
    x_lo = [(x - xh.astype(F32)).astype(BF16) for x, xh in zip(xn, x_hi)]
    logits = [(_dot(x_hi[i], wrh_ref[...]) + _dot(x_hi[i], wrl_ref[...]) + _dot(x_lo[i], wrh_ref[...])) + br_ref[...]
              for i in range(2)]
    routes = [_route(lg) for lg in logits]

    lane = lax.broadcasted_iota(jnp.int32, logits[0].shape, 1).astype(F32)
    r = lax.broadcasted_iota(jnp.int32, (hm, hm), 0)
    c = lax.broadcasted_iota(jnp.int32, (hm, hm), 1)
    before = jnp.where(c < r, 1.0, 0.0).astype(BF16)
    onehot = [jnp.where((lane == e1) | (lane == e2), 1.0, 0.0) for e1, e2, _, _ in routes]
    prefix = [_dot(before, oh.astype(BF16)) for oh in onehot]
    base = cnt_ref[...]
    for i, hs in enumerate(halves):
        e1, e2, w1, w2 = routes[i]
        seen = prefix[i] + base
        rank1 = jnp.sum(jnp.where(lane == e1, seen, 0.0), axis=1, keepdims=True)
        rank2 = jnp.sum(jnp.where(lane == e2, seen, 0.0), axis=1, keepdims=True)
        base = base + jnp.sum(onehot[i], axis=0, keepdims=True)
        info = jnp.zeros_like(logits[i])
        for idx, val in ((R_E1, e1), (R_E2, e2), (R_W1, w1), (R_W2, w2), (R_RANK1, rank1), (R_RANK2, rank2)):
            info = jnp.where(lane == idx, val, info)
        info_ref[hs, :] = info
        meta_ref[:, hs] = info.T[0:META_ROWS, :]
    cnt_ref[...] = base


def _mixer_out(h, osb, omla, gosb, gomla, wout, gmoe, wrh, wrl, br):
    t = h.shape[0]
    tm = 2 * TOKEN_TILE
    row = lambda w: pl.BlockSpec((tm, w), lambda i: (i, 0))
    full = lambda a: pl.BlockSpec(a.shape, lambda i: (0,) * a.ndim)
    return pl.pallas_call(
        _mixer_out_kernel,
        grid=(t // tm,),
        in_specs=[row(D_MODEL), row(SB_WIDTH), row(MLA_WIDTH), full(gosb), full(gomla), full(wout),
                  full(gmoe), full(wrh), full(wrl), full(br)],
        out_specs=[row(D_MODEL), pl.BlockSpec((tm, PACK_SUBLANES, LANES), lambda i: (i, 0, 0)), row(ROUTER_LANES),
                   pl.BlockSpec((META_ROWS, tm), lambda i: (0, i)), pl.BlockSpec((1, ROUTER_LANES), lambda i: (0, 0))],
        out_shape=[jax.ShapeDtypeStruct((t, D_MODEL), F32),
                   jax.ShapeDtypeStruct((t, PACK_SUBLANES, LANES), jnp.uint32),
                   jax.ShapeDtypeStruct((t, ROUTER_LANES), F32), jax.ShapeDtypeStruct((META_ROWS, t), F32),
                   jax.ShapeDtypeStruct((1, ROUTER_LANES), F32)],
        compiler_params=_params("arbitrary"),
        name="mixer_out",
    )(h, osb, omla, gosb, gomla, wout, gmoe, wrh, wrl, br)


def _route_plan(meta, counts, n_rows):
    te = EXPERT_TILE
    cnt = counts[0, :N_EXPERTS].astype(jnp.int32)
    padded = (cnt + te - 1) // te * te
    seg_end = jnp.cumsum(padded)
    seg_start = seg_end - padded
    e = meta[R_E1:R_E2 + 1].astype(jnp.int32)
    rank = meta[R_RANK1:R_RANK2 + 1].astype(jnp.int32)
    ids = jnp.arange(N_EXPERTS, dtype=jnp.int32)
    pos = jnp.sum(jnp.where(e[..., None] == ids, seg_start, 0), axis=-1) + rank
    tiles = pos.shape[1] // TOKEN_TILE
    pos = pos.reshape(2, tiles, TOKEN_TILE).transpose(1, 0, 2).reshape(tiles, 1, 2 * TOKEN_TILE)
    tile_start = jnp.arange(n_rows // te, dtype=jnp.int32) * te
    tile_expert = jnp.minimum(jnp.sum(tile_start[:, None] >= seg_end[None, :], axis=1), N_EXPERTS - 1)
    n_valid = (seg_end[-1] // te).reshape(1)
    pad_start = jnp.concatenate([seg_start + cnt, seg_end[-1:]])
    pad_len = jnp.concatenate([padded - cnt, n_rows - seg_end[-1:]])
    return pos, tile_expert.astype(jnp.int32), n_valid.astype(jnp.int32), pad_start, pad_len


def _dispatch_kernel(pad_start_ref, pad_len_ref, pos_ref, x_ref, xs_hbm, zero, sem, zsem):
    tm = x_ref.shape[0]

    @pl.when(pl.program_id(0) == 0)
    def _():
        zero[...] = jnp.zeros_like(zero)

        def each_range(copy_fn):
            def per_range(e, _):
                start = pad_start_ref[e]
                n = pad_len_ref[e]
                runs = n // ZERO_ROWS
                lax.fori_loop(0, runs, lambda j, c: copy_fn(
                    pltpu.make_async_copy(zero, xs_hbm.at[pl.ds(start + j * ZERO_ROWS, ZERO_ROWS)], zsem)), 0)
                lax.fori_loop(runs * ZERO_ROWS, n, lambda j, c: copy_fn(
                    pltpu.make_async_copy(zero.at[0], xs_hbm.at[start + j], zsem)), 0)
                return 0

            lax.fori_loop(0, pad_start_ref.shape[0], per_range, 0)

        def start_copy(cp):
            cp.start()
            return 0

        def wait_copy(cp):
            cp.wait()
            return 0

        each_range(start_copy)
        each_range(wait_copy)

    def issue(t, _):
        for k in range(2):
            pltpu.make_async_copy(x_ref.at[t], xs_hbm.at[pos_ref[0, 0, k * tm + t]],
                                  sem).start(priority=k)
        return 0

    lax.fori_loop(0, tm, issue, 0, unroll=8)
    for _ in range(2):
        pltpu.make_async_copy(x_ref, xs_hbm.at[pl.ds(0, tm)], sem).wait()


def _dispatch(pad_start, pad_len, pos, xp, n_rows):
    t = xp.shape[0]
    tm = TOKEN_TILE
    slab = xp.shape[1:]
    return pl.pallas_call(
        _dispatch_kernel,
        grid_spec=pltpu.PrefetchScalarGridSpec(
            num_scalar_prefetch=2,
            grid=(t // tm,),
            in_specs=[pl.BlockSpec((1, 1, 2 * tm), lambda i, ps, pn: (i, 0, 0), memory_space=pltpu.SMEM),
                      pl.BlockSpec((tm,) + slab, lambda i, ps, pn: (i, 0, 0))],
            out_specs=pl.BlockSpec(memory_space=pl.ANY),
            scratch_shapes=[pltpu.VMEM((ZERO_ROWS,) + slab, xp.dtype), pltpu.SemaphoreType.DMA(()),
                            pltpu.SemaphoreType.DMA(())],
        ),
        out_shape=jax.ShapeDtypeStruct((n_rows,) + slab, xp.dtype),
        compiler_params=_params("arbitrary"),
        name="moe_dispatch",
    )(pad_start, pad_len, pos, xp)


def _expert_kernel(te_ref, nv_ref, xs_ref, wg_ref, wu_ref, wd_ref, ys_ref, wg16, wu16, wd16):
    i = pl.program_id(0)
    live = i < nv_ref[0]

    @pl.when(live & ((i == 0) | (te_ref[i] != te_ref[jnp.maximum(i - 1, 0)])))
    def _():
        wg16[...] = wg_ref[0].astype(BF16)
        wu16[...] = wu_ref[0].astype(BF16)
        wd16[...] = wd_ref[0].astype(BF16)

    @pl.when(live)
    def _():
        half = D_MODEL // 2
        xa, xb = _unpack_bf16_pairs(_from_slabs(xs_ref[...]))
        xa = xa.astype(BF16)
        xb = xb.astype(BF16)
        a = _dot(xa, wg16[:half, :]) + _dot(xb, wg16[half:, :])
        u = _dot(xa, wu16[:half, :]) + _dot(xb, wu16[half:, :])
        hid = (a / (1.0 + jnp.exp(-a))) * u
        ys_ref[...] = _to_slabs(_pack_bf16_pairs(_dot(hid.astype(BF16), wd16[...])))

    @pl.when(i >= nv_ref[0])
    def _():
        ys_ref[...] = jnp.zeros_like(ys_ref)


def _experts(tile_expert, n_valid, xs, wg, wu, wd, layer):
    n_rows = xs.shape[0]
    te = EXPERT_TILE
    base = layer * N_EXPERTS
    last = lambda i, te_ref, nv_ref: jnp.minimum(i, nv_ref[0] - 1)
    slab = (te,) + xs.shape[1:]
    wspec = lambda shape: pl.BlockSpec(
        (1,) + shape, lambda i, te_ref, nv_ref: (base + te_ref[last(i, te_ref, nv_ref)], 0, 0))
    return pl.pallas_call(
        _expert_kernel,
        grid_spec=pltpu.PrefetchScalarGridSpec(
            num_scalar_prefetch=2,
            grid=(n_rows // te,),
            in_specs=[pl.BlockSpec(slab, lambda i, te_ref, nv_ref: (last(i, te_ref, nv_ref), 0, 0)),
                      wspec((D_MODEL, D_EXPERT)), wspec((D_MODEL, D_EXPERT)), wspec((D_EXPERT, D_MODEL))],
            out_specs=pl.BlockSpec(slab, lambda i, te_ref, nv_ref: (i, 0, 0)),
            scratch_shapes=[pltpu.VMEM((D_MODEL, D_EXPERT), BF16), pltpu.VMEM((D_MODEL, D_EXPERT), BF16),
                            pltpu.VMEM((D_EXPERT, D_MODEL), BF16)],
        ),
        out_shape=jax.ShapeDtypeStruct(xs.shape, xs.dtype),
        compiler_params=_params("arbitrary"),
        name="moe_experts",
    )(tile_expert, n_valid, xs, wg, wu, wd)


def _combine_ple_kernel(pos_ref, pos_next_ref, info_ref, h_ref, p_ref, g_ref, wpg_ref, wpe_ref, gf_ref, ys_hbm,
                        o_ref, buf, sem, *, final):
    i = pl.program_id(0)
    tm = h_ref.shape[0]
    slot = i % 2

    def gather(rows_ref, dst):
        def issue(t, _):
            for k in range(2):
                pltpu.make_async_copy(ys_hbm.at[rows_ref[0, 0, k * tm + t]],
                                      buf.at[dst, k * tm + t], sem.at[dst]).start(priority=k)
            return 0

        lax.fori_loop(0, tm, issue, 0, unroll=8)

    @pl.when(i == 0)
    def _():
        gather(pos_ref, 0)

    @pl.when(i + 1 < pl.num_programs(0))
    def _():
        gather(pos_next_ref, 1 - slot)

    info = info_ref[...]
    lane = lax.broadcasted_iota(jnp.int32, info.shape, 1)
    gate1 = jnp.sum(jnp.where(lane == R_W1, info, 0.0), axis=1, keepdims=True)
    gate2 = jnp.sum(jnp.where(lane == R_W2, info, 0.0), axis=1, keepdims=True)
    pltpu.make_async_copy(ys_hbm.at[pl.ds(0, 2 * tm)], buf.at[slot], sem.at[slot]).wait()
    y1a, y1b = _unpack_bf16_pairs(_from_slabs(buf[slot, 0:tm]))
    y2a, y2b = _unpack_bf16_pairs(_from_slabs(buf[slot, tm:2 * tm]))
    y = jnp.concatenate([gate1 * y1a + gate2 * y2a, gate1 * y1b + gate2 * y2b], axis=1)
    h = h_ref[...] + y
    xn = _rms(h, g_ref[...]).astype(BF16)
    gate = 1.0 / (1.0 + jnp.exp(-_dot(xn, wpg_ref[...])))
    out = h + gate * _dot(p_ref[0].astype(BF16), wpe_ref[...])
    if final:
        out = _rms(out, gf_ref[...])
    o_ref[...] = out


def _combine_ple(pos, info, h, p, layer, g, wpg, wpe, gf, ys, final):
    t = h.shape[0]
    tm = TOKEN_TILE
    n = t // tm
    row = lambda w: pl.BlockSpec((tm, w), lambda i: (i, 0))
    full = lambda a: pl.BlockSpec(a.shape, lambda i: (0,) * a.ndim)
    return pl.pallas_call(
        functools.partial(_combine_ple_kernel, final=final),
        grid=(n,),
        in_specs=[pl.BlockSpec((1, 1, 2 * tm), lambda i: (i, 0, 0), memory_space=pltpu.SMEM),
                  pl.BlockSpec((1, 1, 2 * tm), lambda i: (jnp.minimum(i + 1, n - 1), 0, 0), memory_space=pltpu.SMEM),
                  row(ROUTER_LANES), row(D_MODEL), pl.BlockSpec((1, tm, D_PLE), lambda i: (layer, i, 0)),
                  full(g), full(wpg), full(wpe), full(gf),
                  pl.BlockSpec(memory_space=pl.ANY)],
        out_specs=row(D_MODEL),
        out_shape=jax.ShapeDtypeStruct((t, D_MODEL), F32),
        scratch_shapes=[pltpu.VMEM((2, 2 * tm) + ys.shape[1:], ys.dtype), pltpu.SemaphoreType.DMA((2,))],
        compiler_params=_params("arbitrary"),
        name="moe_combine_ple",
    )(pos, pos, info, h, p, g, wpg, wpe, gf, ys)


def _rot_cols(w):
    half = w.shape[-1] // 2
    return jnp.concatenate([-w[:, half:], w[:, :half]], axis=1)


def _prep_in(w_in):
    sb_scale = SB_HEAD_DIM ** -0.5 * LOG2_E
    kr = w_in[:, 3 * SB_WIDTH + MLA_Q_RANK + MLA_KV_RANK:]
    pad = jnp.zeros((w_in.shape[0], LANES - 2 * MLA_ROPE), w_in.dtype)
    kra = jnp.concatenate([kr, kr, pad], axis=1)
    krr = _rot_cols(kr)
    krb = jnp.concatenate([krr, krr, pad], axis=1)
    w1 = jnp.concatenate([w_in[:, :SB_WIDTH] * sb_scale,
                          w_in[:, SB_WIDTH:3 * SB_WIDTH + MLA_Q_RANK + MLA_KV_RANK], kra, krb], axis=1)
    return w1.astype(BF16)


def _prep_uq(w_uq):
    scale = MLA_QK ** -0.5 * LOG2_E
    w = w_uq.reshape(MLA_Q_RANK, MLA_HEADS, MLA_QK) * scale
    wqn = w[:, :, :MLA_NOPE].reshape(MLA_Q_RANK, MLA_HEADS * MLA_NOPE)
    rope = w[:, :, MLA_NOPE:]
    half = MLA_ROPE // 2
    rot = jnp.concatenate([-rope[:, :, half:], rope[:, :, :half]], axis=2)
    pad = jnp.zeros((MLA_Q_RANK, HEAD_PAIRS, LANES - 2 * MLA_ROPE), w.dtype)

    def pairs(r):
        return jnp.concatenate([r.reshape(MLA_Q_RANK, HEAD_PAIRS, 2 * MLA_ROPE), pad], axis=2).reshape(
            MLA_Q_RANK, HEAD_PAIRS * LANES)

    return wqn.astype(BF16), pairs(rope).astype(BF16), pairs(rot).astype(BF16)


def _prep_ukv(w_ukv):
    w = w_ukv.reshape(MLA_KV_RANK, MLA_HEADS, MLA_NOPE + MLA_V)
    wk = w[:, :, :MLA_NOPE].reshape(MLA_KV_RANK, MLA_HEADS * MLA_NOPE)
    pad = jnp.zeros((MLA_KV_RANK, MLA_HEADS, LANES - MLA_V), w.dtype)
    wv = jnp.concatenate([w[:, :, MLA_NOPE:], pad], axis=2).reshape(MLA_KV_RANK, MLA_HEADS * LANES)
    return wk.astype(BF16), wv.astype(BF16)


def _prep_router(w_rg, b_rg, w_re, b_re):
    pad = ROUTER_LANES - N_GROUPS - N_EXPERTS
    wr = jnp.concatenate([w_rg, w_re, jnp.zeros((D_MODEL, pad), F32)], axis=1)
    br = jnp.concatenate([b_rg, b_re, jnp.zeros((pad,), F32)])[None, :]
    wr_hi = wr.astype(BF16)
    wr_lo = (wr - wr_hi.astype(F32)).astype(BF16)
    return wr_hi, wr_lo, br


def kernel(x, p, positions, g_mix, w_in, g_cq, w_uq, g_ckv, w_ukv, g_osb, g_omla, w_out, g_moe,
           w_rg, b_rg, w_re, b_re, w_gate, w_up, w_down, g_ple, w_pg, w_pe, g_final):
    b, s, d = x.shape
    t = b * s
    depth = w_in.shape[0]

    inv_freq = ROPE_THETA ** (-jnp.arange(0, MLA_ROPE, 2, dtype=F32) / MLA_ROPE)
    ang = positions.astype(F32)[..., None] * inv_freq
    reps = LANES // (MLA_ROPE // 2)
    cos = jnp.tile(jnp.cos(ang), (1, 1, reps)).reshape(t, LANES)
    sin = jnp.tile(jnp.sin(ang), (1, 1, reps)).reshape(t, LANES)

    h = x.reshape(t, d)
    n_rows = -(-(2 * t + N_EXPERTS * (EXPERT_TILE - 1)) // EXPERT_TILE) * EXPERT_TILE
    p_all = p.reshape(depth, t, D_PLE)
    wg_all = w_gate.reshape(depth * N_EXPERTS, D_MODEL, D_EXPERT)
    wu_all = w_up.reshape(depth * N_EXPERTS, D_MODEL, D_EXPERT)
    wd_all = w_down.reshape(depth * N_EXPERTS, D_EXPERT, D_MODEL)
    r3 = lambda a: a.reshape(b, s, a.shape[-1])
    r2 = lambda a: a.reshape(t, a.shape[-1])
    for i in range(depth):
        w1 = _prep_in(w_in[i])
        wqn, wqra, wqrb = _prep_uq(w_uq[i])
        wkvk, wkvv = _prep_ukv(w_ukv[i])
        qsb, ksb, vsb, qn, qr, kn, kr, vm = _mixer_in(
            h, g_mix[i][None], cos, sin, g_cq[i][None], g_ckv[i][None], w1, wqn, wqra, wqrb, wkvk, wkvv)
        osb = _sb_attention(r3(qsb), r3(ksb), r3(vsb))
        omla = _mla_attention(r3(qn), r3(qr), r3(kn), r3(kr), r3(vm))
        wrh, wrl, br = _prep_router(w_rg[i], b_rg[i], w_re[i], b_re[i])
        h1, xp, info, meta, counts = _mixer_out(h, r2(osb), r2(omla), g_osb[i][None], g_omla[i][None],
                                          w_out[i].astype(BF16), g_moe[i][None], wrh, wrl, br)
        pos, tile_expert, n_valid, pad_start, pad_len = _route_plan(meta, counts, n_rows)
        xs = _dispatch(pad_start, pad_len, pos, xp, n_rows)
        ys = _experts(tile_expert, n_valid, xs, wg_all, wu_all, wd_all, i)
        h = _combine_ple(pos, info, h1, p_all, i, g_ple[i][None], w_pg[i].astype(BF16),
                         w_pe[i].astype(BF16), g_final[None], ys, final=(i == depth - 1))
    return h.reshape(b, s, d)
```

```python
import functools

import jax
import jax.numpy as jnp
from jax import lax
from jax.experimental import pallas as pl
from jax.experimental.pallas import tpu as pltpu

F32 = jnp.float32
BF16 = jnp.bfloat16

D_MODEL = 1024
CHUNK = 64
D_PLE = 256
EPS = 1e-6
SB_HEADS = 8
SB_HEAD_DIM = 64
SB_WIDTH = SB_HEADS * SB_HEAD_DIM
MLA_HEADS = 8
MLA_NOPE = 64
MLA_ROPE = 32
MLA_QK = MLA_NOPE + MLA_ROPE
MLA_V = 64
MLA_Q_RANK = 384
MLA_KV_RANK = 256
MLA_WIDTH = MLA_HEADS * MLA_V
ROPE_THETA = 10000.0
N_GROUPS = 4
EXPERTS_PER_GROUP = 8
N_EXPERTS = N_GROUPS * EXPERTS_PER_GROUP
D_EXPERT = 256

LANES = 128
HEAD_PAIRS = SB_HEADS // 2
ROUTER_LANES = LANES
TOKEN_TILE = 512
EXPERT_TILE = 512
PACK_SUBLANES = D_MODEL // 2 // LANES
ZERO_ROWS = 64
ATTN_BLOCK = 256
LOG2_E = 1.4426950408889634
SB_UNDERFLOW = 151.0
VMEM_LIMIT = 48 * 1024 * 1024


def _params(*sem):
    return pltpu.CompilerParams(dimension_semantics=sem, vmem_limit_bytes=VMEM_LIMIT)


def _rms(x, g):
    return x * lax.rsqrt(jnp.mean(x * x, axis=-1, keepdims=True) + EPS) * g


def _dot(a, b):
    return jnp.dot(a, b, preferred_element_type=F32)


def _dot_t(a, b):
    return lax.dot_general(a, b, (((1,), (1,)), ((), ())), preferred_element_type=F32)


def _mixer_in_kernel(x_ref, g_ref, cos_ref, sin_ref, gcq_ref, gckv_ref, w1_ref, wqn_ref,
                     wqra_ref, wqrb_ref, wkvk_ref, wkvv_ref,
                     qsb_ref, ksb_ref, vsb_ref, qn_ref, qr_ref, kn_ref, kr_ref, vm_ref):
    xn = _rms(x_ref[...], g_ref[...]).astype(BF16)
    c0 = 0
    qsb_ref[...] = _dot(xn, w1_ref[:, c0:c0 + SB_WIDTH]).astype(BF16)
    c0 += SB_WIDTH
    ksb_ref[...] = _dot(xn, w1_ref[:, c0:c0 + SB_WIDTH]).astype(BF16)
    c0 += SB_WIDTH
    vsb_ref[...] = _dot(xn, w1_ref[:, c0:c0 + SB_WIDTH]).astype(BF16)
    c0 += SB_WIDTH
    cq = _dot(xn, w1_ref[:, c0:c0 + MLA_Q_RANK])
    c0 += MLA_Q_RANK
    ckv = _dot(xn, w1_ref[:, c0:c0 + MLA_KV_RANK])
    c0 += MLA_KV_RANK
    kra = _dot(xn, w1_ref[:, c0:c0 + LANES])
    c0 += LANES
    krb = _dot(xn, w1_ref[:, c0:c0 + LANES])
    cos = cos_ref[...]
    sin = sin_ref[...]
    kr_ref[...] = (kra * cos + krb * sin).astype(BF16)

    cqn = _rms(cq, gcq_ref[...]).astype(BF16)
    qn_ref[...] = _dot(cqn, wqn_ref[...]).astype(BF16)
    ra = _dot(cqn, wqra_ref[...])
    rb = _dot(cqn, wqrb_ref[...])
    for p in range(HEAD_PAIRS):
        sl = slice(p * LANES, (p + 1) * LANES)
        qr_ref[:, sl] = (ra[:, sl] * cos + rb[:, sl] * sin).astype(BF16)

    ckvn = _rms(ckv, gckv_ref[...]).astype(BF16)
    kn_ref[...] = _dot(ckvn, wkvk_ref[...]).astype(BF16)
    lane = lax.broadcasted_iota(jnp.int32, (1, LANES), 1)
    for hd in range(MLA_HEADS):
        sl = slice(hd * LANES, (hd + 1) * LANES)
        vm_ref[:, sl] = jnp.where(lane < MLA_V, _dot(ckvn, wkvv_ref[:, sl]), 1.0).astype(BF16)


def _mixer_in(h, g, cos, sin, gcq, gckv, w1, wqn, wqra, wqrb, wkvk, wkvv):
    t = h.shape[0]
    tm = TOKEN_TILE
    row = lambda w: pl.BlockSpec((tm, w), lambda i: (i, 0))
    full = lambda a: pl.BlockSpec(a.shape, lambda i: (0,) * a.ndim)
    widths = [SB_WIDTH, SB_WIDTH, SB_WIDTH, MLA_WIDTH, MLA_WIDTH, MLA_WIDTH, LANES, MLA_HEADS * LANES]
    return pl.pallas_call(
        _mixer_in_kernel,
        grid=(t // tm,),
        in_specs=[row(D_MODEL), full(g), row(LANES), row(LANES), full(gcq), full(gckv),
                  full(w1), full(wqn), full(wqra), full(wqrb), full(wkvk), full(wkvv)],
        out_specs=[row(w) for w in widths],
        out_shape=[jax.ShapeDtypeStruct((t, w), BF16) for w in widths],
        compiler_params=_params("parallel"),
        name="mixer_in",
    )(h, g, cos, sin, gcq, gckv, w1, wqn, wqra, wqrb, wkvk, wkvv)


def _sb_kernel(q_ref, k_ref, v_ref, o_ref, acc_ref, c_ref, *, blk):
    qi = pl.program_id(1)
    lane = lax.broadcasted_iota(jnp.int32, (1, LANES), 1)
    first = lane < SB_HEAD_DIM
    q_st = []
    for p in range(HEAD_PAIRS):
        q2 = q_ref[0, :, p * LANES:(p + 1) * LANES]
        zero = jnp.zeros_like(q2)
        q_st.append(jnp.concatenate([jnp.where(first, q2, zero), jnp.where(first, zero, q2)], axis=0))
    r = lax.broadcasted_iota(jnp.int32, (blk, blk), 0)
    c = lax.broadcasted_iota(jnp.int32, (blk, blk), 1)
    tri = jnp.where(r >= c, 1.0, 0.0).astype(BF16)
    tri2 = jnp.concatenate([tri, tri], axis=0)
    r2 = lax.broadcasted_iota(jnp.int32, (2 * blk, blk), 0)
    c2 = lax.broadcasted_iota(jnp.int32, (2 * blk, blk), 1)
    causal = c2 < jnp.where(r2 >= blk, r2 - blk, r2)
    hi_mask = jnp.uint32(0xFFFF0000)
    sign_bit = jnp.uint32(0x80000000)

    acc_ref[...] = jnp.zeros_like(acc_ref)
    c_ref[...] = jnp.zeros_like(c_ref)

    def block(kb, diag):
        start = pl.multiple_of(kb * blk, blk)
        cols = lambda ref, p: ref[0, pl.ds(start, blk), p * LANES:(p + 1) * LANES]
        z = [_dot_t(q_st[p], cols(k_ref, p)) for p in range(HEAD_PAIRS)]
        hi, lo = [], []
        for p in range(HEAD_PAIRS):
            neg_abs = pltpu.bitcast(pltpu.bitcast(z[p], jnp.uint32) | sign_bit, F32)
            sp = jnp.maximum(z[p], 0.0) + jnp.log(1.0 + jnp.exp2(neg_abs)) * LOG2_E
            if diag:
                sp = jnp.where(causal, sp, 0.0)
            top = pltpu.bitcast(pltpu.bitcast(sp, jnp.uint32) & hi_mask, F32)
            hi.append(top.astype(BF16))
            lo.append((sp - top).astype(BF16))
        cl = [_dot(jnp.concatenate([hi[p], lo[p]], axis=1), tri2) for p in range(HEAD_PAIRS)]
        w = []
        for p in range(HEAD_PAIRS):
            wp = jnp.exp2(z[p] - cl[p])
            if diag:
                wp = jnp.where(causal, wp, 0.0)
            w.append(wp.astype(BF16))
        pv = [_dot(w[p], cols(v_ref, p)) for p in range(HEAD_PAIRS)]
        for p in range(HEAD_PAIRS):
            carry = c_ref[p]
            acc_ref[p] += jnp.exp2(-carry) * pv[p]
            c_ref[p] = carry + jnp.broadcast_to(cl[p][:, 0:1], carry.shape)

    def min_carry():
        return jnp.min(jnp.min(c_ref[...], axis=0))

    block(qi, True)

    def cond(st):
        return (st[0] < qi) & (st[1] < SB_UNDERFLOW)

    def body(st):
        block(qi - 1 - st[0], False)
        return st[0] + 1, min_carry()

    lax.while_loop(cond, body, (jnp.int32(0), min_carry()))
    for p in range(HEAD_PAIRS):
        a = acc_ref[p]
        o_ref[0, :, p * LANES:(p + 1) * LANES] = jnp.where(first, a[:blk], a[blk:]).astype(o_ref.dtype)


def _sb_attention(q, k, v):
    b, s, wdt = q.shape
    blk = ATTN_BLOCK
    qspec = pl.BlockSpec((1, blk, wdt), lambda bi, qi: (bi, qi, 0))
    kspec = pl.BlockSpec((1, s, wdt), lambda bi, qi: (bi, 0, 0))
    state = pltpu.VMEM((HEAD_PAIRS, 2 * blk, LANES), F32)
    return pl.pallas_call(
        functools.partial(_sb_kernel, blk=blk),
        grid=(b, s // blk),
        in_specs=[qspec, kspec, kspec],
        out_specs=qspec,
        out_shape=jax.ShapeDtypeStruct(q.shape, BF16),
        scratch_shapes=[state, state],
        compiler_params=_params("parallel", "arbitrary"),
        name="sb_attention",
    )(q, k, v)


def _mla_kernel(qn_ref, qr_ref, kn_ref, kr_ref, v_ref, o_ref, acc_ref, m_ref, *, blk):
    qi = pl.program_id(1)
    lane2 = lax.broadcasted_iota(jnp.int32, (1, 2 * LANES), 1)
    sel0 = (lane2 < MLA_NOPE) | ((lane2 >= LANES) & (lane2 < LANES + MLA_ROPE))
    sel1 = ((lane2 >= MLA_NOPE) & (lane2 < LANES)) | (
        (lane2 >= LANES + MLA_ROPE) & (lane2 < LANES + 2 * MLA_ROPE))
    q_st = []
    for p in range(HEAD_PAIRS):
        sl = slice(p * LANES, (p + 1) * LANES)
        qcat = jnp.concatenate([qn_ref[0, :, sl], qr_ref[0, :, sl]], axis=1)
        zero = jnp.zeros_like(qcat)
        q_st.append(jnp.concatenate([jnp.where(sel0, qcat, zero), jnp.where(sel1, qcat, zero)], axis=0))
    r2 = lax.broadcasted_iota(jnp.int32, (2 * blk, blk), 0)
    c2 = lax.broadcasted_iota(jnp.int32, (2 * blk, blk), 1)
    visible = (c2 // CHUNK) <= (jnp.where(r2 >= blk, r2 - blk, r2) // CHUNK)
    lane = lax.broadcasted_iota(jnp.int32, (1, LANES), 1)
    first = lane < MLA_V

    acc_ref[...] = jnp.zeros_like(acc_ref)
    m_ref[...] = jnp.full(m_ref.shape, -jnp.inf, F32)

    def blocks(kbs, diag):
        starts = [pl.multiple_of(kb * blk, blk) for kb in kbs]
        sc = []
        for start in starts:
            kr_blk = kr_ref[0, pl.ds(start, blk), :]
            sc.append([_dot_t(q_st[p], jnp.concatenate(
                [kn_ref[0, pl.ds(start, blk), p * LANES:(p + 1) * LANES], kr_blk], axis=1))
                for p in range(HEAD_PAIRS)])
        pr, alpha = [], []
        for sc_b in sc:
            pr_b, alpha_b = [], []
            for p in range(HEAD_PAIRS):
                s_p = jnp.where(visible, sc_b[p], -jnp.inf) if diag else sc_b[p]
                m_prev = m_ref[p]
                m_new = jnp.maximum(m_prev, jnp.max(s_p, axis=1, keepdims=True))
                alpha_b.append(jnp.exp2(m_prev - m_new))
                pr_b.append(jnp.exp2(s_p - jnp.concatenate([m_new] * (blk // LANES), axis=1)).astype(BF16))
                m_ref[p] = m_new
            pr.append(pr_b)
            alpha.append(alpha_b)
        for h in range(MLA_HEADS):
            p, half = divmod(h, 2)
            rows = slice(half * blk, (half + 1) * blk)
            a = acc_ref[h]
            for start, pr_b, alpha_b in zip(starts, pr, alpha):
                a = alpha_b[p][rows] * a + _dot(pr_b[p][rows], v_ref[0, pl.ds(start, blk), h * LANES:(h + 1) * LANES])
            acc_ref[h] = a

    blocks([qi], True)

    def body(j, _):
        blocks([2 * j, 2 * j + 1], False)
        return 0

    lax.fori_loop(0, qi // 2, body, 0)

    @pl.when(qi % 2 == 1)
    def _():
        blocks([qi - 1], False)

    for p in range(HEAD_PAIRS):
        a0 = acc_ref[2 * p]
        a1 = acc_ref[2 * p + 1]
        o0 = a0 / pltpu.roll(a0, MLA_V, axis=1)
        o1 = pltpu.roll(a1 / pltpu.roll(a1, MLA_V, axis=1), MLA_V, axis=1)
        o_ref[0, :, p * LANES:(p + 1) * LANES] = jnp.where(first, o0, o1).astype(o_ref.dtype)


def _mla_attention(qn, qr, kn, kr, vcat):
    b, s, wdt = qn.shape
    blk = ATTN_BLOCK
    qspec = pl.BlockSpec((1, blk, wdt), lambda bi, qi: (bi, qi, 0))
    full = lambda a: pl.BlockSpec((1, s, a.shape[-1]), lambda bi, qi: (bi, 0, 0))
    return pl.pallas_call(
        functools.partial(_mla_kernel, blk=blk),
        grid=(b, s // blk),
        in_specs=[qspec, qspec, full(kn), full(kr), full(vcat)],
        out_specs=qspec,
        out_shape=jax.ShapeDtypeStruct(qn.shape, BF16),
        scratch_shapes=[pltpu.VMEM((MLA_HEADS, blk, LANES), F32),
                        pltpu.VMEM((HEAD_PAIRS, 2 * blk, LANES), F32)],
        compiler_params=_params("parallel", "arbitrary"),
        name="mla_attention",
    )(qn, qr, kn, kr, vcat)


def _route(logits):
    lane = lax.broadcasted_iota(jnp.int32, logits.shape, 1).astype(F32)
    ninf = -jnp.inf
    big = float(ROUTER_LANES)
    is_g = lane < N_GROUPS
    lg = jnp.where(is_g, logits, ninf)
    gmax = jnp.max(lg, axis=1, keepdims=True)
    gsum = jnp.sum(jnp.where(is_g, jnp.exp(lg - gmax), 0.0), axis=1, keepdims=True)
    gp = 1.0 / gsum
    g = jnp.min(jnp.where(lg == gmax, lane, big), axis=1, keepdims=True)
    lo = N_GROUPS + EXPERTS_PER_GROUP * g
    in_grp = (lane >= lo) & (lane < lo + EXPERTS_PER_GROUP)
    le = jnp.where(in_grp, logits, ninf)
    l1 = jnp.max(le, axis=1, keepdims=True)
    i1 = jnp.min(jnp.where(le == l1, lane, big), axis=1, keepdims=True)
    le2 = jnp.where(lane == i1, ninf, le)
    l2 = jnp.max(le2, axis=1, keepdims=True)
    i2 = jnp.min(jnp.where(le2 == l2, lane, big), axis=1, keepdims=True)
    t = jnp.exp(l2 - l1)
    w1 = gp / (1.0 + t)
    w2 = gp * t / (1.0 + t)
    return i1 - N_GROUPS, i2 - N_GROUPS, w1, w2


def _pack_bf16_pairs(x):
    n = x.shape[1] // 2
    xb = x.astype(BF16).astype(F32)
    hi = pltpu.bitcast(xb[:, :n], jnp.uint32)
    lo = pltpu.bitcast(xb[:, n:], jnp.uint32)
    return hi | (lo >> 16)


def _unpack_bf16_pairs(w):
    hi = pltpu.bitcast(w & jnp.uint32(0xFFFF0000), F32)
    lo = pltpu.bitcast(w << 16, F32)
    return hi, lo


def _to_slabs(rows):
    return rows.reshape(rows.shape[0], PACK_SUBLANES, LANES)


def _from_slabs(slabs):
    return slabs.reshape(slabs.shape[0], PACK_SUBLANES * LANES)


R_E1, R_E2, R_W1, R_W2, R_RANK1, R_RANK2 = range(6)
META_ROWS = 8


def _mixer_out_kernel(h_ref, osb_ref, omla_ref, gosb_ref, gomla_ref, wout_ref, gmoe_ref,
                      wrh_ref, wrl_ref, br_ref, h1_ref, xp_ref, info_ref, meta_ref, cnt_ref):
    @pl.when(pl.program_id(0) == 0)
    def _():
        cnt_ref[...] = jnp.zeros_like(cnt_ref)

    tm = h_ref.shape[0]
    hm = tm // 2
    halves = [slice(0, hm), slice(hm, tm)]
    nsb = [_rms(osb_ref[hs, :].astype(F32), gosb_ref[...]).astype(BF16) for hs in halves]
    nmla = [_rms(omla_ref[hs, :].astype(F32), gomla_ref[...]).astype(BF16) for hs in halves]
    h1 = [h_ref[hs, :] + _dot(nsb[i], wout_ref[0:SB_WIDTH, :]) + _dot(nmla[i], wout_ref[SB_WIDTH:, :])
          for i, hs in enumerate(halves)]
    xn = []
    for i, hs in enumerate(halves):
        h1_ref[hs, :] = h1[i]
        xn.append(_rms(h1[i], gmoe_ref[...]))
        xp_ref[hs] = _to_slabs(_pack_bf16_pairs(xn[i]))
    x_hi = [x.astype(BF16) for x in xn]
    x_lo = [(x - xh.astype(F32)).astype(BF16) for x, xh in zip(xn, x_hi)]
    logits = [(_dot(x_hi[i], wrh_ref[...]) + _dot(x_hi[i], wrl_ref[...]) + _dot(x_lo[i], wrh_ref[...])) + br_ref[...]
              for i in range(2)]
    routes = [_route(lg) for lg in logits]

    lane = lax.broadcasted_iota(jnp.int32, logits[0].shape, 1).astype(F32)
    r = lax.broadcasted_iota(jnp.int32, (hm, hm), 0)
    c = lax.broadcasted_iota(jnp.int32, (hm, hm), 1)
    before = jnp.where(c < r, 1.0, 0.0).astype(BF16)
    onehot = [jnp.where((lane == e1) | (lane == e2), 1.0, 0.0) for e1, e2, _, _ in routes]
    prefix = [_dot(before, oh.astype(BF16)) for oh in onehot]
    base = cnt_ref[...]
    for i, hs in enumerate(halves):
        e1, e2, w1, w2 = routes[i]
        seen = prefix[i] + base
        rank1 = jnp.sum(jnp.where(lane == e1, seen, 0.0), axis=1, keepdims=True)
        rank2 = jnp.sum(jnp.where(lane == e2, seen, 0.0), axis=1, keepdims=True)
        base = base + jnp.sum(onehot[i], axis=0, keepdims=True)
        info = jnp.zeros_like(logits[i])
        for idx, val in ((R_E1, e1), (R_E2, e2), (R_W1, w1), (R_W2, w2), (R_RANK1, rank1), (R_RANK2, rank2)):
            info = jnp.where(lane == idx, val, info)
        info_ref[hs, :] = info
        meta_ref[:, hs] = info.T[0:META_ROWS, :]
    cnt_ref[...] = base


def _mixer_out(h, osb, omla, gosb, gomla, wout, gmoe, wrh, wrl, br):
    t = h.shape[0]
    tm = 2 * TOKEN_TILE
    row = lambda w: pl.BlockSpec((tm, w), lambda i: (i, 0))
    full = lambda a: pl.BlockSpec(a.shape, lambda i: (0,) * a.ndim)
    return pl.pallas_call(
        _mixer_out_kernel,
        grid=(t // tm,),
        in_specs=[row(D_MODEL), row(SB_WIDTH), row(MLA_WIDTH), full(gosb), full(gomla), full(wout),
                  full(gmoe), full(wrh), full(wrl), full(br)],
        out_specs=[row(D_MODEL), pl.BlockSpec((tm, PACK_SUBLANES, LANES), lambda i: (i, 0, 0)), row(ROUTER_LANES),
                   pl.BlockSpec((META_ROWS, tm), lambda i: (0, i)), pl.BlockSpec((1, ROUTER_LANES), lambda i: (0, 0))],
        out_shape=[jax.ShapeDtypeStruct((t, D_MODEL), F32),
                   jax.ShapeDtypeStruct((t, PACK_SUBLANES, LANES), jnp.uint32),
                   jax.ShapeDtypeStruct((t, ROUTER_LANES), F32), jax.ShapeDtypeStruct((META_ROWS, t), F32),
                   jax.ShapeDtypeStruct((1, ROUTER_LANES), F32)],
        compiler_params=_params("arbitrary"),
        name="mixer_out",
    )(h, osb, omla, gosb, gomla, wout, gmoe, wrh, wrl, br)


def _route_plan(meta, counts, n_rows):
    te = EXPERT_TILE
    cnt = counts[0, :N_EXPERTS].astype(jnp.int32)
    padded = (cnt + te - 1) // te * te
    seg_end = jnp.cumsum(padded)
    seg_start = seg_end - padded
    e = meta[R_E1:R_E2 + 1].astype(jnp.int32)
    rank = meta[R_RANK1:R_RANK2 + 1].astype(jnp.int32)
    ids = jnp.arange(N_EXPERTS, dtype=jnp.int32)
    pos = jnp.sum(jnp.where(e[..., None] == ids, seg_start, 0), axis=-1) + rank
    tiles = pos.shape[1] // TOKEN_TILE
    pos = pos.reshape(2, tiles, TOKEN_TILE).transpose(1, 0, 2).reshape(tiles, 1, 2 * TOKEN_TILE)
    tile_start = jnp.arange(n_rows // te, dtype=jnp.int32) * te
    tile_expert = jnp.minimum(jnp.sum(tile_start[:, None] >= seg_end[None, :], axis=1), N_EXPERTS - 1)
    n_valid = (seg_end[-1] // te).reshape(1)
    pad_start = jnp.concatenate([seg_start + cnt, seg_end[-1:]])
    pad_len = jnp.concatenate([padded - cnt, n_rows - seg_end[-1:]])
    return pos, tile_expert.astype(jnp.int32), n_valid.astype(jnp.int32), pad_start, pad_len


def _dispatch_kernel(pad_start_ref, pad_len_ref, pos_ref, x_ref, xs_hbm, zero, ring, sem, zsem):
    tm = x_ref.shape[0]

    @pl.when(pl.program_id(0) == 0)
    def _():
        zero[...] = jnp.zeros_like(zero)

        def each_range(copy_fn):
            def per_range(e, _):
                start = pad_start_ref[e]
                n = pad_len_ref[e]
                runs = n // ZERO_ROWS
                lax.fori_loop(0, runs, lambda j, c: copy_fn(
                    pltpu.make_async_copy(zero, xs_hbm.at[pl.ds(start + j * ZERO_ROWS, ZERO_ROWS)], zsem)), 0)
                lax.fori_loop(runs * ZERO_ROWS, n, lambda j, c: copy_fn(
                    pltpu.make_async_copy(zero.at[0], xs_hbm.at[start + j], zsem)), 0)
                return 0

            lax.fori_loop(0, pad_start_ref.shape[0], per_range, 0)

        def start_copy(cp):
            cp.start()
            return 0

        def wait_copy(cp):
            cp.wait()
            return 0

        each_range(start_copy)
        each_range(wait_copy)

    i = pl.program_id(0)
    n = pl.num_programs(0)
    slot = i % 2

    def drain(s):
        for _ in range(2):
            pltpu.make_async_copy(ring.at[s], xs_hbm.at[pl.ds(0, tm)], sem.at[s]).wait()

    @pl.when(i >= 2)
    def _():
        drain(slot)

    ring[slot] = x_ref[...]

    def issue(t, _):
        for k in range(2):
            pltpu.make_async_copy(ring.at[slot, t], xs_hbm.at[pos_ref[0, 0, k * tm + t]],
                                  sem.at[slot]).start(priority=k)
        return 0

    lax.fori_loop(0, tm, issue, 0, unroll=8)

    @pl.when((i == n - 1) & (n >= 2))
    def _():
        drain(1 - slot)

    @pl.when(i == n - 1)
    def _():
        drain(slot)


def _dispatch(pad_start, pad_len, pos, xp, n_rows):
    t = xp.shape[0]
    tm = TOKEN_TILE
    slab = xp.shape[1:]
    return pl.pallas_call(
        _dispatch_kernel,
        grid_spec=pltpu.PrefetchScalarGridSpec(
            num_scalar_prefetch=2,
            grid=(t // tm,),
            in_specs=[pl.BlockSpec((1, 1, 2 * tm), lambda i, ps, pn: (i, 0, 0), memory_space=pltpu.SMEM),
                      pl.BlockSpec((tm,) + slab, lambda i, ps, pn: (i, 0, 0))],
            out_specs=pl.BlockSpec(memory_space=pl.ANY),
            scratch_shapes=[pltpu.VMEM((ZERO_ROWS,) + slab, xp.dtype), pltpu.VMEM((2, tm) + slab, xp.dtype),
                            pltpu.SemaphoreType.DMA((2,)), pltpu.SemaphoreType.DMA(())],
        ),
        out_shape=jax.ShapeDtypeStruct((n_rows,) + slab, xp.dtype),
        compiler_params=_params("arbitrary"),
        name="moe_dispatch",
    )(pad_start, pad_len, pos, xp)


def _expert_kernel(te_ref, nv_ref, xs_ref, wg_ref, wu_ref, wd_ref, ys_ref, wg16, wu16, wd16):
    i = pl.program_id(0)
    live = i < nv_ref[0]

    @pl.when(live & ((i == 0) | (te_ref[i] != te_ref[jnp.maximum(i - 1, 0)])))
    def _():
        wg16[...] = wg_ref[0].astype(BF16)
        wu16[...] = wu_ref[0].astype(BF16)
        wd16[...] = wd_ref[0].astype(BF16)

    @pl.when(live)
    def _():
        half = D_MODEL // 2
        xa, xb = _unpack_bf16_pairs(_from_slabs(xs_ref[...]))
        xa = xa.astype(BF16)
        xb = xb.astype(BF16)
        a = _dot(xa, wg16[:half, :]) + _dot(xb, wg16[half:, :])
        u = _dot(xa, wu16[:half, :]) + _dot(xb, wu16[half:, :])
        hid = (a / (1.0 + jnp.exp(-a))) * u
        ys_ref[...] = _to_slabs(_pack_bf16_pairs(_dot(hid.astype(BF16), wd16[...])))

    @pl.when(i >= nv_ref[0])
    def _():
        ys_ref[...] = jnp.zeros_like(ys_ref)


def _experts(tile_expert, n_valid, xs, wg, wu, wd, layer):
    n_rows = xs.shape[0]
    te = EXPERT_TILE
    base = layer * N_EXPERTS
    last = lambda i, te_ref, nv_ref: jnp.minimum(i, nv_ref[0] - 1)
    slab = (te,) + xs.shape[1:]
    wspec = lambda shape: pl.BlockSpec(
        (1,) + shape, lambda i, te_ref, nv_ref: (base + te_ref[last(i, te_ref, nv_ref)], 0, 0))
    return pl.pallas_call(
        _expert_kernel,
        grid_spec=pltpu.PrefetchScalarGridSpec(
            num_scalar_prefetch=2,
            grid=(n_rows // te,),
            in_specs=[pl.BlockSpec(slab, lambda i, te_ref, nv_ref: (last(i, te_ref, nv_ref), 0, 0)),
                      wspec((D_MODEL, D_EXPERT)), wspec((D_MODEL, D_EXPERT)), wspec((D_EXPERT, D_MODEL))],
            out_specs=pl.BlockSpec(slab, lambda i, te_ref, nv_ref: (i, 0, 0)),
            scratch_shapes=[pltpu.VMEM((D_MODEL, D_EXPERT), BF16), pltpu.VMEM((D_MODEL, D_EXPERT), BF16),
                            pltpu.VMEM((D_EXPERT, D_MODEL), BF16)],
        ),
        out_shape=jax.ShapeDtypeStruct(xs.shape, xs.dtype),
        compiler_params=_params("arbitrary"),
        name="moe_experts",
    )(tile_expert, n_valid, xs, wg, wu, wd)


def _combine_ple_kernel(pos_ref, pos_next_ref, info_ref, h_ref, p_ref, g_ref, wpg_ref, wpe_ref, gf_ref, ys_hbm,
                        o_ref, buf, sem, *, final):
    i = pl.program_id(0)
    tm = h_ref.shape[0]
    slot = i % 2

    def gather(rows_ref, dst):
        def issue(t, _):
            for k in range(2):
                pltpu.make_async_copy(ys_hbm.at[rows_ref[0, 0, k * tm + t]],
                                      buf.at[dst, k * tm + t], sem.at[dst]).start(priority=k)
            return 0

        lax.fori_loop(0, tm, issue, 0, unroll=8)

    @pl.when(i == 0)
    def _():
        gather(pos_ref, 0)

    @pl.when(i + 1 < pl.num_programs(0))
    def _():
        gather(pos_next_ref, 1 - slot)

    info = info_ref[...]
    lane = lax.broadcasted_iota(jnp.int32, info.shape, 1)
    gate1 = jnp.sum(jnp.where(lane == R_W1, info, 0.0), axis=1, keepdims=True)
    gate2 = jnp.sum(jnp.where(lane == R_W2, info, 0.0), axis=1, keepdims=True)
    pltpu.make_async_copy(ys_hbm.at[pl.ds(0, 2 * tm)], buf.at[slot], sem.at[slot]).wait()
    y1a, y1b = _unpack_bf16_pairs(_from_slabs(buf[slot, 0:tm]))
    y2a, y2b = _unpack_bf16_pairs(_from_slabs(buf[slot, tm:2 * tm]))
    y = jnp.concatenate([gate1 * y1a + gate2 * y2a, gate1 * y1b + gate2 * y2b], axis=1)
    h = h_ref[...] + y
    xn = _rms(h, g_ref[...]).astype(BF16)
    gate = 1.0 / (1.0 + jnp.exp(-_dot(xn, wpg_ref[...])))
    out = h + gate * _dot(p_ref[0].astype(BF16), wpe_ref[...])
    if final:
        out = _rms(out, gf_ref[...])
    o_ref[...] = out


def _combine_ple(pos, info, h, p, layer, g, wpg, wpe, gf, ys, final):
    t = h.shape[0]
    tm = TOKEN_TILE
    n = t // tm
    row = lambda w: pl.BlockSpec((tm, w), lambda i: (i, 0))
    full = lambda a: pl.BlockSpec(a.shape, lambda i: (0,) * a.ndim)
    return pl.pallas_call(
        functools.partial(_combine_ple_kernel, final=final),
        grid=(n,),
        in_specs=[pl.BlockSpec((1, 1, 2 * tm), lambda i: (i, 0, 0), memory_space=pltpu.SMEM),
                  pl.BlockSpec((1, 1, 2 * tm), lambda i: (jnp.minimum(i + 1, n - 1), 0, 0), memory_space=pltpu.SMEM),
                  row(ROUTER_LANES), row(D_MODEL), pl.BlockSpec((1, tm, D_PLE), lambda i: (layer, i, 0)),
                  full(g), full(wpg), full(wpe), full(gf),
                  pl.BlockSpec(memory_space=pl.ANY)],
        out_specs=row(D_MODEL),
        out_shape=jax.ShapeDtypeStruct((t, D_MODEL), F32),
        scratch_shapes=[pltpu.VMEM((2, 2 * tm) + ys.shape[1:], ys.dtype), pltpu.SemaphoreType.DMA((2,))],
        compiler_params=_params("arbitrary"),
        name="moe_combine_ple",
    )(pos, pos, info, h, p, g, wpg, wpe, gf, ys)


def _rot_cols(w):
    half = w.shape[-1] // 2
    return jnp.concatenate([-w[:, half:], w[:, :half]], axis=1)


def _prep_in(w_in):
    sb_scale = SB_HEAD_DIM ** -0.5 * LOG2_E
    kr = w_in[:, 3 * SB_WIDTH + MLA_Q_RANK + MLA_KV_RANK:]
    pad = jnp.zeros((w_in.shape[0], LANES - 2 * MLA_ROPE), w_in.dtype)
    kra = jnp.concatenate([kr, kr, pad], axis=1)
    krr = _rot_cols(kr)
    krb = jnp.concatenate([krr, krr, pad], axis=1)
    w1 = jnp.concatenate([w_in[:, :SB_WIDTH] * sb_scale,
                          w_in[:, SB_WIDTH:3 * SB_WIDTH + MLA_Q_RANK + MLA_KV_RANK], kra, krb], axis=1)
    return w1.astype(BF16)


def _prep_uq(w_uq):
    scale = MLA_QK ** -0.5 * LOG2_E
    w = w_uq.reshape(MLA_Q_RANK, MLA_HEADS, MLA_QK) * scale
    wqn = w[:, :, :MLA_NOPE].reshape(MLA_Q_RANK, MLA_HEADS * MLA_NOPE)
    rope = w[:, :, MLA_NOPE:]
    half = MLA_ROPE // 2
    rot = jnp.concatenate([-rope[:, :, half:], rope[:, :, :half]], axis=2)
    pad = jnp.zeros((MLA_Q_RANK, HEAD_PAIRS, LANES - 2 * MLA_ROPE), w.dtype)

    def pairs(r):
        return jnp.concatenate([r.reshape(MLA_Q_RANK, HEAD_PAIRS, 2 * MLA_ROPE), pad], axis=2).reshape(
            MLA_Q_RANK, HEAD_PAIRS * LANES)

    return wqn.astype(BF16), pairs(rope).astype(BF16), pairs(rot).astype(BF16)


def _prep_ukv(w_ukv):
    w = w_ukv.reshape(MLA_KV_RANK, MLA_HEADS, MLA_NOPE + MLA_V)
    wk = w[:, :, :MLA_NOPE].reshape(MLA_KV_RANK, MLA_HEADS * MLA_NOPE)
    pad = jnp.zeros((MLA_KV_RANK, MLA_HEADS, LANES - MLA_V), w.dtype)
    wv = jnp.concatenate([w[:, :, MLA_NOPE:], pad], axis=2).reshape(MLA_KV_RANK, MLA_HEADS * LANES)
    return wk.astype(BF16), wv.astype(BF16)


def _prep_router(w_rg, b_rg, w_re, b_re):
    pad = ROUTER_LANES - N_GROUPS - N_EXPERTS
    wr = jnp.concatenate([w_rg, w_re, jnp.zeros((D_MODEL, pad), F32)], axis=1)
    br = jnp.concatenate([b_rg, b_re, jnp.zeros((pad,), F32)])[None, :]
    wr_hi = wr.astype(BF16)
    wr_lo = (wr - wr_hi.astype(F32)).astype(BF16)
    return wr_hi, wr_lo, br


def kernel(x, p, positions, g_mix, w_in, g_cq, w_uq, g_ckv, w_ukv, g_osb, g_omla, w_out, g_moe,
           w_rg, b_rg, w_re, b_re, w_gate, w_up, w_down, g_ple, w_pg, w_pe, g_final):
    b, s, d = x.shape
    t = b * s
    depth = w_in.shape[0]

    inv_freq = ROPE_THETA ** (-jnp.arange(0, MLA_ROPE, 2, dtype=F32) / MLA_ROPE)
    ang = positions.astype(F32)[..., None] * inv_freq
    reps = LANES // (MLA_ROPE // 2)
    cos = jnp.tile(jnp.cos(ang), (1, 1, reps)).reshape(t, LANES)
    sin = jnp.tile(jnp.sin(ang), (1, 1, reps)).reshape(t, LANES)

    h = x.reshape(t, d)
    n_rows = -(-(2 * t + N_EXPERTS * (EXPERT_TILE - 1)) // EXPERT_TILE) * EXPERT_TILE
    p_all = p.reshape(depth, t, D_PLE)
    wg_all = w_gate.reshape(depth * N_EXPERTS, D_MODEL, D_EXPERT)
    wu_all = w_up.reshape(depth * N_EXPERTS, D_MODEL, D_EXPERT)
    wd_all = w_down.reshape(depth * N_EXPERTS, D_EXPERT, D_MODEL)
    r3 = lambda a: a.reshape(b, s, a.shape[-1])
    r2 = lambda a: a.reshape(t, a.shape[-1])
    for i in range(depth):
        w1 = _prep_in(w_in[i])
        wqn, wqra, wqrb = _prep_uq(w_uq[i])
        wkvk, wkvv = _prep_ukv(w_ukv[i])
        qsb, ksb, vsb, qn, qr, kn, kr, vm = _mixer_in(
            h, g_mix[i][None], cos, sin, g_cq[i][None], g_ckv[i][None], w1, wqn, wqra, wqrb, wkvk, wkvv)
        osb = _sb_attention(r3(qsb), r3(ksb), r3(vsb))
        omla = _mla_attention(r3(qn), r3(qr), r3(kn), r3(kr), r3(vm))
        wrh, wrl, br = _prep_router(w_rg[i], b_rg[i], w_re[i], b_re[i])
        h1, xp, info, meta, counts = _mixer_out(h, r2(osb), r2(omla), g_osb[i][None], g_omla[i][None],
                                          w_out[i].astype(BF16), g_moe[i][None], wrh, wrl, br)
        pos, tile_expert, n_valid, pad_start, pad_len = _route_plan(meta, counts, n_rows)
        xs = _dispatch(pad_start, pad_len, pos, xp, n_rows)
        ys = _experts(tile_expert, n_valid, xs, wg_all, wu_all, wd_all, i)
        h = _combine_ple(pos, info, h1, p_all, i, g_ple[i][None], w_pg[i].astype(BF16),
                         w_pe[i].astype(BF16), g_final[None], ys, final=(i == depth - 1))
    return h.reshape(b, s, d)
```

```python
import functools

import jax
import jax.numpy as jnp
from jax import lax
from jax.experimental import pallas as pl
from jax.experimental.pallas import tpu as pltpu

F32 = jnp.float32
BF16 = jnp.bfloat16

D_MODEL = 1024
CHUNK = 64
D_PLE = 256
EPS = 1e-6
SB_HEADS = 8
SB_HEAD_DIM = 64
SB_WIDTH = SB_HEADS * SB_HEAD_DIM
MLA_HEADS = 8
MLA_NOPE = 64
MLA_ROPE = 32
MLA_QK = MLA_NOPE + MLA_ROPE
MLA_V = 64
MLA_Q_RANK = 384
MLA_KV_RANK = 256
MLA_WIDTH = MLA_HEADS * MLA_V
ROPE_THETA = 10000.0
N_GROUPS = 4
EXPERTS_PER_GROUP = 8
N_EXPERTS = N_GROUPS * EXPERTS_PER_GROUP
D_EXPERT = 256

LANES = 128
HEAD_PAIRS = SB_HEADS // 2
ROUTER_LANES = LANES
TOKEN_TILE = 512
EXPERT_TILE = 512
PACK_SUBLANES = D_MODEL // 2 // LANES
ZERO_ROWS = 64
ATTN_BLOCK = 256
LOG2_E = 1.4426950408889634
SB_UNDERFLOW = 151.0
VMEM_LIMIT = 48 * 1024 * 1024


def _params(*sem):
    return pltpu.CompilerParams(dimension_semantics=sem, vmem_limit_bytes=VMEM_LIMIT)


def _rms(x, g):
    return x * lax.rsqrt(jnp.mean(x * x, axis=-1, keepdims=True) + EPS) * g


def _dot(a, b):
    return jnp.dot(a, b, preferred_element_type=F32)


def _dot_t(a, b):
    return lax.dot_general(a, b, (((1,), (1,)), ((), ())), preferred_element_type=F32)


def _mixer_in_kernel(x_ref, g_ref, cos_ref, sin_ref, gcq_ref, gckv_ref, w1_ref, wqn_ref,
                     wqra_ref, wqrb_ref, wkvk_ref, wkvv_ref,
                     qsb_ref, ksb_ref, vsb_ref, qn_ref, qr_ref, kn_ref, kr_ref, vm_ref):
    xn = _rms(x_ref[...], g_ref[...]).astype(BF16)
    c0 = 0
    qsb_ref[...] = _dot(xn, w1_ref[:, c0:c0 + SB_WIDTH]).astype(BF16)
    c0 += SB_WIDTH
    ksb_ref[...] = _dot(xn, w1_ref[:, c0:c0 + SB_WIDTH]).astype(BF16)
    c0 += SB_WIDTH
    vsb_ref[...] = _dot(xn, w1_ref[:, c0:c0 + SB_WIDTH]).astype(BF16)
    c0 += SB_WIDTH
    cq = _dot(xn, w1_ref[:, c0:c0 + MLA_Q_RANK])
    c0 += MLA_Q_RANK
    ckv = _dot(xn, w1_ref[:, c0:c0 + MLA_KV_RANK])
    c0 += MLA_KV_RANK
    kra = _dot(xn, w1_ref[:, c0:c0 + LANES])
    c0 += LANES
    krb = _dot(xn, w1_ref[:, c0:c0 + LANES])
    cos = cos_ref[...]
    sin = sin_ref[...]
    kr_ref[...] = (kra * cos + krb * sin).astype(BF16)

    cqn = _rms(cq, gcq_ref[...]).astype(BF16)
    qn_ref[...] = _dot(cqn, wqn_ref[...]).astype(BF16)
    ra = _dot(cqn, wqra_ref[...])
    rb = _dot(cqn, wqrb_ref[...])
    for p in range(HEAD_PAIRS):
        sl = slice(p * LANES, (p + 1) * LANES)
        qr_ref[:, sl] = (ra[:, sl] * cos + rb[:, sl] * sin).astype(BF16)

    ckvn = _rms(ckv, gckv_ref[...]).astype(BF16)
    kn_ref[...] = _dot(ckvn, wkvk_ref[...]).astype(BF16)
    lane = lax.broadcasted_iota(jnp.int32, (1, LANES), 1)
    for hd in range(MLA_HEADS):
        sl = slice(hd * LANES, (hd + 1) * LANES)
        vm_ref[:, sl] = jnp.where(lane < MLA_V, _dot(ckvn, wkvv_ref[:, sl]), 1.0).astype(BF16)


def _mixer_in(h, g, cos, sin, gcq, gckv, w1, wqn, wqra, wqrb, wkvk, wkvv):
    t = h.shape[0]
    tm = TOKEN_TILE
    row = lambda w: pl.BlockSpec((tm, w), lambda i: (i, 0))
    full = lambda a: pl.BlockSpec(a.shape, lambda i: (0,) * a.ndim)
    widths = [SB_WIDTH, SB_WIDTH, SB_WIDTH, MLA_WIDTH, MLA_WIDTH, MLA_WIDTH, LANES, MLA_HEADS * LANES]
    return pl.pallas_call(
        _mixer_in_kernel,
        grid=(t // tm,),
        in_specs=[row(D_MODEL), full(g), row(LANES), row(LANES), full(gcq), full(gckv),
                  full(w1), full(wqn), full(wqra), full(wqrb), full(wkvk), full(wkvv)],
        out_specs=[row(w) for w in widths],
        out_shape=[jax.ShapeDtypeStruct((t, w), BF16) for w in widths],
        compiler_params=_params("parallel"),
        name="mixer_in",
    )(h, g, cos, sin, gcq, gckv, w1, wqn, wqra, wqrb, wkvk, wkvv)


def _sb_kernel(q_ref, k_ref, v_ref, o_ref, acc_ref, c_ref, *, blk):
    qi = pl.program_id(1)
    lane = lax.broadcasted_iota(jnp.int32, (1, LANES), 1)
    first = lane < SB_HEAD_DIM
    q_st = []
    for p in range(HEAD_PAIRS):
        q2 = q_ref[0, :, p * LANES:(p + 1) * LANES]
        zero = jnp.zeros_like(q2)
        q_st.append(jnp.concatenate([jnp.where(first, q2, zero), jnp.where(first, zero, q2)], axis=0))
    r = lax.broadcasted_iota(jnp.int32, (blk, blk), 0)
    c = lax.broadcasted_iota(jnp.int32, (blk, blk), 1)
    tri = jnp.where(r >= c, 1.0, 0.0).astype(BF16)
    r2 = lax.broadcasted_iota(jnp.int32, (2 * blk, blk), 0)
    c2 = lax.broadcasted_iota(jnp.int32, (2 * blk, blk), 1)
    causal = c2 < jnp.where(r2 >= blk, r2 - blk, r2)
    sign_bit = jnp.uint32(0x80000000)

    acc_ref[...] = jnp.zeros_like(acc_ref)
    c_ref[...] = jnp.zeros_like(c_ref)

    def block(kb, diag):
        start = pl.multiple_of(kb * blk, blk)
        cols = lambda ref, p: ref[0, pl.ds(start, blk), p * LANES:(p + 1) * LANES]
        z = [_dot_t(q_st[p], cols(k_ref, p)) for p in range(HEAD_PAIRS)]
        sp16 = []
        for p in range(HEAD_PAIRS):
            neg_abs = pltpu.bitcast(pltpu.bitcast(z[p], jnp.uint32) | sign_bit, F32)
            sp = jnp.maximum(z[p], 0.0) + jnp.log(1.0 + jnp.exp2(neg_abs)) * LOG2_E
            if diag:
                sp = jnp.where(causal, sp, 0.0)
            sp16.append(sp.astype(BF16))
        cl = [_dot(sp16[p], tri) for p in range(HEAD_PAIRS)]
        w = []
        for p in range(HEAD_PAIRS):
            wp = jnp.exp2(z[p] - cl[p])
            if diag:
                wp = jnp.where(causal, wp, 0.0)
            w.append(wp.astype(BF16))
        pv = [_dot(w[p], cols(v_ref, p)) for p in range(HEAD_PAIRS)]
        for p in range(HEAD_PAIRS):
            carry = c_ref[p]
            acc_ref[p] += jnp.exp2(-carry) * pv[p]
            c_ref[p] = carry + jnp.broadcast_to(cl[p][:, 0:1], carry.shape)

    def min_carry():
        return jnp.min(jnp.min(c_ref[...], axis=0))

    block(qi, True)

    def cond(st):
        return (st[0] < qi) & (st[1] < SB_UNDERFLOW)

    def body(st):
        block(qi - 1 - st[0], False)
        return st[0] + 1, min_carry()

    lax.while_loop(cond, body, (jnp.int32(0), min_carry()))
    for p in range(HEAD_PAIRS):
        a = acc_ref[p]
        o_ref[0, :, p * LANES:(p + 1) * LANES] = jnp.where(first, a[:blk], a[blk:]).astype(o_ref.dtype)


def _sb_attention(q, k, v):
    b, s, wdt = q.shape
    blk = ATTN_BLOCK
    qspec = pl.BlockSpec((1, blk, wdt), lambda bi, qi: (bi, qi, 0))
    kspec = pl.BlockSpec((1, s, wdt), lambda bi, qi: (bi, 0, 0))
    state = pltpu.VMEM((HEAD_PAIRS, 2 * blk, LANES), F32)
    return pl.pallas_call(
        functools.partial(_sb_kernel, blk=blk),
        grid=(b, s // blk),
        in_specs=[qspec, kspec, kspec],
        out_specs=qspec,
        out_shape=jax.ShapeDtypeStruct(q.shape, BF16),
        scratch_shapes=[state, state],
        compiler_params=_params("parallel", "arbitrary"),
        name="sb_attention",
    )(q, k, v)


def _mla_kernel(qn_ref, qr_ref, kn_ref, kr_ref, v_ref, o_ref, acc_ref, m_ref, *, blk):
    qi = pl.program_id(1)
    lane2 = lax.broadcasted_iota(jnp.int32, (1, 2 * LANES), 1)
    sel0 = (lane2 < MLA_NOPE) | ((lane2 >= LANES) & (lane2 < LANES + MLA_ROPE))
    sel1 = ((lane2 >= MLA_NOPE) & (lane2 < LANES)) | (
        (lane2 >= LANES + MLA_ROPE) & (lane2 < LANES + 2 * MLA_ROPE))
    q_st = []
    for p in range(HEAD_PAIRS):
        sl = slice(p * LANES, (p + 1) * LANES)
        qcat = jnp.concatenate([qn_ref[0, :, sl], qr_ref[0, :, sl]], axis=1)
        zero = jnp.zeros_like(qcat)
        q_st.append(jnp.concatenate([jnp.where(sel0, qcat, zero), jnp.where(sel1, qcat, zero)], axis=0))
    r2 = lax.broadcasted_iota(jnp.int32, (2 * blk, blk), 0)
    c2 = lax.broadcasted_iota(jnp.int32, (2 * blk, blk), 1)
    visible = (c2 // CHUNK) <= (jnp.where(r2 >= blk, r2 - blk, r2) // CHUNK)
    lane = lax.broadcasted_iota(jnp.int32, (1, LANES), 1)
    first = lane < MLA_V

    acc_ref[...] = jnp.zeros_like(acc_ref)
    m_ref[...] = jnp.full(m_ref.shape, -jnp.inf, F32)

    def blocks(kbs, diag):
        starts = [pl.multiple_of(kb * blk, blk) for kb in kbs]
        sc = []
        for start in starts:
            kr_blk = kr_ref[0, pl.ds(start, blk), :]
            sc.append([_dot_t(q_st[p], jnp.concatenate(
                [kn_ref[0, pl.ds(start, blk), p * LANES:(p + 1) * LANES], kr_blk], axis=1))
                for p in range(HEAD_PAIRS)])
        pr, alpha = [], []
        for sc_b in sc:
            pr_b, alpha_b = [], []
            for p in range(HEAD_PAIRS):
                s_p = jnp.where(visible, sc_b[p], -jnp.inf) if diag else sc_b[p]
                m_prev = m_ref[p]
                m_new = jnp.maximum(m_prev, jnp.max(s_p, axis=1, keepdims=True))
                alpha_b.append(jnp.exp2(m_prev - m_new))
                pr_b.append(jnp.exp2(s_p - jnp.concatenate([m_new] * (blk // LANES), axis=1)).astype(BF16))
                m_ref[p] = m_new
            pr.append(pr_b)
            alpha.append(alpha_b)
        for h in range(MLA_HEADS):
            p, half = divmod(h, 2)
            rows = slice(half * blk, (half + 1) * blk)
            a = acc_ref[h]
            for start, pr_b, alpha_b in zip(starts, pr, alpha):
                a = alpha_b[p][rows] * a + _dot(pr_b[p][rows], v_ref[0, pl.ds(start, blk), h * LANES:(h + 1) * LANES])
            acc_ref[h] = a

    blocks([qi], True)

    def body(j, _):
        blocks([2 * j, 2 * j + 1], False)
        return 0

    lax.fori_loop(0, qi // 2, body, 0)

    @pl.when(qi % 2 == 1)
    def _():
        blocks([qi - 1], False)

    for p in range(HEAD_PAIRS):
        a0 = acc_ref[2 * p]
        a1 = acc_ref[2 * p + 1]
        o0 = a0 / pltpu.roll(a0, MLA_V, axis=1)
        o1 = pltpu.roll(a1 / pltpu.roll(a1, MLA_V, axis=1), MLA_V, axis=1)
        o_ref[0, :, p * LANES:(p + 1) * LANES] = jnp.where(first, o0, o1).astype(o_ref.dtype)


def _mla_attention(qn, qr, kn, kr, vcat):
    b, s, wdt = qn.shape
    blk = ATTN_BLOCK
    qspec = pl.BlockSpec((1, blk, wdt), lambda bi, qi: (bi, qi, 0))
    full = lambda a: pl.BlockSpec((1, s, a.shape[-1]), lambda bi, qi: (bi, 0, 0))
    return pl.pallas_call(
        functools.partial(_mla_kernel, blk=blk),
        grid=(b, s // blk),
        in_specs=[qspec, qspec, full(kn), full(kr), full(vcat)],
        out_specs=qspec,
        out_shape=jax.ShapeDtypeStruct(qn.shape, BF16),
        scratch_shapes=[pltpu.VMEM((MLA_HEADS, blk, LANES), F32),
                        pltpu.VMEM((HEAD_PAIRS, 2 * blk, LANES), F32)],
        compiler_params=_params("parallel", "arbitrary"),
        name="mla_attention",
    )(qn, qr, kn, kr, vcat)


def _route(logits):
    lane = lax.broadcasted_iota(jnp.int32, logits.shape, 1).astype(F32)
    ninf = -jnp.inf
    big = float(ROUTER_LANES)
    is_g = lane < N_GROUPS
    lg = jnp.where(is_g, logits, ninf)
    gmax = jnp.max(lg, axis=1, keepdims=True)
    gsum = jnp.sum(jnp.where(is_g, jnp.exp(lg - gmax), 0.0), axis=1, keepdims=True)
    gp = 1.0 / gsum
    g = jnp.min(jnp.where(lg == gmax, lane, big), axis=1, keepdims=True)
    lo = N_GROUPS + EXPERTS_PER_GROUP * g
    in_grp = (lane >= lo) & (lane < lo + EXPERTS_PER_GROUP)
    le = jnp.where(in_grp, logits, ninf)
    l1 = jnp.max(le, axis=1, keepdims=True)
    i1 = jnp.min(jnp.where(le == l1, lane, big), axis=1, keepdims=True)
    le2 = jnp.where(lane == i1, ninf, le)
    l2 = jnp.max(le2, axis=1, keepdims=True)
    i2 = jnp.min(jnp.where(le2 == l2, lane, big), axis=1, keepdims=True)
    t = jnp.exp(l2 - l1)
    w1 = gp / (1.0 + t)
    w2 = gp * t / (1.0 + t)
    return i1 - N_GROUPS, i2 - N_GROUPS, w1, w2


def _pack_bf16_pairs(x):
    n = x.shape[1] // 2
    xb = x.astype(BF16).astype(F32)
    hi = pltpu.bitcast(xb[:, :n], jnp.uint32)
    lo = pltpu.bitcast(xb[:, n:], jnp.uint32)
    return hi | (lo >> 16)


def _unpack_bf16_pairs(w):
    hi = pltpu.bitcast(w & jnp.uint32(0xFFFF0000), F32)
    lo = pltpu.bitcast(w << 16, F32)
    return hi, lo


def _to_slabs(rows):
    return rows.reshape(rows.shape[0], PACK_SUBLANES, LANES)


def _from_slabs(slabs):
    return slabs.reshape(slabs.shape[0], PACK_SUBLANES * LANES)


R_E1, R_E2, R_W1, R_W2, R_RANK1, R_RANK2 = range(6)
META_ROWS = 8


def _mixer_out_kernel(h_ref, osb_ref, omla_ref, gosb_ref, gomla_ref, wout_ref, gmoe_ref,
                      wrh_ref, wrl_ref, br_ref, h1_ref, xp_ref, info_ref, meta_ref, cnt_ref):
    @pl.when(pl.program_id(0) == 0)
    def _():
        cnt_ref[...] = jnp.zeros_like(cnt_ref)

    tm = h_ref.shape[0]
    hm = tm // 2
    halves = [slice(0, hm), slice(hm, tm)]
    nsb = [_rms(osb_ref[hs, :].astype(F32), gosb_ref[...]).astype(BF16) for hs in halves]
    nmla = [_rms(omla_ref[hs, :].astype(F32), gomla_ref[...]).astype(BF16) for hs in halves]
    h1 = [h_ref[hs, :] + _dot(nsb[i], wout_ref[0:SB_WIDTH, :]) + _dot(nmla[i], wout_ref[SB_WIDTH:, :])
          for i, hs in enumerate(halves)]
    xn = []
    for i, hs in enumerate(halves):
        h1_ref[hs, :] = h1[i]
        xn.append(_rms(h1[i], gmoe_ref[...]))
        xp_ref[hs] = _to_slabs(_pack_bf16_pairs(xn[i]))
    x_hi = [x.astype(BF16) for x in xn]
    x_lo = [(x - xh.astype(F32)).astype(BF16) for x, xh in zip(xn, x_hi)]
    logits = [(_dot(x_hi[i], wrh_ref[...]) + _dot(x_hi[i], wrl_ref[...]) + _dot(x_lo[i], wrh_ref[...])) + br_ref[...]
              for i in range(2)]
    routes = [_route(lg) for lg in logits]

    lane = lax.broadcasted_iota(jnp.int32, logits[0].shape, 1).astype(F32)
    r = lax.broadcasted_iota(jnp.int32, (hm, hm), 0)
    c = lax.broadcasted_iota(jnp.int32, (hm, hm), 1)
    before = jnp.where(c < r, 1.0, 0.0).astype(BF16)
    onehot = [jnp.where((lane == e1) | (lane == e2), 1.0, 0.0) for e1, e2, _, _ in routes]
    prefix = [_dot(before, oh.astype(BF16)) for oh in onehot]
    base = cnt_ref[...]
    for i, hs in enumerate(halves):
        e1, e2, w1, w2 = routes[i]
        seen = prefix[i] + base
        rank1 = jnp.sum(jnp.where(lane == e1, seen, 0.0), axis=1, keepdims=True)
        rank2 = jnp.sum(jnp.where(lane == e2, seen, 0.0), axis=1, keepdims=True)
        base = base + jnp.sum(onehot[i], axis=0, keepdims=True)
        info = jnp.zeros_like(logits[i])
        for idx, val in ((R_E1, e1), (R_E2, e2), (R_W1, w1), (R_W2, w2), (R_RANK1, rank1), (R_RANK2, rank2)):
            info = jnp.where(lane == idx, val, info)
        info_ref[hs, :] = info
        meta_ref[:, hs] = info.T[0:META_ROWS, :]
    cnt_ref[...] = base


def _mixer_out(h, osb, omla, gosb, gomla, wout, gmoe, wrh, wrl, br):
    t = h.shape[0]
    tm = 2 * TOKEN_TILE
    row = lambda w: pl.BlockSpec((tm, w), lambda i: (i, 0))
    full = lambda a: pl.BlockSpec(a.shape, lambda i: (0,) * a.ndim)
    return pl.pallas_call(
        _mixer_out_kernel,
        grid=(t // tm,),
        in_specs=[row(D_MODEL), row(SB_WIDTH), row(MLA_WIDTH), full(gosb), full(gomla), full(wout),
                  full(gmoe), full(wrh), full(wrl), full(br)],
        out_specs=[row(D_MODEL), pl.BlockSpec((tm, PACK_SUBLANES, LANES), lambda i: (i, 0, 0)), row(ROUTER_LANES),
                   pl.BlockSpec((META_ROWS, tm), lambda i: (0, i)), pl.BlockSpec((1, ROUTER_LANES), lambda i: (0, 0))],
        out_shape=[jax.ShapeDtypeStruct((t, D_MODEL), F32),
                   jax.ShapeDtypeStruct((t, PACK_SUBLANES, LANES), jnp.uint32),
                   jax.ShapeDtypeStruct((t, ROUTER_LANES), F32), jax.ShapeDtypeStruct((META_ROWS, t), F32),
                   jax.ShapeDtypeStruct((1, ROUTER_LANES), F32)],
        compiler_params=_params("arbitrary"),
        name="mixer_out",
    )(h, osb, omla, gosb, gomla, wout, gmoe, wrh, wrl, br)


def _route_plan(meta, counts, n_rows):
    te = EXPERT_TILE
    cnt = counts[0, :N_EXPERTS].astype(jnp.int32)
    padded = (cnt + te - 1) // te * te
    seg_end = jnp.cumsum(padded)
    seg_start = seg_end - padded
    e = meta[R_E1:R_E2 + 1].astype(jnp.int32)
    rank = meta[R_RANK1:R_RANK2 + 1].astype(jnp.int32)
    ids = jnp.arange(N_EXPERTS, dtype=jnp.int32)
    pos = jnp.sum(jnp.where(e[..., None] == ids, seg_start, 0), axis=-1) + rank
    tiles = pos.shape[1] // TOKEN_TILE
    pos = pos.reshape(2, tiles, TOKEN_TILE).transpose(1, 0, 2).reshape(tiles, 1, 2 * TOKEN_TILE)
    tile_start = jnp.arange(n_rows // te, dtype=jnp.int32) * te
    tile_expert = jnp.minimum(jnp.sum(tile_start[:, None] >= seg_end[None, :], axis=1), N_EXPERTS - 1)
    n_valid = (seg_end[-1] // te).reshape(1)
    pad_start = jnp.concatenate([seg_start + cnt, seg_end[-1:]])
    pad_len = jnp.concatenate([padded - cnt, n_rows - seg_end[-1:]])
    return pos, tile_expert.astype(jnp.int32), n_valid.astype(jnp.int32), pad_start, pad_len


def _dispatch_kernel(pad_start_ref, pad_len_ref, pos_ref, x_ref, xs_hbm, zero, ring, sem, zsem):
    tm = x_ref.shape[0]

    @pl.when(pl.program_id(0) == 0)
    def _():
        zero[...] = jnp.zeros_like(zero)

        def each_range(copy_fn):
            def per_range(e, _):
                start = pad_start_ref[e]
                n = pad_len_ref[e]
                runs = n // ZERO_ROWS
                lax.fori_loop(0, runs, lambda j, c: copy_fn(
                    pltpu.make_async_copy(zero, xs_hbm.at[pl.ds(start + j * ZERO_ROWS, ZERO_ROWS)], zsem)), 0)
                lax.fori_loop(runs * ZERO_ROWS, n, lambda j, c: copy_fn(
                    pltpu.make_async_copy(zero.at[0], xs_hbm.at[start + j], zsem)), 0)
                return 0

            lax.fori_loop(0, pad_start_ref.shape[0], per_range, 0)

        def start_copy(cp):
            cp.start()
            return 0

        def wait_copy(cp):
            cp.wait()
            return 0

        each_range(start_copy)
        each_range(wait_copy)

    i = pl.program_id(0)
    n = pl.num_programs(0)
    slot = i % 2

    def drain(s):
        for _ in range(2):
            pltpu.make_async_copy(ring.at[s], xs_hbm.at[pl.ds(0, tm)], sem.at[s]).wait()

    @pl.when(i >= 2)
    def _():
        drain(slot)

    ring[slot] = x_ref[...]

    def issue(t, _):
        for k in range(2):
            pltpu.make_async_copy(ring.at[slot, t], xs_hbm.at[pos_ref[0, 0, k * tm + t]],
                                  sem.at[slot]).start(priority=k)
        return 0

    lax.fori_loop(0, tm, issue, 0, unroll=8)

    @pl.when((i == n - 1) & (n >= 2))
    def _():
        drain(1 - slot)

    @pl.when(i == n - 1)
    def _():
        drain(slot)


def _dispatch(pad_start, pad_len, pos, xp, n_rows):
    t = xp.shape[0]
    tm = TOKEN_TILE
    slab = xp.shape[1:]
    return pl.pallas_call(
        _dispatch_kernel,
        grid_spec=pltpu.PrefetchScalarGridSpec(
            num_scalar_prefetch=2,
            grid=(t // tm,),
            in_specs=[pl.BlockSpec((1, 1, 2 * tm), lambda i, ps, pn: (i, 0, 0), memory_space=pltpu.SMEM),
                      pl.BlockSpec((tm,) + slab, lambda i, ps, pn: (i, 0, 0))],
            out_specs=pl.BlockSpec(memory_space=pl.ANY),
            scratch_shapes=[pltpu.VMEM((ZERO_ROWS,) + slab, xp.dtype), pltpu.VMEM((2, tm) + slab, xp.dtype),
                            pltpu.SemaphoreType.DMA((2,)), pltpu.SemaphoreType.DMA(())],
        ),
        out_shape=jax.ShapeDtypeStruct((n_rows,) + slab, xp.dtype),
        compiler_params=_params("arbitrary"),
        name="moe_dispatch",
    )(pad_start, pad_len, pos, xp)


def _expert_kernel(te_ref, nv_ref, xs_ref, wg_ref, wu_ref, wd_ref, ys_ref, wg16, wu16, wd16):
    i = pl.program_id(0)
    live = i < nv_ref[0]

    @pl.when(live & ((i == 0) | (te_ref[i] != te_ref[jnp.maximum(i - 1, 0)])))
    def _():
        wg16[...] = wg_ref[0].astype(BF16)
        wu16[...] = wu_ref[0].astype(BF16)
        wd16[...] = wd_ref[0].astype(BF16)

    @pl.when(live)
    def _():
        half = D_MODEL // 2
        xa, xb = _unpack_bf16_pairs(_from_slabs(xs_ref[...]))
        xa = xa.astype(BF16)
        xb = xb.astype(BF16)
        a = _dot(xa, wg16[:half, :]) + _dot(xb, wg16[half:, :])
        u = _dot(xa, wu16[:half, :]) + _dot(xb, wu16[half:, :])
        hid = (a / (1.0 + jnp.exp(-a))) * u
        ys_ref[...] = _to_slabs(_pack_bf16_pairs(_dot(hid.astype(BF16), wd16[...])))

    @pl.when(i >= nv_ref[0])
    def _():
        ys_ref[...] = jnp.zeros_like(ys_ref)


def _experts(tile_expert, n_valid, xs, wg, wu, wd, layer):
    n_rows = xs.shape[0]
    te = EXPERT_TILE
    base = layer * N_EXPERTS
    last = lambda i, te_ref, nv_ref: jnp.minimum(i, nv_ref[0] - 1)
    slab = (te,) + xs.shape[1:]
    wspec = lambda shape: pl.BlockSpec(
        (1,) + shape, lambda i, te_ref, nv_ref: (base + te_ref[last(i, te_ref, nv_ref)], 0, 0))
    return pl.pallas_call(
        _expert_kernel,
        grid_spec=pltpu.PrefetchScalarGridSpec(
            num_scalar_prefetch=2,
            grid=(n_rows // te,),
            in_specs=[pl.BlockSpec(slab, lambda i, te_ref, nv_ref: (last(i, te_ref, nv_ref), 0, 0)),
                      wspec((D_MODEL, D_EXPERT)), wspec((D_MODEL, D_EXPERT)), wspec((D_EXPERT, D_MODEL))],
            out_specs=pl.BlockSpec(slab, lambda i, te_ref, nv_ref: (i, 0, 0)),
            scratch_shapes=[pltpu.VMEM((D_MODEL, D_EXPERT), BF16), pltpu.VMEM((D_MODEL, D_EXPERT), BF16),
                            pltpu.VMEM((D_EXPERT, D_MODEL), BF16)],
        ),
        out_shape=jax.ShapeDtypeStruct(xs.shape, xs.dtype),
        compiler_params=_params("arbitrary"),
        name="moe_experts",
    )(tile_expert, n_valid, xs, wg, wu, wd)


def _combine_ple_kernel(pos_ref, pos_next_ref, info_ref, h_ref, p_ref, g_ref, wpg_ref, wpe_ref, gf_ref, ys_hbm,
                        o_ref, buf, sem, *, final):
    i = pl.program_id(0)
    tm = h_ref.shape[0]
    slot = i % 2

    def gather(rows_ref, dst):
        def issue(t, _):
            for k in range(2):
                pltpu.make_async_copy(ys_hbm.at[rows_ref[0, 0, k * tm + t]],
                                      buf.at[dst, k * tm + t], sem.at[dst]).start(priority=k)
            return 0

        lax.fori_loop(0, tm, issue, 0, unroll=8)

    @pl.when(i == 0)
    def _():
        gather(pos_ref, 0)

    @pl.when(i + 1 < pl.num_programs(0))
    def _():
        gather(pos_next_ref, 1 - slot)

    info = info_ref[...]
    lane = lax.broadcasted_iota(jnp.int32, info.shape, 1)
    gate1 = jnp.sum(jnp.where(lane == R_W1, info, 0.0), axis=1, keepdims=True)
    gate2 = jnp.sum(jnp.where(lane == R_W2, info, 0.0), axis=1, keepdims=True)
    pltpu.make_async_copy(ys_hbm.at[pl.ds(0, 2 * tm)], buf.at[slot], sem.at[slot]).wait()
    y1a, y1b = _unpack_bf16_pairs(_from_slabs(buf[slot, 0:tm]))
    y2a, y2b = _unpack_bf16_pairs(_from_slabs(buf[slot, tm:2 * tm]))
    y = jnp.concatenate([gate1 * y1a + gate2 * y2a, gate1 * y1b + gate2 * y2b], axis=1)
    h = h_ref[...] + y
    xn = _rms(h, g_ref[...]).astype(BF16)
    gate = 1.0 / (1.0 + jnp.exp(-_dot(xn, wpg_ref[...])))
    out = h + gate * _dot(p_ref[0].astype(BF16), wpe_ref[...])
    if final:
        out = _rms(out, gf_ref[...])
    o_ref[...] = out


def _combine_ple(pos, info, h, p, layer, g, wpg, wpe, gf, ys, final):
    t = h.shape[0]
    tm = TOKEN_TILE
    n = t // tm
    row = lambda w: pl.BlockSpec((tm, w), lambda i: (i, 0))
    full = lambda a: pl.BlockSpec(a.shape, lambda i: (0,) * a.ndim)
    return pl.pallas_call(
        functools.partial(_combine_ple_kernel, final=final),
        grid=(n,),
        in_specs=[pl.BlockSpec((1, 1, 2 * tm), lambda i: (i, 0, 0), memory_space=pltpu.SMEM),
                  pl.BlockSpec((1, 1, 2 * tm), lambda i: (jnp.minimum(i + 1, n - 1), 0, 0), memory_space=pltpu.SMEM),
                  row(ROUTER_LANES), row(D_MODEL), pl.BlockSpec((1, tm, D_PLE), lambda i: (layer, i, 0)),
                  full(g), full(wpg), full(wpe), full(gf),
                  pl.BlockSpec(memory_space=pl.ANY)],
        out_specs=row(D_MODEL),
        out_shape=jax.ShapeDtypeStruct((t, D_MODEL), F32),
        scratch_shapes=[pltpu.VMEM((2, 2 * tm) + ys.shape[1:], ys.dtype), pltpu.SemaphoreType.DMA((2,))],
        compiler_params=_params("arbitrary"),
        name="moe_combine_ple",
    )(pos, pos, info, h, p, g, wpg, wpe, gf, ys)


def _rot_cols(w):
    half = w.shape[-1] // 2
    return jnp.concatenate([-w[:, half:], w[:, :half]], axis=1)


def _prep_in(w_in):
    sb_scale = SB_HEAD_DIM ** -0.5 * LOG2_E
    kr = w_in[:, 3 * SB_WIDTH + MLA_Q_RANK + MLA_KV_RANK:]
    pad = jnp.zeros((w_in.shape[0], LANES - 2 * MLA_ROPE), w_in.dtype)
    kra = jnp.concatenate([kr, kr, pad], axis=1)
    krr = _rot_cols(kr)
    krb = jnp.concatenate([krr, krr, pad], axis=1)
    w1 = jnp.concatenate([w_in[:, :SB_WIDTH] * sb_scale,
                          w_in[:, SB_WIDTH:3 * SB_WIDTH + MLA_Q_RANK + MLA_KV_RANK], kra, krb], axis=1)
    return w1.astype(BF16)


def _prep_uq(w_uq):
    scale = MLA_QK ** -0.5 * LOG2_E
    w = w_uq.reshape(MLA_Q_RANK, MLA_HEADS, MLA_QK) * scale
    wqn = w[:, :, :MLA_NOPE].reshape(MLA_Q_RANK, MLA_HEADS * MLA_NOPE)
    rope = w[:, :, MLA_NOPE:]
    half = MLA_ROPE // 2
    rot = jnp.concatenate([-rope[:, :, half:], rope[:, :, :half]], axis=2)
    pad = jnp.zeros((MLA_Q_RANK, HEAD_PAIRS, LANES - 2 * MLA_ROPE), w.dtype)

    def pairs(r):
        return jnp.concatenate([r.reshape(MLA_Q_RANK, HEAD_PAIRS, 2 * MLA_ROPE), pad], axis=2).reshape(
            MLA_Q_RANK, HEAD_PAIRS * LANES)

    return wqn.astype(BF16), pairs(rope).astype(BF16), pairs(rot).astype(BF16)


def _prep_ukv(w_ukv):
    w = w_ukv.reshape(MLA_KV_RANK, MLA_HEADS, MLA_NOPE + MLA_V)
    wk = w[:, :, :MLA_NOPE].reshape(MLA_KV_RANK, MLA_HEADS * MLA_NOPE)
    pad = jnp.zeros((MLA_KV_RANK, MLA_HEADS, LANES - MLA_V), w.dtype)
    wv = jnp.concatenate([w[:, :, MLA_NOPE:], pad], axis=2).reshape(MLA_KV_RANK, MLA_HEADS * LANES)
    return wk.astype(BF16), wv.astype(BF16)


def _prep_router(w_rg, b_rg, w_re, b_re):
    pad = ROUTER_LANES - N_GROUPS - N_EXPERTS
    wr = jnp.concatenate([w_rg, w_re, jnp.zeros((D_MODEL, pad), F32)], axis=1)
    br = jnp.concatenate([b_rg, b_re, jnp.zeros((pad,), F32)])[None, :]
    wr_hi = wr.astype(BF16)
    wr_lo = (wr - wr_hi.astype(F32)).astype(BF16)
    return wr_hi, wr_lo, br


def kernel(x, p, positions, g_mix, w_in, g_cq, w_uq, g_ckv, w_ukv, g_osb, g_omla, w_out, g_moe,
           w_rg, b_rg, w_re, b_re, w_gate, w_up, w_down, g_ple, w_pg, w_pe, g_final):
    b, s, d = x.shape
    t = b * s
    depth = w_in.shape[0]

    inv_freq = ROPE_THETA ** (-jnp.arange(0, MLA_ROPE, 2, dtype=F32) / MLA_ROPE)
    ang = positions.astype(F32)[..., None] * inv_freq
    reps = LANES // (MLA_ROPE // 2)
    cos = jnp.tile(jnp.cos(ang), (1, 1, reps)).reshape(t, LANES)
    sin = jnp.tile(jnp.sin(ang), (1, 1, reps)).reshape(t, LANES)

    h = x.reshape(t, d)
    n_rows = -(-(2 * t + N_EXPERTS * (EXPERT_TILE - 1)) // EXPERT_TILE) * EXPERT_TILE
    p_all = p.reshape(depth, t, D_PLE)
    wg_all = w_gate.reshape(depth * N_EXPERTS, D_MODEL, D_EXPERT)
    wu_all = w_up.reshape(depth * N_EXPERTS, D_MODEL, D_EXPERT)
    wd_all = w_down.reshape(depth * N_EXPERTS, D_EXPERT, D_MODEL)
    r3 = lambda a: a.reshape(b, s, a.shape[-1])
    r2 = lambda a: a.reshape(t, a.shape[-1])
    for i in range(depth):
        w1 = _prep_in(w_in[i])
        wqn, wqra, wqrb = _prep_uq(w_uq[i])
        wkvk, wkvv = _prep_ukv(w_ukv[i])
        qsb, ksb, vsb, qn, qr, kn, kr, vm = _mixer_in(
            h, g_mix[i][None], cos, sin, g_cq[i][None], g_ckv[i][None], w1, wqn, wqra, wqrb, wkvk, wkvv)
        osb = _sb_attention(r3(qsb), r3(ksb), r3(vsb))
        omla = _mla_attention(r3(qn), r3(qr), r3(kn), r3(kr), r3(vm))
        wrh, wrl, br = _prep_router(w_rg[i], b_rg[i], w_re[i], b_re[i])
        h1, xp, info, meta, counts = _mixer_out(h, r2(osb), r2(omla), g_osb[i][None], g_omla[i][None],
                                          w_out[i].astype(BF16), g_moe[i][None], wrh, wrl, br)
        pos, tile_expert, n_valid, pad_start, pad_len = _route_plan(meta, counts, n_rows)
        xs = _dispatch(pad_start, pad_len, pos, xp, n_rows)
        ys = _experts(tile_expert, n_valid, xs, wg_all, wu_all, wd_all, i)
        h = _combine_ple(pos, info, h1, p_all, i, g_ple[i][None], w_pg[i].astype(BF16),
                         w_pe[i].astype(BF16), g_final[None], ys, final=(i == depth - 1))
    return h.reshape(b, s, d)
```

```python
import functools

import jax
import jax.numpy as jnp
from jax import lax
from jax.experimental import pallas as pl
from jax.experimental.pallas import tpu as pltpu

F32 = jnp.float32
BF16 = jnp.bfloat16

D_MODEL = 1024
CHUNK = 64
D_PLE = 256
EPS = 1e-6
SB_HEADS = 8
SB_HEAD_DIM = 64
SB_WIDTH = SB_HEADS * SB_HEAD_DIM
MLA_HEADS = 8
MLA_NOPE = 64
MLA_ROPE = 32
MLA_QK = MLA_NOPE + MLA_ROPE
MLA_V = 64
MLA_Q_RANK = 384
MLA_KV_RANK = 256
MLA_WIDTH = MLA_HEADS * MLA_V
ROPE_THETA = 10000.0
N_GROUPS = 4
EXPERTS_PER_GROUP = 8
N_EXPERTS = N_GROUPS * EXPERTS_PER_GROUP
D_EXPERT = 256

LANES = 128
HEAD_PAIRS = SB_HEADS // 2
ROUTER_LANES = LANES
TOKEN_TILE = 512
EXPERT_TILE = 512
PACK_SUBLANES = D_MODEL // 2 // LANES
ZERO_ROWS = 64
ATTN_BLOCK = 256
LOG2_E = 1.4426950408889634
SB_UNDERFLOW = 151.0
VMEM_LIMIT = 48 * 1024 * 1024


def _params(*sem):
    return pltpu.CompilerParams(dimension_semantics=sem, vmem_limit_bytes=VMEM_LIMIT)


def _rms(x, g):
    return x * lax.rsqrt(jnp.mean(x * x, axis=-1, keepdims=True) + EPS) * g


def _dot(a, b):
    return jnp.dot(a, b, preferred_element_type=F32)


def _dot_t(a, b):
    return lax.dot_general(a, b, (((1,), (1,)), ((), ())), preferred_element_type=F32)


def _mixer_in_kernel(x_ref, g_ref, cos_ref, sin_ref, gcq_ref, gckv_ref, w1_ref, wqn_ref,
                     wqra_ref, wqrb_ref, wkvk_ref, wkvv_ref,
                     qsb_ref, ksb_ref, vsb_ref, qn_ref, qr_ref, kn_ref, kr_ref, vm_ref):
    xn = _rms(x_ref[...], g_ref[...]).astype(BF16)
    c0 = 0
    qsb_ref[...] = _dot(xn, w1_ref[:, c0:c0 + SB_WIDTH]).astype(BF16)
    c0 += SB_WIDTH
    ksb_ref[...] = _dot(xn, w1_ref[:, c0:c0 + SB_WIDTH]).astype(BF16)
    c0 += SB_WIDTH
    vsb_ref[...] = _dot(xn, w1_ref[:, c0:c0 + SB_WIDTH]).astype(BF16)
    c0 += SB_WIDTH
    cq = _dot(xn, w1_ref[:, c0:c0 + MLA_Q_RANK])
    c0 += MLA_Q_RANK
    ckv = _dot(xn, w1_ref[:, c0:c0 + MLA_KV_RANK])
    c0 += MLA_KV_RANK
    kra = _dot(xn, w1_ref[:, c0:c0 + LANES])
    c0 += LANES
    krb = _dot(xn, w1_ref[:, c0:c0 + LANES])
    cos = cos_ref[...]
    sin = sin_ref[...]
    kr_ref[...] = (kra * cos + krb * sin).astype(BF16)

    cqn = _rms(cq, gcq_ref[...]).astype(BF16)
    qn_ref[...] = _dot(cqn, wqn_ref[...]).astype(BF16)
    ra = _dot(cqn, wqra_ref[...])
    rb = _dot(cqn, wqrb_ref[...])
    for p in range(HEAD_PAIRS):
        sl = slice(p * LANES, (p + 1) * LANES)
        qr_ref[:, sl] = (ra[:, sl] * cos + rb[:, sl] * sin).astype(BF16)

    ckvn = _rms(ckv, gckv_ref[...]).astype(BF16)
    kn_ref[...] = _dot(ckvn, wkvk_ref[...]).astype(BF16)
    lane = lax.broadcasted_iota(jnp.int32, (1, LANES), 1)
    for hd in range(MLA_HEADS):
        sl = slice(hd * LANES, (hd + 1) * LANES)
        vm_ref[:, sl] = jnp.where(lane < MLA_V, _dot(ckvn, wkvv_ref[:, sl]), 1.0).astype(BF16)


def _mixer_in(h, g, cos, sin, gcq, gckv, w1, wqn, wqra, wqrb, wkvk, wkvv):
    t = h.shape[0]
    tm = TOKEN_TILE
    row = lambda w: pl.BlockSpec((tm, w), lambda i: (i, 0))
    full = lambda a: pl.BlockSpec(a.shape, lambda i: (0,) * a.ndim)
    widths = [SB_WIDTH, SB_WIDTH, SB_WIDTH, MLA_WIDTH, MLA_WIDTH, MLA_WIDTH, LANES, MLA_HEADS * LANES]
    return pl.pallas_call(
        _mixer_in_kernel,
        grid=(t // tm,),
        in_specs=[row(D_MODEL), full(g), row(LANES), row(LANES), full(gcq), full(gckv),
                  full(w1), full(wqn), full(wqra), full(wqrb), full(wkvk), full(wkvv)],
        out_specs=[row(w) for w in widths],
        out_shape=[jax.ShapeDtypeStruct((t, w), BF16) for w in widths],
        compiler_params=_params("parallel"),
        name="mixer_in",
    )(h, g, cos, sin, gcq, gckv, w1, wqn, wqra, wqrb, wkvk, wkvv)


def _sb_kernel(q_ref, k_ref, v_ref, o_ref, acc_ref, c_ref, *, blk):
    qi = pl.program_id(1)
    lane = lax.broadcasted_iota(jnp.int32, (1, LANES), 1)
    first = lane < SB_HEAD_DIM
    q_st = []
    for p in range(HEAD_PAIRS):
        q2 = q_ref[0, :, p * LANES:(p + 1) * LANES]
        zero = jnp.zeros_like(q2)
        q_st.append(jnp.concatenate([jnp.where(first, q2, zero), jnp.where(first, zero, q2)], axis=0))
    r = lax.broadcasted_iota(jnp.int32, (blk, blk), 0)
    c = lax.broadcasted_iota(jnp.int32, (blk, blk), 1)
    tri = jnp.where(r >= c, 1.0, 0.0).astype(BF16)
    r2 = lax.broadcasted_iota(jnp.int32, (2 * blk, blk), 0)
    c2 = lax.broadcasted_iota(jnp.int32, (2 * blk, blk), 1)
    causal = c2 < jnp.where(r2 >= blk, r2 - blk, r2)
    sign_bit = jnp.uint32(0x80000000)

    acc_ref[...] = jnp.zeros_like(acc_ref)
    c_ref[...] = jnp.zeros_like(c_ref)

    def block(kb, diag):
        start = pl.multiple_of(kb * blk, blk)
        cols = lambda ref, p: ref[0, pl.ds(start, blk), p * LANES:(p + 1) * LANES]
        z = [_dot_t(q_st[p], cols(k_ref, p)) for p in range(HEAD_PAIRS)]
        sp16 = []
        for p in range(HEAD_PAIRS):
            neg_abs = pltpu.bitcast(pltpu.bitcast(z[p], jnp.uint32) | sign_bit, F32)
            sp = jnp.maximum(z[p], 0.0) + jnp.log(1.0 + jnp.exp2(neg_abs)) * LOG2_E
            if diag:
                sp = jnp.where(causal, sp, 0.0)
            sp16.append(sp.astype(BF16))
        cl = [_dot(sp16[p], tri) for p in range(HEAD_PAIRS)]
        w = []
        for p in range(HEAD_PAIRS):
            wp = jnp.exp2(z[p] - cl[p])
            if diag:
                wp = jnp.where(causal, wp, 0.0)
            w.append(wp.astype(BF16))
        pv = [_dot(w[p], cols(v_ref, p)) for p in range(HEAD_PAIRS)]
        for p in range(HEAD_PAIRS):
            carry = c_ref[p]
            acc_ref[p] += jnp.exp2(-carry) * pv[p]
            c_ref[p] = carry + jnp.broadcast_to(cl[p][:, 0:1], carry.shape)

    def min_carry():
        return jnp.min(jnp.min(c_ref[...], axis=0))

    block(qi, True)

    def cond(st):
        return (st[0] < qi) & (st[1] < SB_UNDERFLOW)

    def body(st):
        block(qi - 1 - st[0], False)
        return st[0] + 1, min_carry()

    lax.while_loop(cond, body, (jnp.int32(0), min_carry()))
    for p in range(HEAD_PAIRS):
        a = acc_ref[p]
        o_ref[0, :, p * LANES:(p + 1) * LANES] = jnp.where(first, a[:blk], a[blk:]).astype(o_ref.dtype)


def _sb_attention(q, k, v):
    b, s, wdt = q.shape
    blk = ATTN_BLOCK
    qspec = pl.BlockSpec((1, blk, wdt), lambda bi, qi: (bi, qi, 0))
    kspec = pl.BlockSpec((1, s, wdt), lambda bi, qi: (bi, 0, 0))
    state = pltpu.VMEM((HEAD_PAIRS, 2 * blk, LANES), F32)
    return pl.pallas_call(
        functools.partial(_sb_kernel, blk=blk),
        grid=(b, s // blk),
        in_specs=[qspec, kspec, kspec],
        out_specs=qspec,
        out_shape=jax.ShapeDtypeStruct(q.shape, BF16),
        scratch_shapes=[state, state],
        compiler_params=_params("parallel", "arbitrary"),
        name="sb_attention",
    )(q, k, v)


def _mla_kernel(qn_ref, qr_ref, kn_ref, kr_ref, v_ref, o_ref, acc_ref, m_ref, *, blk):
    qi = pl.program_id(1)
    lane2 = lax.broadcasted_iota(jnp.int32, (1, 2 * LANES), 1)
    sel0 = (lane2 < MLA_NOPE) | ((lane2 >= LANES) & (lane2 < LANES + MLA_ROPE))
    sel1 = ((lane2 >= MLA_NOPE) & (lane2 < LANES)) | (
        (lane2 >= LANES + MLA_ROPE) & (lane2 < LANES + 2 * MLA_ROPE))
    q_st = []
    for p in range(HEAD_PAIRS):
        sl = slice(p * LANES, (p + 1) * LANES)
        qcat = jnp.concatenate([qn_ref[0, :, sl], qr_ref[0, :, sl]], axis=1)
        zero = jnp.zeros_like(qcat)
        q_st.append(jnp.concatenate([jnp.where(sel0, qcat, zero), jnp.where(sel1, qcat, zero)], axis=0))
    r2 = lax.broadcasted_iota(jnp.int32, (2 * blk, blk), 0)
    c2 = lax.broadcasted_iota(jnp.int32, (2 * blk, blk), 1)
    visible = (c2 // CHUNK) <= (jnp.where(r2 >= blk, r2 - blk, r2) // CHUNK)
    lane = lax.broadcasted_iota(jnp.int32, (1, LANES), 1)
    first = lane < MLA_V

    acc_ref[...] = jnp.zeros_like(acc_ref)
    m_ref[...] = jnp.full(m_ref.shape, -jnp.inf, F32)

    def blocks(kbs, diag):
        starts = [pl.multiple_of(kb * blk, blk) for kb in kbs]
        sc = []
        for start in starts:
            kr_blk = kr_ref[0, pl.ds(start, blk), :]
            sc.append([_dot_t(q_st[p], jnp.concatenate(
                [kn_ref[0, pl.ds(start, blk), p * LANES:(p + 1) * LANES], kr_blk], axis=1))
                for p in range(HEAD_PAIRS)])
        pr, alpha = [], []
        for sc_b in sc:
            pr_b, alpha_b = [], []
            for p in range(HEAD_PAIRS):
                s_p = jnp.where(visible, sc_b[p], -jnp.inf) if diag else sc_b[p]
                m_prev = m_ref[p]
                m_new = jnp.maximum(m_prev, jnp.max(s_p, axis=1, keepdims=True))
                alpha_b.append(jnp.exp2(m_prev - m_new))
                pr_b.append(jnp.exp2(s_p - jnp.concatenate([m_new] * (blk // LANES), axis=1)).astype(BF16))
                m_ref[p] = m_new
            pr.append(pr_b)
            alpha.append(alpha_b)
        for h in range(MLA_HEADS):
            p, half = divmod(h, 2)
            rows = slice(half * blk, (half + 1) * blk)
            a = acc_ref[h]
            for start, pr_b, alpha_b in zip(starts, pr, alpha):
                a = alpha_b[p][rows] * a + _dot(pr_b[p][rows], v_ref[0, pl.ds(start, blk), h * LANES:(h + 1) * LANES])
            acc_ref[h] = a

    blocks([qi], True)

    def body(j, _):
        blocks([2 * j, 2 * j + 1], False)
        return 0

    lax.fori_loop(0, qi // 2, body, 0)

    @pl.when(qi % 2 == 1)
    def _():
        blocks([qi - 1], False)

    for p in range(HEAD_PAIRS):
        a0 = acc_ref[2 * p]
        a1 = acc_ref[2 * p + 1]
        o0 = a0 / pltpu.roll(a0, MLA_V, axis=1)
        o1 = pltpu.roll(a1 / pltpu.roll(a1, MLA_V, axis=1), MLA_V, axis=1)
        o_ref[0, :, p * LANES:(p + 1) * LANES] = jnp.where(first, o0, o1).astype(o_ref.dtype)


def _mla_attention(qn, qr, kn, kr, vcat):
    b, s, wdt = qn.shape
    blk = ATTN_BLOCK
    qspec = pl.BlockSpec((1, blk, wdt), lambda bi, qi: (bi, qi, 0))
    full = lambda a: pl.BlockSpec((1, s, a.shape[-1]), lambda bi, qi: (bi, 0, 0))
    return pl.pallas_call(
        functools.partial(_mla_kernel, blk=blk),
        grid=(b, s // blk),
        in_specs=[qspec, qspec, full(kn), full(kr), full(vcat)],
        out_specs=qspec,
        out_shape=jax.ShapeDtypeStruct(qn.shape, BF16),
        scratch_shapes=[pltpu.VMEM((MLA_HEADS, blk, LANES), F32),
                        pltpu.VMEM((HEAD_PAIRS, 2 * blk, LANES), F32)],
        compiler_params=_params("parallel", "arbitrary"),
        name="mla_attention",
    )(qn, qr, kn, kr, vcat)


def _route(logits):
    lane = lax.broadcasted_iota(jnp.int32, logits.shape, 1).astype(F32)
    ninf = -jnp.inf
    big = float(ROUTER_LANES)
    is_g = lane < N_GROUPS
    lg = jnp.where(is_g, logits, ninf)
    gmax = jnp.max(lg, axis=1, keepdims=True)
    gsum = jnp.sum(jnp.where(is_g, jnp.exp(lg - gmax), 0.0), axis=1, keepdims=True)
    gp = 1.0 / gsum
    g = jnp.min(jnp.where(lg == gmax, lane, big), axis=1, keepdims=True)
    lo = N_GROUPS + EXPERTS_PER_GROUP * g
    in_grp = (lane >= lo) & (lane < lo + EXPERTS_PER_GROUP)
    le = jnp.where(in_grp, logits, ninf)
    l1 = jnp.max(le, axis=1, keepdims=True)
    i1 = jnp.min(jnp.where(le == l1, lane, big), axis=1, keepdims=True)
    le2 = jnp.where(lane == i1, ninf, le)
    l2 = jnp.max(le2, axis=1, keepdims=True)
    i2 = jnp.min(jnp.where(le2 == l2, lane, big), axis=1, keepdims=True)
    t = jnp.exp(l2 - l1)
    w1 = gp / (1.0 + t)
    w2 = gp * t / (1.0 + t)
    return i1 - N_GROUPS, i2 - N_GROUPS, w1, w2


def _pack_bf16_pairs(x):
    n = x.shape[1] // 2
    xb = x.astype(BF16).astype(F32)
    hi = pltpu.bitcast(xb[:, :n], jnp.uint32)
    lo = pltpu.bitcast(xb[:, n:], jnp.uint32)
    return hi | (lo >> 16)


def _unpack_bf16_pairs(w):
    hi = pltpu.bitcast(w & jnp.uint32(0xFFFF0000), F32)
    lo = pltpu.bitcast(w << 16, F32)
    return hi, lo


def _to_slabs(rows):
    return rows.reshape(rows.shape[0], PACK_SUBLANES, LANES)


def _from_slabs(slabs):
    return slabs.reshape(slabs.shape[0], PACK_SUBLANES * LANES)


R_E1, R_E2, R_W1, R_W2, R_RANK1, R_RANK2 = range(6)
META_ROWS = 8


def _mixer_out_kernel(h_ref, osb_ref, omla_ref, gosb_ref, gomla_ref, wout_ref, gmoe_ref,
                      wrh_ref, wrl_ref, br_ref, h1_ref, xp_ref, info_ref, meta_ref, cnt_ref):
    @pl.when(pl.program_id(0) == 0)
    def _():
        cnt_ref[...] = jnp.zeros_like(cnt_ref)

    tm = h_ref.shape[0]
    hm = tm // 2
    halves = [slice(0, hm), slice(hm, tm)]
    nsb = [_rms(osb_ref[hs, :].astype(F32), gosb_ref[...]).astype(BF16) for hs in halves]
    nmla = [_rms(omla_ref[hs, :].astype(F32), gomla_ref[...]).astype(BF16) for hs in halves]
    h1 = [h_ref[hs, :] + _dot(nsb[i], wout_ref[0:SB_WIDTH, :]) + _dot(nmla[i], wout_ref[SB_WIDTH:, :])
          for i, hs in enumerate(halves)]
    xn = []
    for i, hs in enumerate(halves):
        h1_ref[hs, :] = h1[i]
        xn.append(_rms(h1[i], gmoe_ref[...]))
        xp_ref[hs] = _to_slabs(_pack_bf16_pairs(xn[i]))
    x_hi = [x.astype(BF16) for x in xn]
    x_lo = [(x - xh.astype(F32)).astype(BF16) for x, xh in zip(xn, x_hi)]
    logits = [(_dot(x_hi[i], wrh_ref[...]) + _dot(x_hi[i], wrl_ref[...]) + _dot(x_lo[i], wrh_ref[...])) + br_ref[...]
              for i in range(2)]
    routes = [_route(lg) for lg in logits]

    lane = lax.broadcasted_iota(jnp.int32, logits[0].shape, 1).astype(F32)
    r = lax.broadcasted_iota(jnp.int32, (hm, hm), 0)
    c = lax.broadcasted_iota(jnp.int32, (hm, hm), 1)
    before = jnp.where(c < r, 1.0, 0.0).astype(BF16)
    onehot = [jnp.where((lane == e1) | (lane == e2), 1.0, 0.0) for e1, e2, _, _ in routes]
    prefix = [_dot(before, oh.astype(BF16)) for oh in onehot]
    base = cnt_ref[...]
    for i, hs in enumerate(halves):
        e1, e2, w1, w2 = routes[i]
        seen = prefix[i] + base
        rank1 = jnp.sum(jnp.where(lane == e1, seen, 0.0), axis=1, keepdims=True)
        rank2 = jnp.sum(jnp.where(lane == e2, seen, 0.0), axis=1, keepdims=True)
        base = base + jnp.sum(onehot[i], axis=0, keepdims=True)
        info = jnp.zeros_like(logits[i])
        for idx, val in ((R_E1, e1), (R_E2, e2), (R_W1, w1), (R_W2, w2), (R_RANK1, rank1), (R_RANK2, rank2)):
            info = jnp.where(lane == idx, val, info)
        info_ref[hs, :] = info
        meta_ref[:, hs] = info.T[0:META_ROWS, :]
    cnt_ref[...] = base


def _mixer_out(h, osb, omla, gosb, gomla, wout, gmoe, wrh, wrl, br):
    t = h.shape[0]
    tm = 2 * TOKEN_TILE
    row = lambda w: pl.BlockSpec((tm, w), lambda i: (i, 0))
    full = lambda a: pl.BlockSpec(a.shape, lambda i: (0,) * a.ndim)
    return pl.pallas_call(
        _mixer_out_kernel,
        grid=(t // tm,),
        in_specs=[row(D_MODEL), row(SB_WIDTH), row(MLA_WIDTH), full(gosb), full(gomla), full(wout),
                  full(gmoe), full(wrh), full(wrl), full(br)],
        out_specs=[row(D_MODEL), pl.BlockSpec((tm, PACK_SUBLANES, LANES), lambda i: (i, 0, 0)), row(ROUTER_LANES),
                   pl.BlockSpec((META_ROWS, tm), lambda i: (0, i)), pl.BlockSpec((1, ROUTER_LANES), lambda i: (0, 0))],
        out_shape=[jax.ShapeDtypeStruct((t, D_MODEL), F32),
                   jax.ShapeDtypeStruct((t, PACK_SUBLANES, LANES), jnp.uint32),
                   jax.ShapeDtypeStruct((t, ROUTER_LANES), F32), jax.ShapeDtypeStruct((META_ROWS, t), F32),
                   jax.ShapeDtypeStruct((1, ROUTER_LANES), F32)],
        compiler_params=_params("arbitrary"),
        name="mixer_out",
    )(h, osb, omla, gosb, gomla, wout, gmoe, wrh, wrl, br)


def _route_plan(meta, counts, n_rows):
    te = EXPERT_TILE
    cnt = counts[0, :N_EXPERTS].astype(jnp.int32)
    padded = (cnt + te - 1) // te * te
    seg_end = jnp.cumsum(padded)
    seg_start = seg_end - padded
    e = meta[R_E1:R_E2 + 1].astype(jnp.int32)
    rank = meta[R_RANK1:R_RANK2 + 1].astype(jnp.int32)
    ids = jnp.arange(N_EXPERTS, dtype=jnp.int32)
    pos = jnp.sum(jnp.where(e[..., None] == ids, seg_start, 0), axis=-1) + rank
    tiles = pos.shape[1] // TOKEN_TILE
    pos = pos.reshape(2, tiles, TOKEN_TILE).transpose(1, 0, 2).reshape(tiles, 1, 2 * TOKEN_TILE)
    tile_start = jnp.arange(n_rows // te, dtype=jnp.int32) * te
    tile_expert = jnp.minimum(jnp.sum(tile_start[:, None] >= seg_end[None, :], axis=1), N_EXPERTS - 1)
    n_valid = (seg_end[-1] // te).reshape(1)
    pad_start = jnp.concatenate([seg_start + cnt, seg_end[-1:]])
    pad_len = jnp.concatenate([padded - cnt, n_rows - seg_end[-1:]])
    return pos, tile_expert.astype(jnp.int32), n_valid.astype(jnp.int32), pad_start, pad_len


def _dispatch_kernel(pad_start_ref, pad_len_ref, pos_ref, x_ref, xs_hbm, zero, ring, sem, zsem):
    tm = x_ref.shape[0]

    @pl.when(pl.program_id(0) == 0)
    def _():
        zero[...] = jnp.zeros_like(zero)

        def each_range(copy_fn):
            def per_range(e, _):
                start = pad_start_ref[e]
                n = pad_len_ref[e]
                runs = n // ZERO_ROWS
                lax.fori_loop(0, runs, lambda j, c: copy_fn(
                    pltpu.make_async_copy(zero, xs_hbm.at[pl.ds(start + j * ZERO_ROWS, ZERO_ROWS)], zsem)), 0)
                lax.fori_loop(runs * ZERO_ROWS, n, lambda j, c: copy_fn(
                    pltpu.make_async_copy(zero.at[0], xs_hbm.at[start + j], zsem)), 0)
                return 0

            lax.fori_loop(0, pad_start_ref.shape[0], per_range, 0)

        def start_copy(cp):
            cp.start()
            return 0

        def wait_copy(cp):
            cp.wait()
            return 0

        each_range(start_copy)
        each_range(wait_copy)

    i = pl.program_id(0)
    n = pl.num_programs(0)
    slot = i % 2

    def drain(s):
        for _ in range(2):
            pltpu.make_async_copy(ring.at[s], xs_hbm.at[pl.ds(0, tm)], sem.at[s]).wait()

    @pl.when(i >= 2)
    def _():
        drain(slot)

    ring[slot] = x_ref[...]

    def issue(t, _):
        for k in range(2):
            pltpu.make_async_copy(ring.at[slot, t], xs_hbm.at[pos_ref[0, 0, k * tm + t]],
                                  sem.at[slot]).start(priority=k)
        return 0

    lax.fori_loop(0, tm, issue, 0, unroll=8)

    @pl.when((i == n - 1) & (n >= 2))
    def _():
        drain(1 - slot)

    @pl.when(i == n - 1)
    def _():
        drain(slot)


def _dispatch(pad_start, pad_len, pos, xp, n_rows):
    t = xp.shape[0]
    tm = TOKEN_TILE
    slab = xp.shape[1:]
    return pl.pallas_call(
        _dispatch_kernel,
        grid_spec=pltpu.PrefetchScalarGridSpec(
            num_scalar_prefetch=2,
            grid=(t // tm,),
            in_specs=[pl.BlockSpec((1, 1, 2 * tm), lambda i, ps, pn: (i, 0, 0), memory_space=pltpu.SMEM),
                      pl.BlockSpec((tm,) + slab, lambda i, ps, pn: (i, 0, 0))],
            out_specs=pl.BlockSpec(memory_space=pl.ANY),
            scratch_shapes=[pltpu.VMEM((ZERO_ROWS,) + slab, xp.dtype), pltpu.VMEM((2, tm) + slab, xp.dtype),
                            pltpu.SemaphoreType.DMA((2,)), pltpu.SemaphoreType.DMA(())],
        ),
        out_shape=jax.ShapeDtypeStruct((n_rows,) + slab, xp.dtype),
        compiler_params=_params("arbitrary"),
        name="moe_dispatch",
    )(pad_start, pad_len, pos, xp)


def _expert_kernel(te_ref, nv_ref, xs_ref, wg_ref, wu_ref, wd_ref, ys_ref, wg16, wu16, wd16):
    i = pl.program_id(0)
    live = i < nv_ref[0]

    @pl.when(live & ((i == 0) | (te_ref[i] != te_ref[jnp.maximum(i - 1, 0)])))
    def _():
        wg16[...] = wg_ref[0].astype(BF16)
        wu16[...] = wu_ref[0].astype(BF16)
        wd16[...] = wd_ref[0].astype(BF16)

    @pl.when(live)
    def _():
        half = D_MODEL // 2
        xa, xb = _unpack_bf16_pairs(_from_slabs(xs_ref[...]))
        xa = xa.astype(BF16)
        xb = xb.astype(BF16)
        a = _dot(xa, wg16[:half, :]) + _dot(xb, wg16[half:, :])
        u = _dot(xa, wu16[:half, :]) + _dot(xb, wu16[half:, :])
        hid = (a / (1.0 + jnp.exp(-a))) * u
        ys_ref[...] = _to_slabs(_pack_bf16_pairs(_dot(hid.astype(BF16), wd16[...])))

    @pl.when(i >= nv_ref[0])
    def _():
        ys_ref[...] = jnp.zeros_like(ys_ref)


def _experts(tile_expert, n_valid, xs, wg, wu, wd, layer):
    n_rows = xs.shape[0]
    te = EXPERT_TILE
    base = layer * N_EXPERTS
    last = lambda i, te_ref, nv_ref: jnp.minimum(i, nv_ref[0] - 1)
    slab = (te,) + xs.shape[1:]
    wspec = lambda shape: pl.BlockSpec(
        (1,) + shape, lambda i, te_ref, nv_ref: (base + te_ref[last(i, te_ref, nv_ref)], 0, 0))
    return pl.pallas_call(
        _expert_kernel,
        grid_spec=pltpu.PrefetchScalarGridSpec(
            num_scalar_prefetch=2,
            grid=(n_rows // te,),
            in_specs=[pl.BlockSpec(slab, lambda i, te_ref, nv_ref: (last(i, te_ref, nv_ref), 0, 0)),
                      wspec((D_MODEL, D_EXPERT)), wspec((D_MODEL, D_EXPERT)), wspec((D_EXPERT, D_MODEL))],
            out_specs=pl.BlockSpec(slab, lambda i, te_ref, nv_ref: (i, 0, 0)),
            scratch_shapes=[pltpu.VMEM((D_MODEL, D_EXPERT), BF16), pltpu.VMEM((D_MODEL, D_EXPERT), BF16),
                            pltpu.VMEM((D_EXPERT, D_MODEL), BF16)],
        ),
        out_shape=jax.ShapeDtypeStruct(xs.shape, xs.dtype),
        compiler_params=_params("arbitrary"),
        name="moe_experts",
    )(tile_expert, n_valid, xs, wg, wu, wd)


def _combine_ple_kernel(pos_ref, pos_next_ref, info_ref, h_ref, p_ref, g_ref, wpg_ref, wpe_ref, gf_ref, ys_hbm,
                        o_ref, buf, sem, *, final):
    i = pl.program_id(0)
    tm = h_ref.shape[0]
    slot = i % 2

    def gather(rows_ref, dst):
        def issue(t, _):
            for k in range(2):
                pltpu.make_async_copy(ys_hbm.at[rows_ref[0, 0, k * tm + t]],
                                      buf.at[dst, k * tm + t], sem.at[dst]).start(priority=k)
            return 0

        lax.fori_loop(0, tm, issue, 0, unroll=8)

    @pl.when(i == 0)
    def _():
        gather(pos_ref, 0)

    @pl.when(i + 1 < pl.num_programs(0))
    def _():
        gather(pos_next_ref, 1 - slot)

    info = info_ref[...]
    lane = lax.broadcasted_iota(jnp.int32, info.shape, 1)
    gate1 = jnp.sum(jnp.where(lane == R_W1, info, 0.0), axis=1, keepdims=True)
    gate2 = jnp.sum(jnp.where(lane == R_W2, info, 0.0), axis=1, keepdims=True)
    pltpu.make_async_copy(ys_hbm.at[pl.ds(0, 2 * tm)], buf.at[slot], sem.at[slot]).wait()
    hm = tm // 2
    halves = [slice(0, hm), slice(hm, tm)]
    hs_new = []
    for i, hs in enumerate(halves):
        y1a, y1b = _unpack_bf16_pairs(_from_slabs(buf[slot, i * hm:(i + 1) * hm]))
        y2a, y2b = _unpack_bf16_pairs(_from_slabs(buf[slot, tm + i * hm:tm + (i + 1) * hm]))
        g1, g2 = gate1[hs], gate2[hs]
        y = jnp.concatenate([g1 * y1a + g2 * y2a, g1 * y1b + g2 * y2b], axis=1)
        hs_new.append(h_ref[hs, :] + y)
    xn = [_rms(h, g_ref[...]).astype(BF16) for h in hs_new]
    pe = [_dot(p_ref[0, hs, :].astype(BF16), wpe_ref[...]) for hs in halves]
    gate = [1.0 / (1.0 + jnp.exp(-_dot(x, wpg_ref[...]))) for x in xn]
    for i, hs in enumerate(halves):
        out = hs_new[i] + gate[i] * pe[i]
        if final:
            out = _rms(out, gf_ref[...])
        o_ref[hs, :] = out


def _combine_ple(pos, info, h, p, layer, g, wpg, wpe, gf, ys, final):
    t = h.shape[0]
    tm = TOKEN_TILE
    n = t // tm
    row = lambda w: pl.BlockSpec((tm, w), lambda i: (i, 0))
    full = lambda a: pl.BlockSpec(a.shape, lambda i: (0,) * a.ndim)
    return pl.pallas_call(
        functools.partial(_combine_ple_kernel, final=final),
        grid=(n,),
        in_specs=[pl.BlockSpec((1, 1, 2 * tm), lambda i: (i, 0, 0), memory_space=pltpu.SMEM),
                  pl.BlockSpec((1, 1, 2 * tm), lambda i: (jnp.minimum(i + 1, n - 1), 0, 0), memory_space=pltpu.SMEM),
                  row(ROUTER_LANES), row(D_MODEL), pl.BlockSpec((1, tm, D_PLE), lambda i: (layer, i, 0)),
                  full(g), full(wpg), full(wpe), full(gf),
                  pl.BlockSpec(memory_space=pl.ANY)],
        out_specs=row(D_MODEL),
        out_shape=jax.ShapeDtypeStruct((t, D_MODEL), F32),
        scratch_shapes=[pltpu.VMEM((2, 2 * tm) + ys.shape[1:], ys.dtype), pltpu.SemaphoreType.DMA((2,))],
        compiler_params=_params("arbitrary"),
        name="moe_combine_ple",
    )(pos, pos, info, h, p, g, wpg, wpe, gf, ys)


def _rot_cols(w):
    half = w.shape[-1] // 2
    return jnp.concatenate([-w[:, half:], w[:, :half]], axis=1)


def _prep_in(w_in):
    sb_scale = SB_HEAD_DIM ** -0.5 * LOG2_E
    kr = w_in[:, 3 * SB_WIDTH + MLA_Q_RANK + MLA_KV_RANK:]
    pad = jnp.zeros((w_in.shape[0], LANES - 2 * MLA_ROPE), w_in.dtype)
    kra = jnp.concatenate([kr, kr, pad], axis=1)
    krr = _rot_cols(kr)
    krb = jnp.concatenate([krr, krr, pad], axis=1)
    w1 = jnp.concatenate([w_in[:, :SB_WIDTH] * sb_scale,
                          w_in[:, SB_WIDTH:3 * SB_WIDTH + MLA_Q_RANK + MLA_KV_RANK], kra, krb], axis=1)
    return w1.astype(BF16)


def _prep_uq(w_uq):
    scale = MLA_QK ** -0.5 * LOG2_E
    w = w_uq.reshape(MLA_Q_RANK, MLA_HEADS, MLA_QK) * scale
    wqn = w[:, :, :MLA_NOPE].reshape(MLA_Q_RANK, MLA_HEADS * MLA_NOPE)
    rope = w[:, :, MLA_NOPE:]
    half = MLA_ROPE // 2
    rot = jnp.concatenate([-rope[:, :, half:], rope[:, :, :half]], axis=2)
    pad = jnp.zeros((MLA_Q_RANK, HEAD_PAIRS, LANES - 2 * MLA_ROPE), w.dtype)

    def pairs(r):
        return jnp.concatenate([r.reshape(MLA_Q_RANK, HEAD_PAIRS, 2 * MLA_ROPE), pad], axis=2).reshape(
            MLA_Q_RANK, HEAD_PAIRS * LANES)

    return wqn.astype(BF16), pairs(rope).astype(BF16), pairs(rot).astype(BF16)


def _prep_ukv(w_ukv):
    w = w_ukv.reshape(MLA_KV_RANK, MLA_HEADS, MLA_NOPE + MLA_V)
    wk = w[:, :, :MLA_NOPE].reshape(MLA_KV_RANK, MLA_HEADS * MLA_NOPE)
    pad = jnp.zeros((MLA_KV_RANK, MLA_HEADS, LANES - MLA_V), w.dtype)
    wv = jnp.concatenate([w[:, :, MLA_NOPE:], pad], axis=2).reshape(MLA_KV_RANK, MLA_HEADS * LANES)
    return wk.astype(BF16), wv.astype(BF16)


def _prep_router(w_rg, b_rg, w_re, b_re):
    pad = ROUTER_LANES - N_GROUPS - N_EXPERTS
    wr = jnp.concatenate([w_rg, w_re, jnp.zeros((D_MODEL, pad), F32)], axis=1)
    br = jnp.concatenate([b_rg, b_re, jnp.zeros((pad,), F32)])[None, :]
    wr_hi = wr.astype(BF16)
    wr_lo = (wr - wr_hi.astype(F32)).astype(BF16)
    return wr_hi, wr_lo, br


def kernel(x, p, positions, g_mix, w_in, g_cq, w_uq, g_ckv, w_ukv, g_osb, g_omla, w_out, g_moe,
           w_rg, b_rg, w_re, b_re, w_gate, w_up, w_down, g_ple, w_pg, w_pe, g_final):
    b, s, d = x.shape
    t = b * s
    depth = w_in.shape[0]
    assert d == D_MODEL and s % ATTN_BLOCK == 0 and t % (2 * TOKEN_TILE) == 0, (b, s, d)

    inv_freq = ROPE_THETA ** (-jnp.arange(0, MLA_ROPE, 2, dtype=F32) / MLA_ROPE)
    ang = positions.astype(F32)[..., None] * inv_freq
    reps = LANES // (MLA_ROPE // 2)
    cos = jnp.tile(jnp.cos(ang), (1, 1, reps)).reshape(t, LANES)
    sin = jnp.tile(jnp.sin(ang), (1, 1, reps)).reshape(t, LANES)

    h = x.reshape(t, d)
    n_rows = -(-(2 * t + N_EXPERTS * (EXPERT_TILE - 1)) // EXPERT_TILE) * EXPERT_TILE
    p_all = p.reshape(depth, t, D_PLE)
    wg_all = w_gate.reshape(depth * N_EXPERTS, D_MODEL, D_EXPERT)
    wu_all = w_up.reshape(depth * N_EXPERTS, D_MODEL, D_EXPERT)
    wd_all = w_down.reshape(depth * N_EXPERTS, D_EXPERT, D_MODEL)
    r3 = lambda a: a.reshape(b, s, a.shape[-1])
    r2 = lambda a: a.reshape(t, a.shape[-1])
    for i in range(depth):
        w1 = _prep_in(w_in[i])
        wqn, wqra, wqrb = _prep_uq(w_uq[i])
        wkvk, wkvv = _prep_ukv(w_ukv[i])
        qsb, ksb, vsb, qn, qr, kn, kr, vm = _mixer_in(
            h, g_mix[i][None], cos, sin, g_cq[i][None], g_ckv[i][None], w1, wqn, wqra, wqrb, wkvk, wkvv)
        osb = _sb_attention(r3(qsb), r3(ksb), r3(vsb))
        omla = _mla_attention(r3(qn), r3(qr), r3(kn), r3(kr), r3(vm))
        wrh, wrl, br = _prep_router(w_rg[i], b_rg[i], w_re[i], b_re[i])
        h1, xp, info, meta, counts = _mixer_out(h, r2(osb), r2(omla), g_osb[i][None], g_omla[i][None],
                                          w_out[i].astype(BF16), g_moe[i][None], wrh, wrl, br)
        pos, tile_expert, n_valid, pad_start, pad_len = _route_plan(meta, counts, n_rows)
        xs = _dispatch(pad_start, pad_len, pos, xp, n_rows)
        ys = _experts(tile_expert, n_valid, xs, wg_all, wu_all, wd_all, i)
        h = _combine_ple(pos, info, h1, p_all, i, g_ple[i][None], w_pg[i].astype(BF16),
                         w_pe[i].astype(BF16), g_final[None], ys, final=(i == depth - 1))
    return h.reshape(b, s, d)
```

```python
import functools

import jax
import jax.numpy as jnp
from jax import lax
from jax.experimental import pallas as pl
from jax.experimental.pallas import tpu as pltpu

F32 = jnp.float32
BF16 = jnp.bfloat16

D_MODEL = 1024
CHUNK = 64
D_PLE = 256
EPS = 1e-6
SB_HEADS = 8
SB_HEAD_DIM = 64
SB_WIDTH = SB_HEADS * SB_HEAD_DIM
MLA_HEADS = 8
MLA_NOPE = 64
MLA_ROPE = 32
MLA_QK = MLA_NOPE + MLA_ROPE
MLA_V = 64
MLA_Q_RANK = 384
MLA_KV_RANK = 256
MLA_WIDTH = MLA_HEADS * MLA_V
ROPE_THETA = 10000.0
N_GROUPS = 4
EXPERTS_PER_GROUP = 8
N_EXPERTS = N_GROUPS * EXPERTS_PER_GROUP
D_EXPERT = 256

LANES = 128
HEAD_PAIRS = SB_HEADS // 2
ROUTER_LANES = LANES
TOKEN_TILE = 512
EXPERT_TILE = 512
PACK_SUBLANES = D_MODEL // LANES
ZERO_ROWS = 64
ATTN_BLOCK = 256
LOG2_E = 1.4426950408889634
SB_UNDERFLOW = 151.0
VMEM_LIMIT = 48 * 1024 * 1024


def _params(*sem):
    return pltpu.CompilerParams(dimension_semantics=sem, vmem_limit_bytes=VMEM_LIMIT)


def _rms(x, g):
    return x * lax.rsqrt(jnp.mean(x * x, axis=-1, keepdims=True) + EPS) * g


def _dot(a, b):
    return jnp.dot(a, b, preferred_element_type=F32)


def _dot_t(a, b):
    return lax.dot_general(a, b, (((1,), (1,)), ((), ())), preferred_element_type=F32)


def _mixer_in_kernel(x_ref, g_ref, cos_ref, sin_ref, gcq_ref, gckv_ref, w1_ref, wqn_ref,
                     wqra_ref, wqrb_ref, wkvk_ref, wkvv_ref,
                     qsb_ref, ksb_ref, vsb_ref, qn_ref, qr_ref, kn_ref, kr_ref, vm_ref):
    xn = _rms(x_ref[...], g_ref[...]).astype(BF16)
    c0 = 0
    qsb_ref[...] = _dot(xn, w1_ref[:, c0:c0 + SB_WIDTH]).astype(BF16)
    c0 += SB_WIDTH
    ksb_ref[...] = _dot(xn, w1_ref[:, c0:c0 + SB_WIDTH]).astype(BF16)
    c0 += SB_WIDTH
    vsb_ref[...] = _dot(xn, w1_ref[:, c0:c0 + SB_WIDTH]).astype(BF16)
    c0 += SB_WIDTH
    cq = _dot(xn, w1_ref[:, c0:c0 + MLA_Q_RANK])
    c0 += MLA_Q_RANK
    ckv = _dot(xn, w1_ref[:, c0:c0 + MLA_KV_RANK])
    c0 += MLA_KV_RANK
    kra = _dot(xn, w1_ref[:, c0:c0 + LANES])
    c0 += LANES
    krb = _dot(xn, w1_ref[:, c0:c0 + LANES])
    cos = cos_ref[...]
    sin = sin_ref[...]
    kr_ref[...] = (kra * cos + krb * sin).astype(BF16)

    cqn = _rms(cq, gcq_ref[...]).astype(BF16)
    qn_ref[...] = _dot(cqn, wqn_ref[...]).astype(BF16)
    ra = _dot(cqn, wqra_ref[...])
    rb = _dot(cqn, wqrb_ref[...])
    for p in range(HEAD_PAIRS):
        sl = slice(p * LANES, (p + 1) * LANES)
        qr_ref[:, sl] = (ra[:, sl] * cos + rb[:, sl] * sin).astype(BF16)

    ckvn = _rms(ckv, gckv_ref[...]).astype(BF16)
    kn_ref[...] = _dot(ckvn, wkvk_ref[...]).astype(BF16)
    lane = lax.broadcasted_iota(jnp.int32, (1, LANES), 1)
    for hd in range(MLA_HEADS):
        sl = slice(hd * LANES, (hd + 1) * LANES)
        vm_ref[:, sl] = jnp.where(lane < MLA_V, _dot(ckvn, wkvv_ref[:, sl]), 1.0).astype(BF16)


def _mixer_in(h, g, cos, sin, gcq, gckv, w1, wqn, wqra, wqrb, wkvk, wkvv):
    t = h.shape[0]
    tm = TOKEN_TILE
    row = lambda w: pl.BlockSpec((tm, w), lambda i: (i, 0))
    full = lambda a: pl.BlockSpec(a.shape, lambda i: (0,) * a.ndim)
    widths = [SB_WIDTH, SB_WIDTH, SB_WIDTH, MLA_WIDTH, MLA_WIDTH, MLA_WIDTH, LANES, MLA_HEADS * LANES]
    return pl.pallas_call(
        _mixer_in_kernel,
        grid=(t // tm,),
        in_specs=[row(D_MODEL), full(g), row(LANES), row(LANES), full(gcq), full(gckv),
                  full(w1), full(wqn), full(wqra), full(wqrb), full(wkvk), full(wkvv)],
        out_specs=[row(w) for w in widths],
        out_shape=[jax.ShapeDtypeStruct((t, w), BF16) for w in widths],
        compiler_params=_params("parallel"),
        name="mixer_in",
    )(h, g, cos, sin, gcq, gckv, w1, wqn, wqra, wqrb, wkvk, wkvv)


def _sb_kernel(q_ref, k_ref, v_ref, o_ref, acc_ref, c_ref, *, blk):
    qi = pl.program_id(1)
    lane = lax.broadcasted_iota(jnp.int32, (1, LANES), 1)
    first = lane < SB_HEAD_DIM
    q_st = []
    for p in range(HEAD_PAIRS):
        q2 = q_ref[0, :, p * LANES:(p + 1) * LANES]
        zero = jnp.zeros_like(q2)
        q_st.append(jnp.concatenate([jnp.where(first, q2, zero), jnp.where(first, zero, q2)], axis=0))
    r = lax.broadcasted_iota(jnp.int32, (blk, blk), 0)
    c = lax.broadcasted_iota(jnp.int32, (blk, blk), 1)
    tri = jnp.where(r >= c, 1.0, 0.0).astype(BF16)
    r2 = lax.broadcasted_iota(jnp.int32, (2 * blk, blk), 0)
    c2 = lax.broadcasted_iota(jnp.int32, (2 * blk, blk), 1)
    causal = c2 < jnp.where(r2 >= blk, r2 - blk, r2)
    sign_bit = jnp.uint32(0x80000000)

    acc_ref[...] = jnp.zeros_like(acc_ref)
    c_ref[...] = jnp.zeros_like(c_ref)

    def block(kb, diag):
        start = pl.multiple_of(kb * blk, blk)
        cols = lambda ref, p: ref[0, pl.ds(start, blk), p * LANES:(p + 1) * LANES]
        z = [_dot_t(q_st[p], cols(k_ref, p)) for p in range(HEAD_PAIRS)]
        sp16 = []
        for p in range(HEAD_PAIRS):
            neg_abs = pltpu.bitcast(pltpu.bitcast(z[p], jnp.uint32) | sign_bit, F32)
            sp = jnp.maximum(z[p], 0.0) + jnp.log(1.0 + jnp.exp2(neg_abs)) * LOG2_E
            if diag:
                sp = jnp.where(causal, sp, 0.0)
            sp16.append(sp.astype(BF16))
        cl = [_dot(sp16[p], tri) for p in range(HEAD_PAIRS)]
        w = []
        for p in range(HEAD_PAIRS):
            wp = jnp.exp2(z[p] - cl[p])
            if diag:
                wp = jnp.where(causal, wp, 0.0)
            w.append(wp.astype(BF16))
        pv = [_dot(w[p], cols(v_ref, p)) for p in range(HEAD_PAIRS)]
        for p in range(HEAD_PAIRS):
            carry = c_ref[p]
            acc_ref[p] += jnp.exp2(-carry) * pv[p]
            c_ref[p] = carry + jnp.broadcast_to(cl[p][:, 0:1], carry.shape)

    def min_carry():
        return jnp.min(jnp.min(c_ref[...], axis=0))

    block(qi, True)

    def cond(st):
        return (st[0] < qi) & (st[1] < SB_UNDERFLOW)

    def body(st):
        block(qi - 1 - st[0], False)
        return st[0] + 1, min_carry()

    lax.while_loop(cond, body, (jnp.int32(0), min_carry()))
    for p in range(HEAD_PAIRS):
        a = acc_ref[p]
        o_ref[0, :, p * LANES:(p + 1) * LANES] = jnp.where(first, a[:blk], a[blk:]).astype(o_ref.dtype)


def _sb_attention(q, k, v):
    b, s, wdt = q.shape
    blk = ATTN_BLOCK
    qspec = pl.BlockSpec((1, blk, wdt), lambda bi, qi: (bi, qi, 0))
    kspec = pl.BlockSpec((1, s, wdt), lambda bi, qi: (bi, 0, 0))
    state = pltpu.VMEM((HEAD_PAIRS, 2 * blk, LANES), F32)
    return pl.pallas_call(
        functools.partial(_sb_kernel, blk=blk),
        grid=(b, s // blk),
        in_specs=[qspec, kspec, kspec],
        out_specs=qspec,
        out_shape=jax.ShapeDtypeStruct(q.shape, BF16),
        scratch_shapes=[state, state],
        compiler_params=_params("parallel", "arbitrary"),
        name="sb_attention",
    )(q, k, v)


def _mla_kernel(qn_ref, qr_ref, kn_ref, kr_ref, v_ref, o_ref, acc_ref, m_ref, *, blk):
    qi = pl.program_id(1)
    lane2 = lax.broadcasted_iota(jnp.int32, (1, 2 * LANES), 1)
    sel0 = (lane2 < MLA_NOPE) | ((lane2 >= LANES) & (lane2 < LANES + MLA_ROPE))
    sel1 = ((lane2 >= MLA_NOPE) & (lane2 < LANES)) | (
        (lane2 >= LANES + MLA_ROPE) & (lane2 < LANES + 2 * MLA_ROPE))
    q_st = []
    for p in range(HEAD_PAIRS):
        sl = slice(p * LANES, (p + 1) * LANES)
        qcat = jnp.concatenate([qn_ref[0, :, sl], qr_ref[0, :, sl]], axis=1)
        zero = jnp.zeros_like(qcat)
        q_st.append(jnp.concatenate([jnp.where(sel0, qcat, zero), jnp.where(sel1, qcat, zero)], axis=0))
    r2 = lax.broadcasted_iota(jnp.int32, (2 * blk, blk), 0)
    c2 = lax.broadcasted_iota(jnp.int32, (2 * blk, blk), 1)
    visible = (c2 // CHUNK) <= (jnp.where(r2 >= blk, r2 - blk, r2) // CHUNK)
    lane = lax.broadcasted_iota(jnp.int32, (1, LANES), 1)
    first = lane < MLA_V

    acc_ref[...] = jnp.zeros_like(acc_ref)
    m_ref[...] = jnp.full(m_ref.shape, -jnp.inf, F32)

    def blocks(kbs, diag):
        starts = [pl.multiple_of(kb * blk, blk) for kb in kbs]
        sc = []
        for start in starts:
            kr_blk = kr_ref[0, pl.ds(start, blk), :]
            sc.append([_dot_t(q_st[p], jnp.concatenate(
                [kn_ref[0, pl.ds(start, blk), p * LANES:(p + 1) * LANES], kr_blk], axis=1))
                for p in range(HEAD_PAIRS)])
        pr, alpha = [], []
        for sc_b in sc:
            pr_b, alpha_b = [], []
            for p in range(HEAD_PAIRS):
                s_p = jnp.where(visible, sc_b[p], -jnp.inf) if diag else sc_b[p]
                m_prev = m_ref[p]
                m_new = jnp.maximum(m_prev, jnp.max(s_p, axis=1, keepdims=True))
                alpha_b.append(jnp.exp2(m_prev - m_new))
                pr_b.append(jnp.exp2(s_p - jnp.concatenate([m_new] * (blk // LANES), axis=1)).astype(BF16))
                m_ref[p] = m_new
            pr.append(pr_b)
            alpha.append(alpha_b)
        for h in range(MLA_HEADS):
            p, half = divmod(h, 2)
            rows = slice(half * blk, (half + 1) * blk)
            a = acc_ref[h]
            for start, pr_b, alpha_b in zip(starts, pr, alpha):
                a = alpha_b[p][rows] * a + _dot(pr_b[p][rows], v_ref[0, pl.ds(start, blk), h * LANES:(h + 1) * LANES])
            acc_ref[h] = a

    blocks([qi], True)

    def body(j, _):
        blocks([2 * j, 2 * j + 1], False)
        return 0

    lax.fori_loop(0, qi // 2, body, 0)

    @pl.when(qi % 2 == 1)
    def _():
        blocks([qi - 1], False)

    for p in range(HEAD_PAIRS):
        a0 = acc_ref[2 * p]
        a1 = acc_ref[2 * p + 1]
        o0 = a0 / pltpu.roll(a0, MLA_V, axis=1)
        o1 = pltpu.roll(a1 / pltpu.roll(a1, MLA_V, axis=1), MLA_V, axis=1)
        o_ref[0, :, p * LANES:(p + 1) * LANES] = jnp.where(first, o0, o1).astype(o_ref.dtype)


def _mla_attention(qn, qr, kn, kr, vcat):
    b, s, wdt = qn.shape
    blk = ATTN_BLOCK
    qspec = pl.BlockSpec((1, blk, wdt), lambda bi, qi: (bi, qi, 0))
    full = lambda a: pl.BlockSpec((1, s, a.shape[-1]), lambda bi, qi: (bi, 0, 0))
    return pl.pallas_call(
        functools.partial(_mla_kernel, blk=blk),
        grid=(b, s // blk),
        in_specs=[qspec, qspec, full(kn), full(kr), full(vcat)],
        out_specs=qspec,
        out_shape=jax.ShapeDtypeStruct(qn.shape, BF16),
        scratch_shapes=[pltpu.VMEM((MLA_HEADS, blk, LANES), F32),
                        pltpu.VMEM((HEAD_PAIRS, 2 * blk, LANES), F32)],
        compiler_params=_params("parallel", "arbitrary"),
        name="mla_attention",
    )(qn, qr, kn, kr, vcat)


def _route(logits):
    lane = lax.broadcasted_iota(jnp.int32, logits.shape, 1).astype(F32)
    ninf = -jnp.inf
    big = float(ROUTER_LANES)
    is_g = lane < N_GROUPS
    lg = jnp.where(is_g, logits, ninf)
    gmax = jnp.max(lg, axis=1, keepdims=True)
    gsum = jnp.sum(jnp.where(is_g, jnp.exp(lg - gmax), 0.0), axis=1, keepdims=True)
    gp = 1.0 / gsum
    g = jnp.min(jnp.where(lg == gmax, lane, big), axis=1, keepdims=True)
    lo = N_GROUPS + EXPERTS_PER_GROUP * g
    in_grp = (lane >= lo) & (lane < lo + EXPERTS_PER_GROUP)
    le = jnp.where(in_grp, logits, ninf)
    l1 = jnp.max(le, axis=1, keepdims=True)
    i1 = jnp.min(jnp.where(le == l1, lane, big), axis=1, keepdims=True)
    le2 = jnp.where(lane == i1, ninf, le)
    l2 = jnp.max(le2, axis=1, keepdims=True)
    i2 = jnp.min(jnp.where(le2 == l2, lane, big), axis=1, keepdims=True)
    t = jnp.exp(l2 - l1)
    w1 = gp / (1.0 + t)
    w2 = gp * t / (1.0 + t)
    return i1 - N_GROUPS, i2 - N_GROUPS, w1, w2


def _to_slabs(rows):
    return rows.reshape(rows.shape[0], PACK_SUBLANES, LANES)


def _from_slabs(slabs):
    return slabs.reshape(slabs.shape[0], PACK_SUBLANES * LANES)


R_E1, R_E2, R_W1, R_W2, R_RANK1, R_RANK2 = range(6)
META_ROWS = 8


def _mixer_out_kernel(h_ref, osb_ref, omla_ref, gosb_ref, gomla_ref, wout_ref, gmoe_ref,
                      wrh_ref, wrl_ref, br_ref, h1_ref, xp_ref, info_ref, meta_ref, cnt_ref):
    @pl.when(pl.program_id(0) == 0)
    def _():
        cnt_ref[...] = jnp.zeros_like(cnt_ref)

    tm = h_ref.shape[0]
    hm = tm // 2
    halves = [slice(0, hm), slice(hm, tm)]
    nsb = [_rms(osb_ref[hs, :].astype(F32), gosb_ref[...]).astype(BF16) for hs in halves]
    nmla = [_rms(omla_ref[hs, :].astype(F32), gomla_ref[...]).astype(BF16) for hs in halves]
    h1 = [h_ref[hs, :] + _dot(nsb[i], wout_ref[0:SB_WIDTH, :]) + _dot(nmla[i], wout_ref[SB_WIDTH:, :])
          for i, hs in enumerate(halves)]
    xn = []
    for i, hs in enumerate(halves):
        h1_ref[hs, :] = h1[i]
        xn.append(_rms(h1[i], gmoe_ref[...]))
        xp_ref[hs] = _to_slabs(xn[i])
    x_hi = [x.astype(BF16) for x in xn]
    x_lo = [(x - xh.astype(F32)).astype(BF16) for x, xh in zip(xn, x_hi)]
    logits = [(_dot(x_hi[i], wrh_ref[...]) + _dot(x_hi[i], wrl_ref[...]) + _dot(x_lo[i], wrh_ref[...])) + br_ref[...]
              for i in range(2)]
    routes = [_route(lg) for lg in logits]

    lane = lax.broadcasted_iota(jnp.int32, logits[0].shape, 1).astype(F32)
    r = lax.broadcasted_iota(jnp.int32, (hm, hm), 0)
    c = lax.broadcasted_iota(jnp.int32, (hm, hm), 1)
    before = jnp.where(c < r, 1.0, 0.0).astype(BF16)
    onehot = [jnp.where((lane == e1) | (lane == e2), 1.0, 0.0) for e1, e2, _, _ in routes]
    prefix = [_dot(before, oh.astype(BF16)) for oh in onehot]
    base = cnt_ref[...]
    for i, hs in enumerate(halves):
        e1, e2, w1, w2 = routes[i]
        seen = prefix[i] + base
        rank1 = jnp.sum(jnp.where(lane == e1, seen, 0.0), axis=1, keepdims=True)
        rank2 = jnp.sum(jnp.where(lane == e2, seen, 0.0), axis=1, keepdims=True)
        base = base + jnp.sum(onehot[i], axis=0, keepdims=True)
        info = jnp.zeros_like(logits[i])
        for idx, val in ((R_E1, e1), (R_E2, e2), (R_W1, w1), (R_W2, w2), (R_RANK1, rank1), (R_RANK2, rank2)):
            info = jnp.where(lane == idx, val, info)
        info_ref[hs, :] = info
        meta_ref[:, hs] = info.T[0:META_ROWS, :]
    cnt_ref[...] = base


def _mixer_out(h, osb, omla, gosb, gomla, wout, gmoe, wrh, wrl, br):
    t = h.shape[0]
    tm = 2 * TOKEN_TILE
    row = lambda w: pl.BlockSpec((tm, w), lambda i: (i, 0))
    full = lambda a: pl.BlockSpec(a.shape, lambda i: (0,) * a.ndim)
    return pl.pallas_call(
        _mixer_out_kernel,
        grid=(t // tm,),
        in_specs=[row(D_MODEL), row(SB_WIDTH), row(MLA_WIDTH), full(gosb), full(gomla), full(wout),
                  full(gmoe), full(wrh), full(wrl), full(br)],
        out_specs=[row(D_MODEL), pl.BlockSpec((tm, PACK_SUBLANES, LANES), lambda i: (i, 0, 0)), row(ROUTER_LANES),
                   pl.BlockSpec((META_ROWS, tm), lambda i: (0, i)), pl.BlockSpec((1, ROUTER_LANES), lambda i: (0, 0))],
        out_shape=[jax.ShapeDtypeStruct((t, D_MODEL), F32),
                   jax.ShapeDtypeStruct((t, PACK_SUBLANES, LANES), F32),
                   jax.ShapeDtypeStruct((t, ROUTER_LANES), F32), jax.ShapeDtypeStruct((META_ROWS, t), F32),
                   jax.ShapeDtypeStruct((1, ROUTER_LANES), F32)],
        compiler_params=_params("arbitrary"),
        name="mixer_out",
    )(h, osb, omla, gosb, gomla, wout, gmoe, wrh, wrl, br)


def _route_plan(meta, counts, n_rows):
    te = EXPERT_TILE
    cnt = counts[0, :N_EXPERTS].astype(jnp.int32)
    padded = (cnt + te - 1) // te * te
    seg_end = jnp.cumsum(padded)
    seg_start = seg_end - padded
    e = meta[R_E1:R_E2 + 1].astype(jnp.int32)
    rank = meta[R_RANK1:R_RANK2 + 1].astype(jnp.int32)
    ids = jnp.arange(N_EXPERTS, dtype=jnp.int32)
    pos = jnp.sum(jnp.where(e[..., None] == ids, seg_start, 0), axis=-1) + rank
    tiles = pos.shape[1] // TOKEN_TILE
    pos = pos.reshape(2, tiles, TOKEN_TILE).transpose(1, 0, 2).reshape(tiles, 1, 2 * TOKEN_TILE)
    tile_start = jnp.arange(n_rows // te, dtype=jnp.int32) * te
    tile_expert = jnp.minimum(jnp.sum(tile_start[:, None] >= seg_end[None, :], axis=1), N_EXPERTS - 1)
    n_valid = (seg_end[-1] // te).reshape(1)
    pad_start = jnp.concatenate([seg_start + cnt, seg_end[-1:]])
    pad_len = jnp.concatenate([padded - cnt, n_rows - seg_end[-1:]])
    return pos, tile_expert.astype(jnp.int32), n_valid.astype(jnp.int32), pad_start, pad_len


def _dispatch_kernel(pad_start_ref, pad_len_ref, pos_ref, x_ref, xs_hbm, zero, ring, sem, zsem):
    tm = x_ref.shape[0]

    @pl.when(pl.program_id(0) == 0)
    def _():
        zero[...] = jnp.zeros_like(zero)

        def each_range(copy_fn):
            def per_range(e, _):
                start = pad_start_ref[e]
                n = pad_len_ref[e]
                runs = n // ZERO_ROWS
                lax.fori_loop(0, runs, lambda j, c: copy_fn(
                    pltpu.make_async_copy(zero, xs_hbm.at[pl.ds(start + j * ZERO_ROWS, ZERO_ROWS)], zsem)), 0)
                lax.fori_loop(runs * ZERO_ROWS, n, lambda j, c: copy_fn(
                    pltpu.make_async_copy(zero.at[0], xs_hbm.at[start + j], zsem)), 0)
                return 0

            lax.fori_loop(0, pad_start_ref.shape[0], per_range, 0)

        def start_copy(cp):
            cp.start()
            return 0

        def wait_copy(cp):
            cp.wait()
            return 0

        each_range(start_copy)
        each_range(wait_copy)

    i = pl.program_id(0)
    n = pl.num_programs(0)
    slot = i % 2

    def drain(s):
        for _ in range(2):
            pltpu.make_async_copy(ring.at[s], xs_hbm.at[pl.ds(0, tm)], sem.at[s]).wait()

    @pl.when(i >= 2)
    def _():
        drain(slot)

    ring[slot] = x_ref[...]

    def issue(t, _):
        for k in range(2):
            pltpu.make_async_copy(ring.at[slot, t], xs_hbm.at[pos_ref[0, 0, k * tm + t]],
                                  sem.at[slot]).start(priority=k)
        return 0

    lax.fori_loop(0, tm, issue, 0, unroll=8)

    @pl.when((i == n - 1) & (n >= 2))
    def _():
        drain(1 - slot)

    @pl.when(i == n - 1)
    def _():
        drain(slot)


def _dispatch(pad_start, pad_len, pos, xp, n_rows):
    t = xp.shape[0]
    tm = TOKEN_TILE
    slab = xp.shape[1:]
    return pl.pallas_call(
        _dispatch_kernel,
        grid_spec=pltpu.PrefetchScalarGridSpec(
            num_scalar_prefetch=2,
            grid=(t // tm,),
            in_specs=[pl.BlockSpec((1, 1, 2 * tm), lambda i, ps, pn: (i, 0, 0), memory_space=pltpu.SMEM),
                      pl.BlockSpec((tm,) + slab, lambda i, ps, pn: (i, 0, 0))],
            out_specs=pl.BlockSpec(memory_space=pl.ANY),
            scratch_shapes=[pltpu.VMEM((ZERO_ROWS,) + slab, xp.dtype), pltpu.VMEM((2, tm) + slab, xp.dtype),
                            pltpu.SemaphoreType.DMA((2,)), pltpu.SemaphoreType.DMA(())],
        ),
        out_shape=jax.ShapeDtypeStruct((n_rows,) + slab, xp.dtype),
        compiler_params=_params("arbitrary"),
        name="moe_dispatch",
    )(pad_start, pad_len, pos, xp)


def _expert_kernel(te_ref, nv_ref, xs_ref, wg_ref, wu_ref, wd_ref, ys_ref, wg16, wu16, wd16):
    i = pl.program_id(0)
    live = i < nv_ref[0]

    @pl.when(live & ((i == 0) | (te_ref[i] != te_ref[jnp.maximum(i - 1, 0)])))
    def _():
        wg16[...] = wg_ref[0].astype(BF16)
        wu16[...] = wu_ref[0].astype(BF16)
        wd16[...] = wd_ref[0].astype(BF16)

    @pl.when(live)
    def _():
        x = _from_slabs(xs_ref[...]).astype(BF16)
        a = _dot(x, wg16[...])
        u = _dot(x, wu16[...])
        hid = (a / (1.0 + jnp.exp(-a))) * u
        ys_ref[...] = _to_slabs(_dot(hid.astype(BF16), wd16[...]))

    @pl.when(i >= nv_ref[0])
    def _():
        ys_ref[...] = jnp.zeros_like(ys_ref)


def _experts(tile_expert, n_valid, xs, wg, wu, wd, layer):
    n_rows = xs.shape[0]
    te = EXPERT_TILE
    base = layer * N_EXPERTS
    last = lambda i, te_ref, nv_ref: jnp.minimum(i, nv_ref[0] - 1)
    slab = (te,) + xs.shape[1:]
    wspec = lambda shape: pl.BlockSpec(
        (1,) + shape, lambda i, te_ref, nv_ref: (base + te_ref[last(i, te_ref, nv_ref)], 0, 0))
    return pl.pallas_call(
        _expert_kernel,
        grid_spec=pltpu.PrefetchScalarGridSpec(
            num_scalar_prefetch=2,
            grid=(n_rows // te,),
            in_specs=[pl.BlockSpec(slab, lambda i, te_ref, nv_ref: (last(i, te_ref, nv_ref), 0, 0)),
                      wspec((D_MODEL, D_EXPERT)), wspec((D_MODEL, D_EXPERT)), wspec((D_EXPERT, D_MODEL))],
            out_specs=pl.BlockSpec(slab, lambda i, te_ref, nv_ref: (i, 0, 0)),
            scratch_shapes=[pltpu.VMEM((D_MODEL, D_EXPERT), BF16), pltpu.VMEM((D_MODEL, D_EXPERT), BF16),
                            pltpu.VMEM((D_EXPERT, D_MODEL), BF16)],
        ),
        out_shape=jax.ShapeDtypeStruct(xs.shape, xs.dtype),
        compiler_params=_params("arbitrary"),
        name="moe_experts",
    )(tile_expert, n_valid, xs, wg, wu, wd)


def _combine_ple_kernel(pos_ref, pos_next_ref, info_ref, h_ref, p_ref, g_ref, wpg_ref, wpe_ref, gf_ref, ys_hbm,
                        o_ref, buf, sem, *, final):
    i = pl.program_id(0)
    tm = h_ref.shape[0]
    slot = i % 2

    def gather(rows_ref, dst):
        def issue(t, _):
            for k in range(2):
                pltpu.make_async_copy(ys_hbm.at[rows_ref[0, 0, k * tm + t]],
                                      buf.at[dst, k * tm + t], sem.at[dst]).start(priority=k)
            return 0

        lax.fori_loop(0, tm, issue, 0, unroll=8)

    @pl.when(i == 0)
    def _():
        gather(pos_ref, 0)

    @pl.when(i + 1 < pl.num_programs(0))
    def _():
        gather(pos_next_ref, 1 - slot)

    info = info_ref[...]
    lane = lax.broadcasted_iota(jnp.int32, info.shape, 1)
    gate1 = jnp.sum(jnp.where(lane == R_W1, info, 0.0), axis=1, keepdims=True)
    gate2 = jnp.sum(jnp.where(lane == R_W2, info, 0.0), axis=1, keepdims=True)
    pltpu.make_async_copy(ys_hbm.at[pl.ds(0, 2 * tm)], buf.at[slot], sem.at[slot]).wait()
    y = gate1 * _from_slabs(buf[slot, 0:tm]) + gate2 * _from_slabs(buf[slot, tm:2 * tm])
    h = h_ref[...] + y
    xn = _rms(h, g_ref[...]).astype(BF16)
    gate = 1.0 / (1.0 + jnp.exp(-_dot(xn, wpg_ref[...])))
    out = h + gate * _dot(p_ref[0].astype(BF16), wpe_ref[...])
    if final:
        out = _rms(out, gf_ref[...])
    o_ref[...] = out


def _combine_ple(pos, info, h, p, layer, g, wpg, wpe, gf, ys, final):
    t = h.shape[0]
    tm = TOKEN_TILE
    n = t // tm
    row = lambda w: pl.BlockSpec((tm, w), lambda i: (i, 0))
    full = lambda a: pl.BlockSpec(a.shape, lambda i: (0,) * a.ndim)
    return pl.pallas_call(
        functools.partial(_combine_ple_kernel, final=final),
        grid=(n,),
        in_specs=[pl.BlockSpec((1, 1, 2 * tm), lambda i: (i, 0, 0), memory_space=pltpu.SMEM),
                  pl.BlockSpec((1, 1, 2 * tm), lambda i: (jnp.minimum(i + 1, n - 1), 0, 0), memory_space=pltpu.SMEM),
                  row(ROUTER_LANES), row(D_MODEL), pl.BlockSpec((1, tm, D_PLE), lambda i: (layer, i, 0)),
                  full(g), full(wpg), full(wpe), full(gf),
                  pl.BlockSpec(memory_space=pl.ANY)],
        out_specs=row(D_MODEL),
        out_shape=jax.ShapeDtypeStruct((t, D_MODEL), F32),
        scratch_shapes=[pltpu.VMEM((2, 2 * tm) + ys.shape[1:], ys.dtype), pltpu.SemaphoreType.DMA((2,))],
        compiler_params=_params("arbitrary"),
        name="moe_combine_ple",
    )(pos, pos, info, h, p, g, wpg, wpe, gf, ys)


def _rot_cols(w):
    half = w.shape[-1] // 2
    return jnp.concatenate([-w[:, half:], w[:, :half]], axis=1)


def _prep_in(w_in):
    sb_scale = SB_HEAD_DIM ** -0.5 * LOG2_E
    kr = w_in[:, 3 * SB_WIDTH + MLA_Q_RANK + MLA_KV_RANK:]
    pad = jnp.zeros((w_in.shape[0], LANES - 2 * MLA_ROPE), w_in.dtype)
    kra = jnp.concatenate([kr, kr, pad], axis=1)
    krr = _rot_cols(kr)
    krb = jnp.concatenate([krr, krr, pad], axis=1)
    w1 = jnp.concatenate([w_in[:, :SB_WIDTH] * sb_scale,
                          w_in[:, SB_WIDTH:3 * SB_WIDTH + MLA_Q_RANK + MLA_KV_RANK], kra, krb], axis=1)
    return w1.astype(BF16)


def _prep_uq(w_uq):
    scale = MLA_QK ** -0.5 * LOG2_E
    w = w_uq.reshape(MLA_Q_RANK, MLA_HEADS, MLA_QK) * scale
    wqn = w[:, :, :MLA_NOPE].reshape(MLA_Q_RANK, MLA_HEADS * MLA_NOPE)
    rope = w[:, :, MLA_NOPE:]
    half = MLA_ROPE // 2
    rot = jnp.concatenate([-rope[:, :, half:], rope[:, :, :half]], axis=2)
    pad = jnp.zeros((MLA_Q_RANK, HEAD_PAIRS, LANES - 2 * MLA_ROPE), w.dtype)

    def pairs(r):
        return jnp.concatenate([r.reshape(MLA_Q_RANK, HEAD_PAIRS, 2 * MLA_ROPE), pad], axis=2).reshape(
            MLA_Q_RANK, HEAD_PAIRS * LANES)

    return wqn.astype(BF16), pairs(rope).astype(BF16), pairs(rot).astype(BF16)


def _prep_ukv(w_ukv):
    w = w_ukv.reshape(MLA_KV_RANK, MLA_HEADS, MLA_NOPE + MLA_V)
    wk = w[:, :, :MLA_NOPE].reshape(MLA_KV_RANK, MLA_HEADS * MLA_NOPE)
    pad = jnp.zeros((MLA_KV_RANK, MLA_HEADS, LANES - MLA_V), w.dtype)
    wv = jnp.concatenate([w[:, :, MLA_NOPE:], pad], axis=2).reshape(MLA_KV_RANK, MLA_HEADS * LANES)
    return wk.astype(BF16), wv.astype(BF16)


def _prep_router(w_rg, b_rg, w_re, b_re):
    pad = ROUTER_LANES - N_GROUPS - N_EXPERTS
    wr = jnp.concatenate([w_rg, w_re, jnp.zeros((D_MODEL, pad), F32)], axis=1)
    br = jnp.concatenate([b_rg, b_re, jnp.zeros((pad,), F32)])[None, :]
    wr_hi = wr.astype(BF16)
    wr_lo = (wr - wr_hi.astype(F32)).astype(BF16)
    return wr_hi, wr_lo, br


def kernel(x, p, positions, g_mix, w_in, g_cq, w_uq, g_ckv, w_ukv, g_osb, g_omla, w_out, g_moe,
           w_rg, b_rg, w_re, b_re, w_gate, w_up, w_down, g_ple, w_pg, w_pe, g_final):
    b, s, d = x.shape
    t = b * s
    depth = w_in.shape[0]

    inv_freq = ROPE_THETA ** (-jnp.arange(0, MLA_ROPE, 2, dtype=F32) / MLA_ROPE)
    ang = positions.astype(F32)[..., None] * inv_freq
    reps = LANES // (MLA_ROPE // 2)
    cos = jnp.tile(jnp.cos(ang), (1, 1, reps)).reshape(t, LANES)
    sin = jnp.tile(jnp.sin(ang), (1, 1, reps)).reshape(t, LANES)

    h = x.reshape(t, d)
    n_rows = -(-(2 * t + N_EXPERTS * (EXPERT_TILE - 1)) // EXPERT_TILE) * EXPERT_TILE
    p_all = p.reshape(depth, t, D_PLE)
    wg_all = w_gate.reshape(depth * N_EXPERTS, D_MODEL, D_EXPERT)
    wu_all = w_up.reshape(depth * N_EXPERTS, D_MODEL, D_EXPERT)
    wd_all = w_down.reshape(depth * N_EXPERTS, D_EXPERT, D_MODEL)
    r3 = lambda a: a.reshape(b, s, a.shape[-1])
    r2 = lambda a: a.reshape(t, a.shape[-1])
    for i in range(depth):
        w1 = _prep_in(w_in[i])
        wqn, wqra, wqrb = _prep_uq(w_uq[i])
        wkvk, wkvv = _prep_ukv(w_ukv[i])
        qsb, ksb, vsb, qn, qr, kn, kr, vm = _mixer_in(
            h, g_mix[i][None], cos, sin, g_cq[i][None], g_ckv[i][None], w1, wqn, wqra, wqrb, wkvk, wkvv)
        osb = _sb_attention(r3(qsb), r3(ksb), r3(vsb))
        omla = _mla_attention(r3(qn), r3(qr), r3(kn), r3(kr), r3(vm))
        wrh, wrl, br = _prep_router(w_rg[i], b_rg[i], w_re[i], b_re[i])
        h1, xp, info, meta, counts = _mixer_out(h, r2(osb), r2(omla), g_osb[i][None], g_omla[i][None],
                                          w_out[i].astype(BF16), g_moe[i][None], wrh, wrl, br)
        pos, tile_expert, n_valid, pad_start, pad_len = _route_plan(meta, counts, n_rows)
        xs = _dispatch(pad_start, pad_len, pos, xp, n_rows)
        ys = _experts(tile_expert, n_valid, xs, wg_all, wu_all, wd_all, i)
        h = _combine_ple(pos, info, h1, p_all, i, g_ple[i][None], w_pg[i].astype(BF16),
                         w_pe[i].astype(BF16), g_final[None], ys, final=(i == depth - 1))
    return h.reshape(b, s, d)
```
